```python
import math
import jax, jax.numpy as jnp
from jax import lax
import numpy as np

D_MODEL = 1024
BATCH = 8
SEQ = 2048
DEPTH = 2
DEC_BATCH = 128
DEC_SEQ = 4
PAST_LEN = 8192
PAGE_SIZE = 128

N_AB = (DEPTH + 1) // 2
N_C = DEPTH // 2
MLA_HEADS = 8
Q_LORA = 384
KV_LORA = 256
NOPE_DIM = 64
ROPE_DIM = 32
QK_DIM = NOPE_DIM + ROPE_DIM
V_DIM = 64
ROPE_THETA = 10000.0
Q_BLOCK = 128
GLA_HEADS = 4
GLA_DK = 128
GLA_DV = 128
GLA_GATE_RANK = 16
GLA_TAU = 16.0
GLA_CHUNK = 64
S5_GROUP = 16
S5_GROUPS = D_MODEL // S5_GROUP
S5_STATE = 64
DT_MIN = 0.001
DT_MAX = 0.1
D_FF = 4 * D_MODEL
PLE_DIM = 256
EPS = 1e-6

IN_SPLITS = (Q_LORA, KV_LORA, ROPE_DIM, GLA_HEADS * GLA_DK, GLA_HEADS * GLA_DK,
             GLA_HEADS * GLA_DV, GLA_GATE_RANK, GLA_HEADS * GLA_DV)
IN_AB = sum(IN_SPLITS)
MIX_AB = MLA_HEADS * V_DIM + GLA_HEADS * GLA_DV

kernel_name = 'mla_gla_s5_hybrid_step'


def rmsnorm(x, g):
    xf = x.astype(jnp.float32)
    y = xf * lax.rsqrt(jnp.mean(xf * xf, axis=-1, keepdims=True) + EPS)
    return (y * g.astype(jnp.float32)).astype(x.dtype)


def rope_tail(x, pos):
    half = ROPE_DIM // 2
    freqs = ROPE_THETA ** (-jnp.arange(half, dtype=jnp.float32) / half)
    ang = pos.astype(jnp.float32)[:, None] * freqs[None, :]
    cos = jnp.cos(ang)[:, None, :]
    sin = jnp.sin(ang)[:, None, :]
    xr = x[..., NOPE_DIM:].astype(jnp.float32)
    x1, x2 = xr[..., :half], xr[..., half:]
    rot = jnp.concatenate([x1 * cos - x2 * sin, x2 * cos + x1 * sin], axis=-1).astype(x.dtype)
    return jnp.concatenate([x[..., :NOPE_DIM], rot], axis=-1)


def mla_queries(c_q, pos, g_qlat, w_uq, g_qh):
    bt, l = c_q.shape[:2]
    q = jnp.einsum('blr,rz->blz', rmsnorm(c_q, g_qlat), w_uq).reshape(bt, l, MLA_HEADS, QK_DIM)
    return rope_tail(rmsnorm(q, g_qh), pos)


def mla_keys_values(c_kv, k_pe, pos, w_ukv, g_kh):
    bt, l = c_kv.shape[:2]
    kv = jnp.einsum('blr,rz->blz', c_kv, w_ukv).reshape(bt, l, MLA_HEADS, NOPE_DIM + V_DIM)
    k_nope, v = kv[..., :NOPE_DIM], kv[..., NOPE_DIM:]
    k_rot = jnp.broadcast_to(k_pe[:, :, None, :], (bt, l, MLA_HEADS, ROPE_DIM)).astype(k_nope.dtype)
    k = jnp.concatenate([k_nope, k_rot], axis=-1)
    return rope_tail(rmsnorm(k, g_kh), pos), v


def mla_prompt_attention(q, k, v):
    bt, s = q.shape[:2]
    nb = s // Q_BLOCK
    scale = QK_DIM ** -0.5
    qb = q.reshape(bt, nb, Q_BLOCK, MLA_HEADS, QK_DIM).swapaxes(0, 1)
    kpos = jnp.arange(s)

    def block(args):
        qi, bi = args
        sc = jnp.einsum('bqhd,bkhd->bhqk', qi, k).astype(jnp.float32) * scale
        qpos = bi * Q_BLOCK + jnp.arange(Q_BLOCK)
        sc = jnp.where(kpos[None, :] <= qpos[:, None], sc, -jnp.inf)
        p = jax.nn.softmax(sc, axis=-1).astype(v.dtype)
        return jnp.einsum('bhqk,bkhv->bqhv', p, v)

    o = lax.map(block, (qb, jnp.arange(nb)))
    return o.swapaxes(0, 1).reshape(bt, s, MLA_HEADS * V_DIM)


def mla_sample_attention(q, k_new, v_new, cache_ckv_l, cache_kpe_l, page_table, w_ukv, g_kh):
    bd, t = q.shape[:2]
    scale = QK_DIM ** -0.5
    n_pages = page_table.shape[1]

    def page_step(carry, xs):
        m, l, acc = carry
        phys, j = xs
        kpos = j * PAGE_SIZE + jnp.arange(PAGE_SIZE)
        k, v = mla_keys_values(cache_ckv_l[phys], cache_kpe_l[phys], kpos, w_ukv, g_kh)
        sc = jnp.einsum('bthd,bkhd->bhtk', q, k).astype(jnp.float32) * scale
        m_new = jnp.maximum(m, sc.max(axis=-1))
        corr = jnp.exp(m - m_new)
        p = jnp.exp(sc - m_new[..., None])
        l = l * corr + p.sum(axis=-1)
        acc = acc * corr[..., None] + jnp.einsum('bhtk,bkhv->bhtv', p, v.astype(jnp.float32))
        return (m_new, l, acc), None

    init = (jnp.full((bd, MLA_HEADS, t), -jnp.inf, jnp.float32),
            jnp.zeros((bd, MLA_HEADS, t), jnp.float32),
            jnp.zeros((bd, MLA_HEADS, t, V_DIM), jnp.float32))
    (m, l, acc), _ = lax.scan(page_step, init, (page_table.T, jnp.arange(n_pages)))
    sc = jnp.einsum('bthd,bkhd->bhtk', q, k_new).astype(jnp.float32) * scale
    causal = jnp.tril(jnp.ones((t, t), dtype=bool))
    sc = jnp.where(causal, sc, -jnp.inf)
    m_new = jnp.maximum(m, sc.max(axis=-1))
    corr = jnp.exp(m - m_new)
    p = jnp.exp(sc - m_new[..., None])
    l = l * corr + p.sum(axis=-1)
    acc = acc * corr[..., None] + jnp.einsum('bhtk,bkhv->bhtv', p, v_new.astype(jnp.float32))
    o = acc / l[..., None]
    return o.transpose(0, 2, 1, 3).reshape(bd, t, MLA_HEADS * V_DIM).astype(q.dtype)


def gla_recurrence(q, k, v, lg, s0):
    bt, l = q.shape[:2]
    c = GLA_CHUNK if l % GLA_CHUNK == 0 else l
    n = l // c

    def chunks(a):
        return a.reshape(bt, n, c, *a.shape[2:]).swapaxes(0, 1)

    causal = jnp.tril(jnp.ones((c, c), dtype=bool))[None, :, :, None, None]

    def step(s, inp):
        qc, kc, vc, gc = inp
        b = jnp.cumsum(gc, axis=1)
        inter = jnp.einsum('bthk,bhkv->bthv', qc * jnp.exp(b), s)
        decay = jnp.exp(jnp.where(causal, b[:, :, None] - b[:, None, :], -jnp.inf))
        att = jnp.einsum('btshk,bshk->bhts', qc[:, :, None] * decay, kc)
        o = inter + jnp.einsum('bhts,bshv->bthv', att, vc)
        bl = b[:, -1]
        s = jnp.exp(bl)[..., None] * s + jnp.einsum('bshk,bshv->bhkv', kc * jnp.exp(bl[:, None] - b), vc)
        return s, o

    s, o = lax.scan(step, s0, (chunks(q), chunks(k), chunks(v), chunks(lg)))
    return o.swapaxes(0, 1).reshape(bt, l, GLA_HEADS, GLA_DV), s


def ab_mixer(n, pos, s0, past, w_in, g_qlat, w_uq, g_kvlat, w_ukv, g_qh, g_kh, w_a2, b_a, g_o, w_out):
    f32 = jnp.float32
    bt, l, _ = n.shape
    z = jnp.einsum('bld,dz->blz', n, w_in)
    idx = [int(i) for i in np.cumsum(IN_SPLITS)[:-1]]
    c_q, c_kv, k_pe, gq, gk, gv, ga, gr = jnp.split(z, idx, axis=-1)
    q = mla_queries(c_q, pos, g_qlat, w_uq, g_qh)
    c_kv = rmsnorm(c_kv, g_kvlat)
    k, v = mla_keys_values(c_kv, k_pe, pos, w_ukv, g_kh)
    if past is None:
        attn = mla_prompt_attention(q, k, v)
    else:
        cache_ckv_l, cache_kpe_l, page_table = past
        attn = mla_sample_attention(q, k, v, cache_ckv_l, cache_kpe_l, page_table, w_ukv, g_kh)
    gq = gq.reshape(bt, l, GLA_HEADS, GLA_DK).astype(f32) * (GLA_DK ** -0.5)
    gk = gk.reshape(bt, l, GLA_HEADS, GLA_DK).astype(f32)
    gv = gv.reshape(bt, l, GLA_HEADS, GLA_DV).astype(f32)
    logit = (jnp.einsum('blr,rz->blz', ga, w_a2) + b_a).astype(f32)
    lg = jax.nn.log_sigmoid(logit).reshape(bt, l, GLA_HEADS, GLA_DK) / GLA_TAU
    o, s = gla_recurrence(gq, gk, gv, lg, s0.astype(f32))
    o = rmsnorm(o, g_o).astype(n.dtype).reshape(bt, l, GLA_HEADS * GLA_DV) * jax.nn.silu(gr)
    y = jnp.einsum('blz,zd->bld', jnp.concatenate([attn, o], axis=-1), w_out)
    return y, c_kv, k_pe, s.astype(s0.dtype)


def s5_combine(e1, e2):
    a1r, a1i, b1r, b1i = e1
    a2r, a2i, b2r, b2i = e2
    return (a2r * a1r - a2i * a1i, a2r * a1i + a2i * a1r,
            a2r * b1r - a2i * b1i + b2r, a2r * b1i + a2i * b1r + b2i)


def s5_mixer(n, x0_re, x0_im, a_re, a_im, log_dt, b_re, b_im, c_re, c_im, d, w_glu):
    f32 = jnp.float32
    bt, l, _ = n.shape
    u = n.astype(f32).reshape(bt, l, S5_GROUPS, S5_GROUP)
    a_re = a_re.astype(f32)
    a_im = a_im.astype(f32)
    dt = jnp.exp(log_dt.astype(f32))[:, None]
    mag = jnp.exp(a_re * dt)
    lr = mag * jnp.cos(a_im * dt)
    li = mag * jnp.sin(a_im * dt)
    den = a_re * a_re + a_im * a_im
    cr = ((lr - 1.0) * a_re + li * a_im) / den
    ci = (li * a_re - (lr - 1.0) * a_im) / den
    b_re = b_re.astype(f32)
    b_im = b_im.astype(f32)
    bbr = cr[..., None] * b_re - ci[..., None] * b_im
    bbi = cr[..., None] * b_im + ci[..., None] * b_re
    bur = jnp.einsum('blgc,gpc->lbgp', u, bbr)
    bui = jnp.einsum('blgc,gpc->lbgp', u, bbi)
    x0r = x0_re.astype(f32)
    x0i = x0_im.astype(f32)
    bur = bur.at[0].add(lr * x0r - li * x0i)
    bui = bui.at[0].add(lr * x0i + li * x0r)
    ar = jnp.broadcast_to(lr, (l, 1, S5_GROUPS, S5_STATE))
    ai = jnp.broadcast_to(li, (l, 1, S5_GROUPS, S5_STATE))
    _, _, xr, xi = lax.associative_scan(s5_combine, (ar, ai, bur, bui), axis=0)
    y = (jnp.einsum('lbgp,gcp->blgc', xr, c_re.astype(f32))
         - jnp.einsum('lbgp,gcp->blgc', xi, c_im.astype(f32)))
    y = y.reshape(bt, l, D_MODEL) + d.astype(f32) * n.astype(f32)
    g = jax.nn.gelu(y).astype(n.dtype)
    zg = jnp.einsum('bld,dz->blz', g, w_glu)
    out = zg[..., :D_MODEL] * jax.nn.sigmoid(zg[..., D_MODEL:])
    return out, xr[-1].astype(x0_re.dtype), xi[-1].astype(x0_im.dtype)


def channel_mixer(h, g, w_up, w_down):
    a = jax.nn.relu(jnp.einsum('bld,df->blf', rmsnorm(h, g), w_up))
    return jnp.einsum('blf,fd->bld', a * a, w_down)


def per_layer_embedding(h, p, g, w_gate, w_proj):
    gate = jax.nn.sigmoid(jnp.einsum('bld,de->ble', rmsnorm(h, g), w_gate))
    return gate * jnp.einsum('blr,rd->bld', p, w_proj)


def setup_inputs(seed: int = 0) -> dict:
    key = jax.random.key(seed)
    ks = iter(jax.random.split(key, 64))
    f32 = jnp.float32

    def nrm(shape, scale):
        return jax.random.normal(next(ks), shape, f32) * scale

    def gain(shape):
        return 1.0 + nrm(shape, 0.02)

    n_pages = PAST_LEN // PAGE_SIZE
    n_used = DEC_BATCH * n_pages
    n_pool = n_used + max(1, n_used // 4)
    page_table = jax.random.permutation(next(ks), n_pool)[:n_used].reshape(DEC_BATCH, n_pages).astype(jnp.int32)
    a_im = jnp.broadcast_to(math.pi * jnp.arange(S5_STATE, dtype=f32), (N_C, S5_GROUPS, S5_STATE))
    a_re = -0.5 * jnp.exp(nrm((N_C, S5_GROUPS, S5_STATE), 0.05))
    log_dt = jax.random.uniform(next(ks), (N_C, S5_GROUPS), f32, math.log(DT_MIN), math.log(DT_MAX))
    return {
        'x_prompt': nrm((BATCH, SEQ, D_MODEL), 1.0),
        'x_sample': nrm((DEC_BATCH, DEC_SEQ, D_MODEL), 1.0),
        'cache_ckv': nrm((N_AB, n_pool, PAGE_SIZE, KV_LORA), 1.0),
        'cache_kpe': nrm((N_AB, n_pool, PAGE_SIZE, ROPE_DIM), 1.0),
        'state_gla': nrm((N_AB, DEC_BATCH, GLA_HEADS, GLA_DK, GLA_DV), 0.1),
        'state_s5_re': nrm((N_C, DEC_BATCH, S5_GROUPS, S5_STATE), 0.1),
        'state_s5_im': nrm((N_C, DEC_BATCH, S5_GROUPS, S5_STATE), 0.1),
        'page_table': page_table,
        'p_prompt': nrm((DEPTH, BATCH, SEQ, PLE_DIM), 1.0),
        'p_sample': nrm((DEPTH, DEC_BATCH, DEC_SEQ, PLE_DIM), 1.0),
        'g_norm_ab': gain((N_AB, D_MODEL)),
        'w_in_ab': nrm((N_AB, D_MODEL, IN_AB), D_MODEL ** -0.5),
        'g_qlat': gain((N_AB, Q_LORA)),
        'w_uq': nrm((N_AB, Q_LORA, MLA_HEADS * QK_DIM), Q_LORA ** -0.5),
        'g_kvlat': gain((N_AB, KV_LORA)),
        'w_ukv': nrm((N_AB, KV_LORA, MLA_HEADS * (NOPE_DIM + V_DIM)), KV_LORA ** -0.5),
        'g_qh': gain((N_AB, QK_DIM)),
        'g_kh': gain((N_AB, QK_DIM)),
        'w_gla_a2': nrm((N_AB, GLA_GATE_RANK, GLA_HEADS * GLA_DK), GLA_GATE_RANK ** -0.5),
        'b_gla_a': nrm((N_AB, GLA_HEADS * GLA_DK), 0.1),
        'g_gla_o': gain((N_AB, GLA_DV)),
        'w_out_ab': nrm((N_AB, MIX_AB, D_MODEL), MIX_AB ** -0.5),
        'g_norm_c': gain((N_C, D_MODEL)),
        's5_a_re': a_re,
        's5_a_im': a_im,
        's5_log_dt': log_dt,
        's5_b_re': nrm((N_C, S5_GROUPS, S5_STATE, S5_GROUP), (2 * S5_GROUP) ** -0.5),
        's5_b_im': nrm((N_C, S5_GROUPS, S5_STATE, S5_GROUP), (2 * S5_GROUP) ** -0.5),
        's5_c_re': nrm((N_C, S5_GROUPS, S5_GROUP, S5_STATE), S5_STATE ** -0.5),
        's5_c_im': nrm((N_C, S5_GROUPS, S5_GROUP, S5_STATE), S5_STATE ** -0.5),
        's5_d': nrm((N_C, D_MODEL), 1.0),
        'w_glu': nrm((N_C, D_MODEL, 2 * D_MODEL), D_MODEL ** -0.5),
        'g_norm_mlp': gain((DEPTH, D_MODEL)),
        'w_up': nrm((DEPTH, D_MODEL, D_FF), D_MODEL ** -0.5),
        'w_down': nrm((DEPTH, D_FF, D_MODEL), D_FF ** -0.5),
        'g_norm_ple': gain((DEPTH, D_MODEL)),
        'w_ple_gate': nrm((DEPTH, D_MODEL, D_MODEL), D_MODEL ** -0.5),
        'w_ple_proj': nrm((DEPTH, PLE_DIM, D_MODEL), PLE_DIM ** -0.5),
    }


def reference(x_prompt, x_sample, cache_ckv, cache_kpe, state_gla, state_s5_re, state_s5_im, page_table,
              p_prompt, p_sample, g_norm_ab, w_in_ab, g_qlat, w_uq, g_kvlat, w_ukv, g_qh, g_kh,
              w_gla_a2, b_gla_a, g_gla_o, w_out_ab, g_norm_c, s5_a_re, s5_a_im, s5_log_dt,
              s5_b_re, s5_b_im, s5_c_re, s5_c_im, s5_d, w_glu, g_norm_mlp, w_up, w_down,
              g_norm_ple, w_ple_gate, w_ple_proj):
    lp = x_prompt.shape[1]
    t = x_sample.shape[1]
    past_len = page_table.shape[1] * PAGE_SIZE
    pos_p = jnp.arange(lp)
    pos_s = past_len + jnp.arange(t)
    hp, hs = x_prompt, x_sample
    ckv_p, kpe_p, gla_p, s5r_p, s5i_p = [], [], [], [], []
    ckv_s, kpe_s, gla_s, s5r_s, s5i_s = [], [], [], [], []
    for i in range(DEPTH):
        j = i // 2
        if i % 2 == 0:
            wts = (w_in_ab[j], g_qlat[j], w_uq[j], g_kvlat[j], w_ukv[j], g_qh[j], g_kh[j],
                   w_gla_a2[j], b_gla_a[j], g_gla_o[j], w_out_ab[j])
            s0 = jnp.zeros((hp.shape[0], GLA_HEADS, GLA_DK, GLA_DV), state_gla.dtype)
            mp, c1, c2, c3 = ab_mixer(rmsnorm(hp, g_norm_ab[j]), pos_p, s0, None, *wts)
            ms, d1, d2, d3 = ab_mixer(rmsnorm(hs, g_norm_ab[j]), pos_s, state_gla[j],
                                      (cache_ckv[j], cache_kpe[j], page_table), *wts)
            ckv_p.append(c1)
            kpe_p.append(c2)
            gla_p.append(c3)
            ckv_s.append(d1)
            kpe_s.append(d2)
            gla_s.append(d3)
        else:
            wts = (s5_a_re[j], s5_a_im[j], s5_log_dt[j], s5_b_re[j], s5_b_im[j],
                   s5_c_re[j], s5_c_im[j], s5_d[j], w_glu[j])
            z0 = jnp.zeros((hp.shape[0], S5_GROUPS, S5_STATE), state_s5_re.dtype)
            mp, c1, c2 = s5_mixer(rmsnorm(hp, g_norm_c[j]), z0, z0, *wts)
            ms, d1, d2 = s5_mixer(rmsnorm(hs, g_norm_c[j]), state_s5_re[j], state_s5_im[j], *wts)
            s5r_p.append(c1)
            s5i_p.append(c2)
            s5r_s.append(d1)
            s5i_s.append(d2)
        hp = hp + mp
        hs = hs + ms
        hp = hp + channel_mixer(hp, g_norm_mlp[i], w_up[i], w_down[i])
        hs = hs + channel_mixer(hs, g_norm_mlp[i], w_up[i], w_down[i])
        hp = hp + per_layer_embedding(hp, p_prompt[i], g_norm_ple[i], w_ple_gate[i], w_ple_proj[i])
        hs = hs + per_layer_embedding(hs, p_sample[i], g_norm_ple[i], w_ple_gate[i], w_ple_proj[i])
    y_prompt, y_sample = hp, hs
    ckv_prompt, kpe_prompt, gla_prompt = jnp.stack(ckv_p), jnp.stack(kpe_p), jnp.stack(gla_p)
    s5re_prompt, s5im_prompt = jnp.stack(s5r_p), jnp.stack(s5i_p)
    ckv_sample, kpe_sample, gla_sample = jnp.stack(ckv_s), jnp.stack(kpe_s), jnp.stack(gla_s)
    s5re_sample, s5im_sample = jnp.stack(s5r_s), jnp.stack(s5i_s)
    return (y_prompt, y_sample, ckv_prompt, kpe_prompt, gla_prompt, s5re_prompt, s5im_prompt,
            ckv_sample, kpe_sample, gla_sample, s5re_sample, s5im_sample)
```

```python
import functools

import jax
import jax.numpy as jnp
from jax import lax
from jax.experimental import pallas as pl
from jax.experimental.pallas import tpu as pltpu

F32 = jnp.float32
BF16 = jnp.bfloat16

D_MODEL = 1024
PAGE_SIZE = 128
MLA_HEADS = 8
Q_LORA = 384
KV_LORA = 256
NOPE_DIM = 64
ROPE_DIM = 32
QK_DIM = NOPE_DIM + ROPE_DIM
V_DIM = 64
ROPE_THETA = 10000.0
GLA_HEADS = 4
GLA_DK = 128
GLA_DV = 128
GLA_GATE_RANK = 16
GLA_TAU = 16.0
GLA_CHUNK = 64
S5_GROUP = 16
S5_GROUPS = D_MODEL // S5_GROUP
S5_STATE = 64
D_FF = 4 * D_MODEL
PLE_DIM = 256
EPS = 1e-6
IN_SPLITS = (Q_LORA, KV_LORA, ROPE_DIM, GLA_HEADS * GLA_DK, GLA_HEADS * GLA_DK,
             GLA_HEADS * GLA_DV, GLA_GATE_RANK, GLA_HEADS * GLA_DV)

LANES = 128
SUBLANES = 8
HEAD_PAD = LANES
VMEM_LIMIT = 56 * 1024 * 1024
S5_SLABS = D_MODEL // LANES
S5_SLAB_STATE = (LANES // S5_GROUP) * S5_STATE
NEG_INF = float("-inf")


def _vmem_spec():
    return pl.BlockSpec(memory_space=pltpu.VMEM)


def _params(*sem):
    return pltpu.CompilerParams(dimension_semantics=sem, vmem_limit_bytes=VMEM_LIMIT)


def _bdot(a, b):
    return jnp.dot(a.astype(BF16), b.astype(BF16), preferred_element_type=F32)


def _bdot_nt(a, b):
    return lax.dot_general(a.astype(BF16), b.astype(BF16), (((1,), (1,)), ((), ())),
                           preferred_element_type=F32)


def _bdot_tn(a, b):
    return lax.dot_general(a.astype(BF16), b.astype(BF16), (((0,), (0,)), ((), ())),
                           preferred_element_type=F32)


def _rms(x, g):
    return x * lax.rsqrt(jnp.mean(x * x, axis=-1, keepdims=True) + EPS) * g


def _sigmoid(x):
    return 1.0 / (1.0 + jnp.exp(-x))


def _rope_table_kernel(cos_ref, sin_ref):
    shape = cos_ref.shape
    half = ROPE_DIM // 2
    per_row = LANES // half
    row = lax.broadcasted_iota(jnp.int32, shape, 0)
    lane = lax.broadcasted_iota(jnp.int32, shape, 1)
    pos = (row * per_row + lane // half).astype(F32)
    freq = jnp.power(F32(ROPE_THETA), -(lane % half).astype(F32) / half)
    ang = pos * freq
    cos_ref[...] = jnp.cos(ang)
    sin_ref[...] = jnp.sin(ang)


def _rope_tables(n_pos):
    half = ROPE_DIM // 2
    per_row = LANES // half
    rows = -(-n_pos // per_row)
    rows = -(-rows // SUBLANES) * SUBLANES
    cos, sin = pl.pallas_call(
        _rope_table_kernel,
        out_shape=(jax.ShapeDtypeStruct((rows, LANES), F32),) * 2,
        name="rope_table",
    )()
    cos = cos.reshape(rows * per_row, half)[:n_pos]
    sin = sin.reshape(rows * per_row, half)[:n_pos]
    return cos, sin


def _front_kernel(x_ref, tc_ref, ts_ref, gn_ref, wcq_ref, wckv_ref, wkpe_ref, wgq_ref, wgk_ref, wgv_ref,
                  wgr_ref, wga_ref, gql_ref, wuq_ref, gkvl_ref, wuk_ref, wuv_ref, gqh_ref, gkh_ref,
                  wa2_ref, ba_ref,
                  q_ref, k_ref, v_ref, ckv_ref, kpe_ref, gq_ref, gk_ref, gv_ref, lg_ref, gr_ref):
    n = _rms(x_ref[...], gn_ref[...])
    nb = n.astype(BF16)
    tc = tc_ref[...]
    ts = ts_ref[...]
    lane = lax.broadcasted_iota(jnp.int32, tc.shape, 1)
    first_half = lane < NOPE_DIM + ROPE_DIM // 2

    def norm_rope(xh, g):
        ss = jnp.sum(xh * xh, axis=-1, keepdims=True) * (1.0 / QK_DIM)
        y = xh * lax.rsqrt(ss + EPS) * g
        partner = jnp.where(first_half, pltpu.roll(y, LANES - ROPE_DIM // 2, 1), pltpu.roll(y, ROPE_DIM // 2, 1))
        return y * tc + partner * ts

    cq = _rms(jnp.dot(nb, wcq_ref[...], preferred_element_type=F32), gql_ref[...])
    q = _bdot(cq, wuq_ref[...])
    ckv = _rms(jnp.dot(nb, wckv_ref[...], preferred_element_type=F32), gkvl_ref[...])
    ckv_ref[...] = ckv
    ckvb = ckv.astype(BF16)
    kn = jnp.dot(ckvb, wuk_ref[...], preferred_element_type=F32)
    v_ref[...] = jnp.dot(ckvb, wuv_ref[...], preferred_element_type=F32).astype(v_ref.dtype)
    kpe = jnp.dot(nb, wkpe_ref[...], preferred_element_type=F32)
    kpe_ref[...] = kpe
    gqh = gqh_ref[...]
    gkh = gkh_ref[...]
    scale = QK_DIM ** -0.5
    for h in range(MLA_HEADS):
        hs = slice(h * HEAD_PAD, (h + 1) * HEAD_PAD)
        q_ref[:, hs] = (norm_rope(q[:, hs], gqh) * scale).astype(q_ref.dtype)
        k_ref[:, hs] = norm_rope(kn[:, hs] + kpe, gkh).astype(k_ref.dtype)

    gq_ref[...] = jnp.dot(nb, wgq_ref[...], preferred_element_type=F32) * (GLA_DK ** -0.5)
    gk_ref[...] = jnp.dot(nb, wgk_ref[...], preferred_element_type=F32)
    gv_ref[...] = jnp.dot(nb, wgv_ref[...], preferred_element_type=F32)
    gr_ref[...] = jnp.dot(nb, wgr_ref[...], preferred_element_type=F32)
    ga = jnp.dot(nb, wga_ref[...], preferred_element_type=F32)
    logit = _bdot(ga, wa2_ref[...]) + ba_ref[...]
    log_sig = jnp.minimum(logit, 0.0) - jnp.log1p(jnp.exp(-jnp.abs(logit)))
    lg_ref[...] = log_sig * (1.0 / GLA_TAU)


def _front_weights(w_in, g_norm, g_qlat, w_uq, g_kvlat, w_ukv, g_qh, g_kh, w_a2, b_a):
    offs = [0]
    for s in IN_SPLITS:
        offs.append(offs[-1] + s)
    cols = [w_in[:, offs[i]:offs[i + 1]] for i in range(len(IN_SPLITS))]
    w_cq, w_ckv, w_kpe, w_gq, w_gk, w_gv, w_ga, w_gr = cols
    pad_head = HEAD_PAD - QK_DIM
    w_kpe_placed = jnp.pad(w_kpe, ((0, 0), (NOPE_DIM, HEAD_PAD - QK_DIM)))
    w_ga_pad = jnp.pad(w_ga, ((0, 0), (0, LANES - GLA_GATE_RANK)))
    w_a2_pad = jnp.pad(w_a2, ((0, LANES - GLA_GATE_RANK), (0, 0)))
    w_uq_pad = jnp.pad(w_uq.reshape(Q_LORA, MLA_HEADS, QK_DIM), ((0, 0), (0, 0), (0, pad_head)))
    w_uq_pad = w_uq_pad.reshape(Q_LORA, MLA_HEADS * HEAD_PAD)
    w_ukv3 = w_ukv.reshape(KV_LORA, MLA_HEADS, NOPE_DIM + V_DIM)
    w_uk = w_ukv3[:, :, :NOPE_DIM]
    w_uv = w_ukv3[:, :, NOPE_DIM:].reshape(KV_LORA, MLA_HEADS * V_DIM)
    w_uk_pad = jnp.pad(w_uk, ((0, 0), (0, 0), (0, HEAD_PAD - NOPE_DIM))).reshape(KV_LORA, MLA_HEADS * HEAD_PAD)
    g_qh_pad = jnp.pad(g_qh, (0, pad_head))[None, :]
    g_kh_pad = jnp.pad(g_kh, (0, pad_head))[None, :]
    bf = lambda a: a.astype(BF16)
    weights = (g_norm[None, :], bf(w_cq), bf(w_ckv), bf(w_kpe_placed), bf(w_gq), bf(w_gk), bf(w_gv), bf(w_gr),
               bf(w_ga_pad), g_qlat[None, :], bf(w_uq_pad), g_kvlat[None, :], bf(w_uk_pad), bf(w_uv),
               g_qh_pad, g_kh_pad, bf(w_a2_pad), b_a[None, :])
    return weights, w_uk, w_uv


def _front(x, tc, ts, weights, *, tm, table_tiles):
    n = x.shape[0]
    hk = GLA_HEADS * GLA_DK
    row = lambda w: pl.BlockSpec((tm, w), lambda i: (i, 0))
    tab = pl.BlockSpec((tm, LANES), lambda i: (i % table_tiles, 0))
    sds = lambda w, dt: jax.ShapeDtypeStruct((n, w), dt)
    out_shape = (sds(MLA_HEADS * HEAD_PAD, BF16), sds(MLA_HEADS * HEAD_PAD, BF16), sds(MLA_HEADS * V_DIM, BF16),
                 sds(KV_LORA, F32), sds(LANES, F32), sds(hk, F32), sds(hk, F32), sds(hk, F32), sds(hk, F32),
                 sds(hk, F32))
    out_specs = (row(MLA_HEADS * HEAD_PAD), row(MLA_HEADS * HEAD_PAD), row(MLA_HEADS * V_DIM), row(KV_LORA),
                 row(LANES), row(hk), row(hk), row(hk), row(hk), row(hk))
    return pl.pallas_call(
        _front_kernel,
        grid=(n // tm,),
        in_specs=[row(D_MODEL), tab, tab] + [_vmem_spec()] * len(weights),
        out_specs=out_specs,
        out_shape=out_shape,
        compiler_params=_params("parallel"),
        name="ab_front",
    )(x, tc, ts, *weights)


def _prompt_attn_kernel(q_ref, k_ref, v_ref, o_ref, *, tq):
    i = pl.program_id(1)
    row = lax.broadcasted_iota(jnp.int32, (tq, tq), 0)
    col = lax.broadcasted_iota(jnp.int32, (tq, tq), 1)
    causal = col <= row
    lane = lax.broadcasted_iota(jnp.int32, (tq, LANES), 1)
    for pair in range(MLA_HEADS // 2):
        vs = slice(pair * LANES, (pair + 1) * LANES)
        outs = []
        for h in (2 * pair, 2 * pair + 1):
            hs = slice(h * HEAD_PAD, (h + 1) * HEAD_PAD)
            qh = q_ref[:, hs]

            def step(j, carry, masked, hs=hs, qh=qh, vs=vs):
                m, l, acc = carry
                start = pl.multiple_of(j * tq, tq)
                kj = k_ref[pl.ds(start, tq), hs]
                vj = v_ref[pl.ds(start, tq), vs]
                s = lax.dot_general(qh, kj, (((1,), (1,)), ((), ())), preferred_element_type=F32)
                if masked:
                    s = jnp.where(causal, s, NEG_INF)
                m_new = jnp.maximum(m, jnp.max(s, axis=-1, keepdims=True))
                alpha = jnp.exp(m - m_new)
                p = jnp.exp(s - m_new)
                l = alpha * l + jnp.sum(p, axis=-1, keepdims=True)
                acc = alpha * acc + jnp.dot(p.astype(BF16), vj, preferred_element_type=F32)
                return m_new, l, acc

            init = (jnp.full((tq, 1), NEG_INF, F32), jnp.zeros((tq, 1), F32), jnp.zeros((tq, LANES), F32))
            carry = lax.fori_loop(0, i, functools.partial(step, masked=False), init)
            m, l, acc = step(i, carry, True)
            outs.append(acc / l)
        o_ref[:, vs] = jnp.where(lane < V_DIM, outs[0], outs[1]).astype(o_ref.dtype)


def _prompt_attn(q, k, v, *, batch, seq, tq):
    nq = seq // tq
    return pl.pallas_call(
        functools.partial(_prompt_attn_kernel, tq=tq),
        grid=(batch, nq),
        in_specs=[pl.BlockSpec((tq, MLA_HEADS * HEAD_PAD), lambda b, i: (b * nq + i, 0)),
                  pl.BlockSpec((seq, MLA_HEADS * HEAD_PAD), lambda b, i: (b, 0)),
                  pl.BlockSpec((seq, MLA_HEADS * V_DIM), lambda b, i: (b, 0))],
        out_specs=pl.BlockSpec((tq, MLA_HEADS * V_DIM), lambda b, i: (b * nq + i, 0)),
        out_shape=jax.ShapeDtypeStruct((batch * seq, MLA_HEADS * V_DIM), BF16),
        compiler_params=_params("parallel", "arbitrary"),
        name="mla_prompt_attn",
    )(q, k, v)


def _gla_intra(q, k, b):
    c = q.shape[0]
    row = lax.broadcasted_iota(jnp.int32, q.shape, 0)
    r2 = lax.broadcasted_iota(jnp.int32, (c, c), 0)
    c2 = lax.broadcasted_iota(jnp.int32, (c, c), 1)
    att = jnp.where(r2 == c2, _bdot_nt(q, k), 0.0)
    before = pltpu.roll(b, 1, 0)
    last = b
    m = 1
    while m < c:
        upper = (row // m) % 2 == 1
        qm = jnp.where(upper, q * jnp.exp(jnp.minimum(b - before, 0.0)), 0.0)
        km = jnp.where(upper, 0.0, k * jnp.exp(jnp.minimum(last - b, 0.0)))
        same = (r2 // (2 * m)) == (c2 // (2 * m))
        att = att + jnp.where(same, _bdot_nt(qm, km), 0.0)
        if 2 * m < c:
            before = jnp.where(upper, pltpu.roll(before, m, 0), before)
            last = jnp.where(upper, last, pltpu.roll(last, c - m, 0))
        m *= 2
    return att


def _cumsum_rows(g):
    c = g.shape[0]
    row = lax.broadcasted_iota(jnp.int32, g.shape, 0)
    b = g
    sh = 1
    while sh < c:
        b = b + jnp.where(row >= sh, pltpu.roll(b, sh, 0), 0.0)
        sh *= 2
    return b


def _gla_prompt_kernel(gq_ref, gk_ref, gv_ref, lg_ref, gr_ref, go_ref, o_ref, st_ref, state_ref, *, n_chunks):
    t = pl.program_id(1)

    @pl.when(t == 0)
    def _():
        state_ref[...] = jnp.zeros_like(state_ref)

    go = go_ref[...]

    def chunk(ci, carry):
        rows = pl.ds(pl.multiple_of(ci * GLA_CHUNK, GLA_CHUNK), GLA_CHUNK)
        for h in range(GLA_HEADS):
            hs = slice(h * GLA_DK, (h + 1) * GLA_DK)
            q = gq_ref[rows, hs]
            k = gk_ref[rows, hs]
            v = gv_ref[rows, hs]
            b = _cumsum_rows(lg_ref[rows, hs])
            st = state_ref[h]
            inter = _bdot_nt(q * jnp.exp(b), st)
            o = inter + _bdot(_gla_intra(q, k, b), v)
            bl = b[GLA_CHUNK - 1:GLA_CHUNK, :]
            state_ref[h] = st * jnp.exp(bl) + _bdot_tn(v, k * jnp.exp(bl - b))
            gr = gr_ref[rows, hs]
            o_ref[rows, hs] = (_rms(o, go) * (gr * _sigmoid(gr))).astype(o_ref.dtype)
        return carry

    lax.fori_loop(0, n_chunks, chunk, 0)

    @pl.when(t == pl.num_programs(1) - 1)
    def _():
        st_ref[0] = state_ref[...]


def _gla_prompt(gq, gk, gv, lg, gr, g_o, *, batch, seq, tg):
    nt = seq // tg
    hk = GLA_HEADS * GLA_DK
    row = pl.BlockSpec((tg, hk), lambda b, t: (b * nt + t, 0))
    return pl.pallas_call(
        functools.partial(_gla_prompt_kernel, n_chunks=tg // GLA_CHUNK),
        grid=(batch, nt),
        in_specs=[row, row, row, row, row, _vmem_spec()],
        out_specs=(row, pl.BlockSpec((1, GLA_HEADS, GLA_DV, GLA_DK), lambda b, t: (b, 0, 0, 0))),
        out_shape=(jax.ShapeDtypeStruct((batch * seq, hk), BF16),
                   jax.ShapeDtypeStruct((batch, GLA_HEADS, GLA_DV, GLA_DK), F32)),
        scratch_shapes=[pltpu.VMEM((GLA_HEADS, GLA_DV, GLA_DK), F32)],
        compiler_params=_params("parallel", "arbitrary"),
        name="gla_prompt",
    )(gq, gk, gv, lg, gr, g_o[None, :])


def _gla_sample_kernel(gq_ref, gk_ref, gv_ref, lg_ref, gr_ref, go_ref, s0_ref, o_ref, s1_ref, oacc_ref,
                       *, nb, t_len):
    rows = nb * t_len
    pad = jnp.zeros((LANES - rows, LANES), F32)

    def cols(x):
        return jnp.transpose(jnp.concatenate([x, pad], axis=0))

    for h in range(GLA_HEADS):
        hs = slice(h * GLA_DK, (h + 1) * GLA_DK)
        qt = cols(gq_ref[:, hs])
        kt = cols(gk_ref[:, hs])
        at = cols(jnp.exp(lg_ref[:, hs]))
        for bi in range(nb):
            s = s0_ref[bi, h]
            for t in range(t_len):
                r = bi * t_len + t
                v = gv_ref[r:r + 1, hs]
                s = s * at[:, r:r + 1] + kt[:, r:r + 1] * v
                oacc_ref[r:r + 1, hs] = jnp.sum(s * qt[:, r:r + 1], axis=0, keepdims=True)
            s1_ref[bi, h] = s
    go = go_ref[...]
    for h in range(GLA_HEADS):
        hs = slice(h * GLA_DK, (h + 1) * GLA_DK)
        gr = gr_ref[:, hs]
        o_ref[:, hs] = (_rms(oacc_ref[:, hs], go) * (gr * _sigmoid(gr))).astype(o_ref.dtype)


def _gla_sample(gq, gk, gv, lg, gr, g_o, s0, *, t_len, nb):
    n = gq.shape[0]
    batch = n // t_len
    hk = GLA_HEADS * GLA_DK
    rows = nb * t_len
    row = pl.BlockSpec((rows, hk), lambda i: (i, 0))
    st = pl.BlockSpec((nb, GLA_HEADS, GLA_DK, GLA_DV), lambda i: (i, 0, 0, 0))
    return pl.pallas_call(
        functools.partial(_gla_sample_kernel, nb=nb, t_len=t_len),
        grid=(batch // nb,),
        in_specs=[row, row, row, row, row, _vmem_spec(), st],
        out_specs=(row, st),
        out_shape=(jax.ShapeDtypeStruct((n, hk), BF16), jax.ShapeDtypeStruct(s0.shape, s0.dtype)),
        scratch_shapes=[pltpu.VMEM((rows, hk), F32)],
        compiler_params=_params("parallel"),
        name="gla_sample",
    )(gq, gk, gv, lg, gr, g_o[None, :], s0)


def _sample_attn_kernel(pt_ref, q_ref, cnew_ref, pnew_ref, kc_ref, ks_ref, kcn_ref, ksn_ref, gk_ref, gpe_ref,
                        wukt_ref, wukp_ref, wuv_ref, *rest, n_pages, chunk_pages, t_len):
    del pt_ref
    ckv_refs = rest[:n_pages]
    kpe_refs = rest[n_pages:2 * n_pages]
    o_ref = rest[2 * n_pages]
    s_scr, c_scr = rest[2 * n_pages + 1:]
    n_cols = t_len * MLA_HEADS
    chunk = chunk_pages * PAGE_SIZE
    n_chunks = n_pages // chunk_pages
    past = n_pages * PAGE_SIZE

    qg = q_ref[...].astype(F32) * gk_ref[...]
    gpe = gpe_ref[...]
    sub = lax.broadcasted_iota(jnp.int32, (SUBLANES, MLA_HEADS * HEAD_PAD), 0)
    lane_head = lax.broadcasted_iota(jnp.int32, (SUBLANES, MLA_HEADS * HEAD_PAD), 1) // HEAD_PAD
    qexp = jnp.concatenate(
        [jnp.where(lane_head == sub, jnp.broadcast_to(qg[t:t + 1, :], sub.shape), 0.0) for t in range(t_len)]
        + [jnp.zeros((LANES - n_cols, MLA_HEADS * HEAD_PAD), F32)], axis=0)
    qt = jnp.transpose(_bdot(qexp, wukt_ref[...]))
    qn = qt[:KV_LORA].astype(BF16)
    qr = qt[KV_LORA:].astype(BF16)
    wukp = wukp_ref[...]
    lane_pad = jnp.zeros((1, LANES - ROPE_DIM), F32)

    def scores(c, kpe, kcos, ksin):
        cb = c.astype(BF16)
        kn = jnp.dot(cb, wukp, preferred_element_type=F32)
        ksq = kn * kn
        a = ksq[:, :LANES] + ksq[:, LANES:2 * LANES] + ksq[:, 2 * LANES:3 * LANES] + ksq[:, 3 * LANES:]
        sh = LANES // 2
        while sh >= MLA_HEADS:
            a = a + pltpu.roll(a, sh, 1)
            sh //= 2
        ss = a + jnp.sum(kpe * kpe, axis=-1, keepdims=True)
        rs = lax.rsqrt(ss * (1.0 / QK_DIM) + EPS)
        half = ROPE_DIM // 2
        kg = kpe * gpe
        partner = jnp.concatenate([kg[:, half:], kg[:, :half]], axis=1)
        kr = kg * kcos + partner * ksin
        kr = jnp.concatenate([kr, jnp.broadcast_to(lane_pad, (kr.shape[0], LANES - ROPE_DIM))], axis=1)
        s = jnp.dot(cb, qn, preferred_element_type=F32) + jnp.dot(kr.astype(BF16), qr, preferred_element_type=F32)
        return cb, s * rs

    m = jnp.full((1, LANES), NEG_INF, F32)
    for ci in range(n_chunks):
        pages = range(ci * chunk_pages, (ci + 1) * chunk_pages)
        c = jnp.concatenate([ckv_refs[p][...] for p in pages], axis=0)
        kpe = jnp.concatenate([kpe_refs[p][...] for p in pages], axis=0)
        rows = slice(ci * chunk, (ci + 1) * chunk)
        cb, s = scores(c, kpe, kc_ref[rows, :], ks_ref[rows, :])
        c_scr[rows, :] = cb
        s_scr[rows, :] = s
        m = jnp.maximum(m, jnp.max(s, axis=0, keepdims=True))

    zpad = lambda a: jnp.concatenate([a, jnp.zeros((PAGE_SIZE - t_len, a.shape[1]), a.dtype)], axis=0)
    cb, s = scores(zpad(cnew_ref[...]), zpad(pnew_ref[...]), kcn_ref[...], ksn_ref[...])
    r2 = lax.broadcasted_iota(jnp.int32, (PAGE_SIZE, LANES), 0)
    c2 = lax.broadcasted_iota(jnp.int32, (PAGE_SIZE, LANES), 1)
    s = jnp.where((r2 < t_len) & (r2 <= c2 // MLA_HEADS), s, NEG_INF)
    new_rows = slice(past, past + PAGE_SIZE)
    c_scr[new_rows, :] = cb
    s_scr[new_rows, :] = s
    m = jnp.maximum(m, jnp.max(s, axis=0, keepdims=True))

    l = jnp.zeros((1, LANES), F32)
    acc = jnp.zeros((n_cols, KV_LORA), F32)
    bounds = [(ci * chunk, chunk) for ci in range(n_chunks)] + [(past, PAGE_SIZE)]
    for start, size in bounds:
        rows = slice(start, start + size)
        p = jnp.exp(s_scr[rows, :] - m)
        l = l + jnp.sum(p, axis=0, keepdims=True)
        pt = jnp.transpose(p)[:n_cols].astype(BF16)
        acc = acc + jnp.dot(pt, c_scr[rows, :], preferred_element_type=F32)

    lcol = jnp.transpose(jnp.broadcast_to(l, (LANES, LANES)))[:n_cols, :1]
    lat = acc / lcol
    res = _bdot(lat, wuv_ref[...])
    r3 = lax.broadcasted_iota(jnp.int32, res.shape, 0)
    c3 = lax.broadcasted_iota(jnp.int32, res.shape, 1)
    res = jnp.where(c3 // V_DIM == r3 % MLA_HEADS, res, 0.0)
    o_ref[...] = jnp.sum(res.reshape(t_len, MLA_HEADS, MLA_HEADS * V_DIM), axis=1).astype(o_ref.dtype)


def _sample_attn(q, ckv_new, kpe_new, cache_ckv, cache_kpe, page_table, kc, ks, g_kh, w_uk, w_uv,
                 *, t_len, chunk_pages):
    batch, n_pages = page_table.shape
    past = n_pages * PAGE_SIZE
    w_ukt = jnp.pad(jnp.transpose(w_uk, (1, 2, 0)), ((0, 0), (0, HEAD_PAD - NOPE_DIM), (0, 0)))
    w_ukt = w_ukt.reshape(MLA_HEADS * HEAD_PAD, KV_LORA)
    sel = jnp.pad(jnp.eye(ROPE_DIM, dtype=F32), ((NOPE_DIM, HEAD_PAD - QK_DIM), (0, LANES - ROPE_DIM)))
    w_ukt_ext = jnp.concatenate([w_ukt, jnp.tile(sel, (MLA_HEADS, 1))], axis=1).astype(BF16)
    w_uk_perm = jnp.transpose(w_uk, (0, 2, 1)).reshape(KV_LORA, NOPE_DIM * MLA_HEADS).astype(BF16)
    g_k_pad = jnp.concatenate([g_kh[:NOPE_DIM], jnp.ones((ROPE_DIM,), F32), jnp.zeros((HEAD_PAD - QK_DIM,), F32)])
    g_k_pad = jnp.tile(g_k_pad, MLA_HEADS)[None, :]
    g_pe = g_kh[NOPE_DIM:][None, :]
    kcn = jnp.pad(kc[past:past + t_len], ((0, PAGE_SIZE - t_len), (0, 0)))
    ksn = jnp.pad(ks[past:past + t_len], ((0, PAGE_SIZE - t_len), (0, 0)))
    per_b = lambda w: pl.BlockSpec((None, t_len, w), lambda b, pt: (b, 0, 0))
    page_spec = lambda w, i: pl.BlockSpec((None, PAGE_SIZE, w), lambda b, pt, i=i: (pt[b, i], 0, 0))
    in_specs = ([per_b(MLA_HEADS * HEAD_PAD), per_b(KV_LORA), per_b(ROPE_DIM)] + [_vmem_spec()] * 9
                + [page_spec(KV_LORA, i) for i in range(n_pages)]
                + [page_spec(ROPE_DIM, i) for i in range(n_pages)])
    grid_spec = pltpu.PrefetchScalarGridSpec(
        num_scalar_prefetch=1,
        grid=(batch,),
        in_specs=in_specs,
        out_specs=pl.BlockSpec((None, t_len, MLA_HEADS * V_DIM), lambda b, pt: (b, 0, 0)),
        scratch_shapes=[pltpu.VMEM((past + PAGE_SIZE, LANES), F32), pltpu.VMEM((past + PAGE_SIZE, KV_LORA), BF16)],
    )
    out = pl.pallas_call(
        functools.partial(_sample_attn_kernel, n_pages=n_pages, chunk_pages=chunk_pages, t_len=t_len),
        grid_spec=grid_spec,
        out_shape=jax.ShapeDtypeStruct((batch, t_len, MLA_HEADS * V_DIM), BF16),
        compiler_params=_params("arbitrary"),
        name="mla_sample_attn",
    )(page_table, q.reshape(batch, t_len, -1), ckv_new.reshape(batch, t_len, -1),
      kpe_new.reshape(batch, t_len, -1), kc[:past], ks[:past], kcn, ksn, g_k_pad, g_pe,
      w_ukt_ext, w_uk_perm, w_uv.astype(BF16), *([cache_ckv] * n_pages), *([cache_kpe] * n_pages))
    return out.reshape(batch * t_len, MLA_HEADS * V_DIM)


def _mlp_ple(h, p_ref, gm_ref, wup_ref, wdown_ref, gp_ref, wgate_ref, wproj_ref, o_ref, ff_chunk):
    nb = _rms(h, gm_ref[...]).astype(BF16)
    y = jnp.zeros_like(h)
    for c in range(D_FF // ff_chunk):
        cs = slice(c * ff_chunk, (c + 1) * ff_chunk)
        a = jnp.maximum(jnp.dot(nb, wup_ref[:, cs], preferred_element_type=F32), 0.0)
        y = y + jnp.dot((a * a).astype(BF16), wdown_ref[cs, :], preferred_element_type=F32)
    h = h + y
    gate = _sigmoid(_bdot(_rms(h, gp_ref[...]), wgate_ref[...]))
    o_ref[...] = h + gate * _bdot(p_ref[...], wproj_ref[...])


def _post_mix_kernel(h_ref, a_ref, o2_ref, p_ref, wout_ref, gm_ref, wup_ref, wdown_ref, gp_ref, wgate_ref,
                     wproj_ref, o_ref, *, ff_chunk):
    n_attn = MLA_HEADS * V_DIM
    h = (h_ref[...] + jnp.dot(a_ref[...], wout_ref[:n_attn, :], preferred_element_type=F32)
         + jnp.dot(o2_ref[...], wout_ref[n_attn:, :], preferred_element_type=F32))
    _mlp_ple(h, p_ref, gm_ref, wup_ref, wdown_ref, gp_ref, wgate_ref, wproj_ref, o_ref, ff_chunk)


def _post_kernel(h_ref, p_ref, gm_ref, wup_ref, wdown_ref, gp_ref, wgate_ref, wproj_ref, o_ref, *, ff_chunk):
    _mlp_ple(h_ref[...], p_ref, gm_ref, wup_ref, wdown_ref, gp_ref, wgate_ref, wproj_ref, o_ref, ff_chunk)


def _post(h, p, weights, mix=None, *, tm, ff_chunk=1024):
    n = h.shape[0]
    row = lambda w: pl.BlockSpec((tm, w), lambda i: (i, 0))
    if mix is None:
        body, acts, specs = _post_kernel, (h, p), [row(D_MODEL), row(PLE_DIM)]
    else:
        attn, o_gla, w_out = mix
        body, acts = _post_mix_kernel, (h, attn, o_gla, p)
        specs = [row(D_MODEL), row(attn.shape[1]), row(o_gla.shape[1]), row(PLE_DIM)]
        weights = (w_out,) + tuple(weights)
    return pl.pallas_call(
        functools.partial(body, ff_chunk=ff_chunk),
        grid=(n // tm,),
        in_specs=specs + [_vmem_spec()] * len(weights),
        out_specs=row(D_MODEL),
        out_shape=jax.ShapeDtypeStruct((n, D_MODEL), F32),
        compiler_params=_params("parallel"),
        name="post_mix" if mix is not None else "post",
    )(*acts, *weights)


def _s5_prep_kernel(are_ref, aim_ref, ldt_ref, bre_ref, bim_ref, lr_ref, li_ref, bbr_ref, bbi_ref):
    a_re = are_ref[...]
    a_im = aim_ref[...]
    dt = jnp.exp(ldt_ref[...])
    mag = jnp.exp(a_re * dt)
    lr = mag * jnp.cos(a_im * dt)
    li = mag * jnp.sin(a_im * dt)
    den = a_re * a_re + a_im * a_im
    cr = ((lr - 1.0) * a_re + li * a_im) / den
    ci = (li * a_re - (lr - 1.0) * a_im) / den
    lr_ref[...] = lr
    li_ref[...] = li
    b_re = bre_ref[...]
    b_im = bim_ref[...]
    bbr_ref[...] = cr[:, None, :] * b_re - ci[:, None, :] * b_im
    bbi_ref[...] = cr[:, None, :] * b_im + ci[:, None, :] * b_re


def _s5_weights(a_re, a_im, log_dt, b_re, b_im, c_re, c_im):
    g, p_, ch = S5_GROUPS, S5_STATE, S5_GROUP
    gp = jax.ShapeDtypeStruct((g, p_), F32)
    gcp = jax.ShapeDtypeStruct((g, ch, p_), F32)
    lr, li, bbr, bbi = pl.pallas_call(
        _s5_prep_kernel, out_shape=(gp, gp, gcp, gcp), name="s5_prep",
    )(a_re, a_im, log_dt[:, None], jnp.transpose(b_re, (0, 2, 1)), jnp.transpose(b_im, (0, 2, 1)))
    per_slab = LANES // ch
    eye = jnp.eye(per_slab, dtype=F32)

    def b_blocks(bb):
        bb = bb.reshape(S5_SLABS, per_slab, ch, p_)
        return jnp.einsum('jgcp,gh->jgchp', bb, eye).reshape(S5_SLABS, LANES, S5_SLAB_STATE)

    def c_blocks(cc):
        cc = cc.reshape(S5_SLABS, per_slab, ch, p_)
        return jnp.einsum('jgcp,gh->jgphc', cc, eye).reshape(S5_SLABS, S5_SLAB_STATE, LANES)

    w_b = jnp.concatenate([b_blocks(bbr), b_blocks(bbi)], axis=2).astype(BF16)
    w_c = jnp.concatenate([c_blocks(c_re), -c_blocks(c_im)], axis=1).astype(BF16)
    lam = jnp.stack([lr.reshape(S5_SLABS, S5_SLAB_STATE), li.reshape(S5_SLABS, S5_SLAB_STATE)], axis=1)
    return w_b, w_c, lam


def _s5_kernel(x_ref, x0r_ref, x0i_ref, gn_ref, wb_ref, wc_ref, lam_ref, d_ref, wglu_ref,
               o_ref, xr_ref, xi_ref, bur_ref, bui_ref, sr_ref, si_ref, *, nb, t_len):
    step = pl.program_id(0)
    ns = S5_SLAB_STATE

    @pl.when(step == 0)
    def _():
        for j in range(S5_SLABS):
            sr_ref[j] = x0r_ref[:, j * ns:(j + 1) * ns]
            si_ref[j] = x0i_ref[:, j * ns:(j + 1) * ns]

    x = x_ref[...]
    u = _rms(x, gn_ref[...])
    ub = u.astype(BF16)
    for j in range(S5_SLABS):
        bu = jnp.dot(ub[:, j * LANES:(j + 1) * LANES], wb_ref[j], preferred_element_type=F32)
        bur_ref[j] = bu[:, :ns]
        bui_ref[j] = bu[:, ns:]

    n_groups = nb // SUBLANES

    def slab(j, carry):
        lr = jnp.broadcast_to(lam_ref[j, 0:1, :], (SUBLANES, ns))
        li = jnp.broadcast_to(lam_ref[j, 1:2, :], (SUBLANES, ns))

        def group(gi, carry):
            srow = pl.ds(pl.multiple_of(gi * SUBLANES, SUBLANES), SUBLANES)
            xr0 = sr_ref[j, srow, :]
            xi0 = si_ref[j, srow, :]

            def tok(t, st):
                xr, xi = st
                rows = pl.ds(pl.multiple_of(t * nb + gi * SUBLANES, SUBLANES), SUBLANES)
                nr = lr * xr - li * xi + bur_ref[j, rows, :]
                ni = lr * xi + li * xr + bui_ref[j, rows, :]
                bur_ref[j, rows, :] = nr
                bui_ref[j, rows, :] = ni
                return nr, ni

            xr, xi = lax.fori_loop(0, t_len, tok, (xr0, xi0))
            sr_ref[j, srow, :] = xr
            si_ref[j, srow, :] = xi
            return carry

        return lax.fori_loop(0, n_groups, group, carry)

    lax.fori_loop(0, S5_SLABS, slab, 0)

    ys = []
    for j in range(S5_SLABS):
        xs = jnp.concatenate([bur_ref[j].astype(BF16), bui_ref[j].astype(BF16)], axis=1)
        ys.append(jnp.dot(xs, wc_ref[j], preferred_element_type=F32))
    y = jnp.concatenate(ys, axis=1) + d_ref[...] * u
    gl = jax.nn.gelu(y)
    zg = _bdot(gl, wglu_ref[...])
    o_ref[...] = x + zg[:, :D_MODEL] * _sigmoid(zg[:, D_MODEL:])

    @pl.when(step == pl.num_programs(0) - 1)
    def _():
        for j in range(S5_SLABS):
            xr_ref[:, j * ns:(j + 1) * ns] = sr_ref[j]
            xi_ref[:, j * ns:(j + 1) * ns] = si_ref[j]


def _s5(x_tm, x0r, x0i, g_norm, w_b, w_c, lam, d, w_glu, *, nb, t_tile):
    n = x_tm.shape[0]
    rows = nb * t_tile
    n_state = S5_GROUPS * S5_STATE
    full = lambda shape: pl.BlockSpec(shape, lambda i: (0,) * len(shape))
    st = jax.ShapeDtypeStruct((nb, n_state), F32)
    return pl.pallas_call(
        functools.partial(_s5_kernel, nb=nb, t_len=t_tile),
        grid=(n // rows,),
        in_specs=[pl.BlockSpec((rows, D_MODEL), lambda i: (i, 0)), full((nb, n_state)), full((nb, n_state))]
        + [_vmem_spec()] * 6,
        out_specs=(pl.BlockSpec((rows, D_MODEL), lambda i: (i, 0)), full((nb, n_state)), full((nb, n_state))),
        out_shape=(jax.ShapeDtypeStruct((n, D_MODEL), F32), st, st),
        scratch_shapes=[pltpu.VMEM((S5_SLABS, rows, S5_SLAB_STATE), F32),
                        pltpu.VMEM((S5_SLABS, rows, S5_SLAB_STATE), F32),
                        pltpu.VMEM((S5_SLABS, nb, S5_SLAB_STATE), F32),
                        pltpu.VMEM((S5_SLABS, nb, S5_SLAB_STATE), F32)],
        compiler_params=_params("arbitrary"),
        name="s5_layer",
    )(x_tm, x0r, x0i, g_norm[None, :], w_b, w_c, lam, d[None, :], w_glu.astype(BF16))


def _to_time_major(h, batch, seq):
    return h.reshape(batch, seq, -1).transpose(1, 0, 2).reshape(batch * seq, -1)


def _to_batch_major(h, batch, seq):
    return h.reshape(seq, batch, -1).transpose(1, 0, 2).reshape(batch * seq, -1)


def kernel(x_prompt, x_sample, cache_ckv, cache_kpe, state_gla, state_s5_re, state_s5_im, page_table,
           p_prompt, p_sample, g_norm_ab, w_in_ab, g_qlat, w_uq, g_kvlat, w_ukv, g_qh, g_kh,
           w_gla_a2, b_gla_a, g_gla_o, w_out_ab, g_norm_c, s5_a_re, s5_a_im, s5_log_dt,
           s5_b_re, s5_b_im, s5_c_re, s5_c_im, s5_d, w_glu, g_norm_mlp, w_up, w_down,
           g_norm_ple, w_ple_gate, w_ple_proj):
    bp, sp, _ = x_prompt.shape
    bs, ts_len, _ = x_sample.shape
    n_pages = page_table.shape[1]
    past = n_pages * PAGE_SIZE
    n_p, n_s = bp * sp, bs * ts_len
    bf = lambda a: a.astype(BF16)

    cos16, sin16 = _rope_tables(past + ts_len)
    ones = jnp.ones((cos16.shape[0], 1), F32)
    tc = jnp.concatenate([ones * jnp.ones((1, NOPE_DIM), F32), cos16, cos16,
                          ones * jnp.ones((1, HEAD_PAD - QK_DIM), F32)], axis=1)
    ts = jnp.concatenate([ones * jnp.zeros((1, NOPE_DIM), F32), -sin16, sin16,
                          ones * jnp.zeros((1, HEAD_PAD - QK_DIM), F32)], axis=1)
    kc = jnp.concatenate([cos16, cos16], axis=1)
    ks = jnp.concatenate([-sin16, sin16], axis=1)
    tc_s = jnp.tile(tc[past:past + ts_len], (bs, 1))
    ts_s = jnp.tile(ts[past:past + ts_len], (bs, 1))

    post_w = lambda i: (g_norm_mlp[i][None, :], bf(w_up[i]), bf(w_down[i]), g_norm_ple[i][None, :],
                        bf(w_ple_gate[i]), bf(w_ple_proj[i]))

    fw, w_uk, w_uv = _front_weights(w_in_ab[0], g_norm_ab[0], g_qlat[0], w_uq[0], g_kvlat[0], w_ukv[0],
                                    g_qh[0], g_kh[0], w_gla_a2[0], b_gla_a[0])
    tm_p = 512
    (q_p, k_p, v_p, ckv_p, kpe_p, gq_p, gk_p, gv_p, lg_p, gr_p) = _front(
        x_prompt.reshape(n_p, D_MODEL), tc, ts, fw, tm=tm_p, table_tiles=sp // tm_p)
    (q_s, _, _, ckv_s, kpe_s, gq_s, gk_s, gv_s, lg_s, gr_s) = _front(
        x_sample.reshape(n_s, D_MODEL), tc_s, ts_s, fw, tm=n_s // 2, table_tiles=2)
    kpe_p = kpe_p[:, NOPE_DIM:QK_DIM]
    kpe_s = kpe_s[:, NOPE_DIM:QK_DIM]

    attn_p = _prompt_attn(q_p, k_p, v_p, batch=bp, seq=sp, tq=256)
    o_p, gla_p = _gla_prompt(gq_p, gk_p, gv_p, lg_p, gr_p, g_gla_o[0], batch=bp, seq=sp, tg=256)
    gla_p = jnp.swapaxes(gla_p, -1, -2)

    attn_s = _sample_attn(q_s, ckv_s, kpe_s, cache_ckv[0], cache_kpe[0], page_table, kc, ks, g_kh[0],
                          w_uk, w_uv, t_len=ts_len, chunk_pages=8)
    o_s, gla_s = _gla_sample(gq_s, gk_s, gv_s, lg_s, gr_s, g_gla_o[0], state_gla[0], t_len=ts_len, nb=8)

    w_out = bf(w_out_ab[0])
    hp = _post(x_prompt.reshape(n_p, D_MODEL), p_prompt[0].reshape(n_p, PLE_DIM), post_w(0),
               mix=(attn_p, o_p, w_out), tm=512)
    hs = _post(x_sample.reshape(n_s, D_MODEL), p_sample[0].reshape(n_s, PLE_DIM), post_w(0),
               mix=(attn_s, o_s, w_out), tm=n_s // 2)

    w_b, w_c, lam = _s5_weights(s5_a_re[0], s5_a_im[0], s5_log_dt[0], s5_b_re[0], s5_b_im[0],
                                s5_c_re[0], s5_c_im[0])
    n_state = S5_GROUPS * S5_STATE
    zero_state = jnp.zeros((bp, n_state), F32)
    hp_tm, s5r_p, s5i_p = _s5(_to_time_major(hp, bp, sp), zero_state, zero_state, g_norm_c[0], w_b, w_c, lam,
                              s5_d[0], w_glu[0], nb=bp, t_tile=64)
    hs_tm, s5r_s, s5i_s = _s5(_to_time_major(hs, bs, ts_len), state_s5_re[0].reshape(bs, n_state),
                              state_s5_im[0].reshape(bs, n_state), g_norm_c[0], w_b, w_c, lam,
                              s5_d[0], w_glu[0], nb=bs, t_tile=ts_len)
    hp = _post(_to_batch_major(hp_tm, bp, sp), p_prompt[1].reshape(n_p, PLE_DIM), post_w(1), tm=512)
    hs = _post(_to_batch_major(hs_tm, bs, ts_len), p_sample[1].reshape(n_s, PLE_DIM), post_w(1), tm=n_s // 2)

    st = lambda a, b: a.reshape(1, b, S5_GROUPS, S5_STATE)
    return (hp.reshape(bp, sp, D_MODEL), hs.reshape(bs, ts_len, D_MODEL),
            ckv_p.reshape(1, bp, sp, KV_LORA), kpe_p.reshape(1, bp, sp, ROPE_DIM), gla_p[None],
            st(s5r_p, bp), st(s5i_p, bp),
            ckv_s.reshape(1, bs, ts_len, KV_LORA), kpe_s.reshape(1, bs, ts_len, ROPE_DIM), gla_s[None],
            st(s5r_s, bs), st(s5i_s, bs))
```

```python
import functools

import jax
import jax.numpy as jnp
from jax import lax
from jax.experimental import pallas as pl
from jax.experimental.pallas import tpu as pltpu

F32 = jnp.float32
BF16 = jnp.bfloat16

D_MODEL = 1024
PAGE_SIZE = 128
MLA_HEADS = 8
Q_LORA = 384
KV_LORA = 256
NOPE_DIM = 64
ROPE_DIM = 32
QK_DIM = NOPE_DIM + ROPE_DIM
V_DIM = 64
ROPE_THETA = 10000.0
GLA_HEADS = 4
GLA_DK = 128
GLA_DV = 128
GLA_GATE_RANK = 16
GLA_TAU = 16.0
GLA_CHUNK = 64
S5_GROUP = 16
S5_GROUPS = D_MODEL // S5_GROUP
S5_STATE = 64
D_FF = 4 * D_MODEL
PLE_DIM = 256
EPS = 1e-6
IN_SPLITS = (Q_LORA, KV_LORA, ROPE_DIM, GLA_HEADS * GLA_DK, GLA_HEADS * GLA_DK,
             GLA_HEADS * GLA_DV, GLA_GATE_RANK, GLA_HEADS * GLA_DV)

LANES = 128
SUBLANES = 8
HEAD_PAD = LANES
VMEM_LIMIT = 56 * 1024 * 1024
S5_SLABS = D_MODEL // LANES
S5_SLAB_STATE = (LANES // S5_GROUP) * S5_STATE
NEG_INF = float("-inf")


def _vmem_spec():
    return pl.BlockSpec(memory_space=pltpu.VMEM)


def _params(*sem):
    return pltpu.CompilerParams(dimension_semantics=sem, vmem_limit_bytes=VMEM_LIMIT)


def _bdot(a, b):
    return jnp.dot(a.astype(BF16), b.astype(BF16), preferred_element_type=F32)


def _bdot_nt(a, b):
    return lax.dot_general(a.astype(BF16), b.astype(BF16), (((1,), (1,)), ((), ())),
                           preferred_element_type=F32)


def _bdot_tn(a, b):
    return lax.dot_general(a.astype(BF16), b.astype(BF16), (((0,), (0,)), ((), ())),
                           preferred_element_type=F32)


def _rms(x, g):
    return x * lax.rsqrt(jnp.mean(x * x, axis=-1, keepdims=True) + EPS) * g


def _sigmoid(x):
    return 1.0 / (1.0 + jnp.exp(-x))


def _rope_table_kernel(cos_ref, sin_ref):
    shape = cos_ref.shape
    half = ROPE_DIM // 2
    per_row = LANES // half
    row = lax.broadcasted_iota(jnp.int32, shape, 0)
    lane = lax.broadcasted_iota(jnp.int32, shape, 1)
    pos = (row * per_row + lane // half).astype(F32)
    freq = jnp.power(F32(ROPE_THETA), -(lane % half).astype(F32) / half)
    ang = pos * freq
    cos_ref[...] = jnp.cos(ang)
    sin_ref[...] = jnp.sin(ang)


def _rope_tables(n_pos):
    half = ROPE_DIM // 2
    per_row = LANES // half
    rows = -(-n_pos // per_row)
    rows = -(-rows // SUBLANES) * SUBLANES
    cos, sin = pl.pallas_call(
        _rope_table_kernel,
        out_shape=(jax.ShapeDtypeStruct((rows, LANES), F32),) * 2,
        name="rope_table",
    )()
    cos = cos.reshape(rows * per_row, half)[:n_pos]
    sin = sin.reshape(rows * per_row, half)[:n_pos]
    return cos, sin


def _front_kernel(x_ref, tc_ref, ts_ref, gn_ref, wcq_ref, wckv_ref, wkpe_ref, wgq_ref, wgk_ref, wgv_ref,
                  wgr_ref, wga_ref, gql_ref, wuq_ref, gkvl_ref, wuk_ref, wuv_ref, gqh_ref, gkh_ref,
                  wa2_ref, ba_ref,
                  q_ref, k_ref, v_ref, ckv_ref, kpe_ref, gq_ref, gk_ref, gv_ref, lg_ref, gr_ref):
    n = _rms(x_ref[...], gn_ref[...])
    nb = n.astype(BF16)
    tc = tc_ref[...]
    ts = ts_ref[...]
    lane = lax.broadcasted_iota(jnp.int32, tc.shape, 1)
    first_half = lane < NOPE_DIM + ROPE_DIM // 2

    def norm_rope(xh, g):
        ss = jnp.sum(xh * xh, axis=-1, keepdims=True) * (1.0 / QK_DIM)
        y = xh * lax.rsqrt(ss + EPS) * g
        partner = jnp.where(first_half, pltpu.roll(y, LANES - ROPE_DIM // 2, 1), pltpu.roll(y, ROPE_DIM // 2, 1))
        return y * tc + partner * ts

    cq = _rms(jnp.dot(nb, wcq_ref[...], preferred_element_type=F32), gql_ref[...])
    q = _bdot(cq, wuq_ref[...])
    ckv = _rms(jnp.dot(nb, wckv_ref[...], preferred_element_type=F32), gkvl_ref[...])
    ckv_ref[...] = ckv
    ckvb = ckv.astype(BF16)
    kn = jnp.dot(ckvb, wuk_ref[...], preferred_element_type=F32)
    v_ref[...] = jnp.dot(ckvb, wuv_ref[...], preferred_element_type=F32).astype(v_ref.dtype)
    kpe = jnp.dot(nb, wkpe_ref[...], preferred_element_type=F32)
    kpe_ref[...] = kpe
    gqh = gqh_ref[...]
    gkh = gkh_ref[...]
    scale = QK_DIM ** -0.5
    for h in range(MLA_HEADS):
        hs = slice(h * HEAD_PAD, (h + 1) * HEAD_PAD)
        q_ref[:, hs] = (norm_rope(q[:, hs], gqh) * scale).astype(q_ref.dtype)
        k_ref[:, hs] = norm_rope(kn[:, hs] + kpe, gkh).astype(k_ref.dtype)

    gq_ref[...] = jnp.dot(nb, wgq_ref[...], preferred_element_type=F32) * (GLA_DK ** -0.5)
    gk_ref[...] = jnp.dot(nb, wgk_ref[...], preferred_element_type=F32)
    gv_ref[...] = jnp.dot(nb, wgv_ref[...], preferred_element_type=F32)
    gr_ref[...] = jnp.dot(nb, wgr_ref[...], preferred_element_type=F32)
    ga = jnp.dot(nb, wga_ref[...], preferred_element_type=F32)
    logit = _bdot(ga, wa2_ref[...]) + ba_ref[...]
    log_sig = jnp.minimum(logit, 0.0) - jnp.log1p(jnp.exp(-jnp.abs(logit)))
    lg_ref[...] = log_sig * (1.0 / GLA_TAU)


def _front_weights(w_in, g_norm, g_qlat, w_uq, g_kvlat, w_ukv, g_qh, g_kh, w_a2, b_a):
    offs = [0]
    for s in IN_SPLITS:
        offs.append(offs[-1] + s)
    cols = [w_in[:, offs[i]:offs[i + 1]] for i in range(len(IN_SPLITS))]
    w_cq, w_ckv, w_kpe, w_gq, w_gk, w_gv, w_ga, w_gr = cols
    pad_head = HEAD_PAD - QK_DIM
    w_kpe_placed = jnp.pad(w_kpe, ((0, 0), (NOPE_DIM, HEAD_PAD - QK_DIM)))
    w_ga_pad = jnp.pad(w_ga, ((0, 0), (0, LANES - GLA_GATE_RANK)))
    w_a2_pad = jnp.pad(w_a2, ((0, LANES - GLA_GATE_RANK), (0, 0)))
    w_uq_pad = jnp.pad(w_uq.reshape(Q_LORA, MLA_HEADS, QK_DIM), ((0, 0), (0, 0), (0, pad_head)))
    w_uq_pad = w_uq_pad.reshape(Q_LORA, MLA_HEADS * HEAD_PAD)
    w_ukv3 = w_ukv.reshape(KV_LORA, MLA_HEADS, NOPE_DIM + V_DIM)
    w_uk = w_ukv3[:, :, :NOPE_DIM]
    w_uv = w_ukv3[:, :, NOPE_DIM:].reshape(KV_LORA, MLA_HEADS * V_DIM)
    w_uk_pad = jnp.pad(w_uk, ((0, 0), (0, 0), (0, HEAD_PAD - NOPE_DIM))).reshape(KV_LORA, MLA_HEADS * HEAD_PAD)
    g_qh_pad = jnp.pad(g_qh, (0, pad_head))[None, :]
    g_kh_pad = jnp.pad(g_kh, (0, pad_head))[None, :]
    bf = lambda a: a.astype(BF16)
    weights = (g_norm[None, :], bf(w_cq), bf(w_ckv), bf(w_kpe_placed), bf(w_gq), bf(w_gk), bf(w_gv), bf(w_gr),
               bf(w_ga_pad), g_qlat[None, :], bf(w_uq_pad), g_kvlat[None, :], bf(w_uk_pad), bf(w_uv),
               g_qh_pad, g_kh_pad, bf(w_a2_pad), b_a[None, :])
    return weights, w_uk, w_uv


def _front(x, tc, ts, weights, *, tm, table_tiles):
    n = x.shape[0]
    hk = GLA_HEADS * GLA_DK
    row = lambda w: pl.BlockSpec((tm, w), lambda i: (i, 0))
    tab = pl.BlockSpec((tm, LANES), lambda i: (i % table_tiles, 0))
    sds = lambda w, dt: jax.ShapeDtypeStruct((n, w), dt)
    out_shape = (sds(MLA_HEADS * HEAD_PAD, BF16), sds(MLA_HEADS * HEAD_PAD, BF16), sds(MLA_HEADS * V_DIM, BF16),
                 sds(KV_LORA, F32), sds(LANES, F32), sds(hk, F32), sds(hk, F32), sds(hk, F32), sds(hk, F32),
                 sds(hk, F32))
    out_specs = (row(MLA_HEADS * HEAD_PAD), row(MLA_HEADS * HEAD_PAD), row(MLA_HEADS * V_DIM), row(KV_LORA),
                 row(LANES), row(hk), row(hk), row(hk), row(hk), row(hk))
    return pl.pallas_call(
        _front_kernel,
        grid=(n // tm,),
        in_specs=[row(D_MODEL), tab, tab] + [_vmem_spec()] * len(weights),
        out_specs=out_specs,
        out_shape=out_shape,
        compiler_params=_params("parallel"),
        name="ab_front",
    )(x, tc, ts, *weights)


def _prompt_attn_kernel(q_ref, k_ref, v_ref, o_ref, *, tq):
    i = pl.program_id(1)
    row = lax.broadcasted_iota(jnp.int32, (tq, tq), 0)
    col = lax.broadcasted_iota(jnp.int32, (tq, tq), 1)
    causal = col <= row
    lane = lax.broadcasted_iota(jnp.int32, (tq, LANES), 1)
    for pair in range(MLA_HEADS // 2):
        vs = slice(pair * LANES, (pair + 1) * LANES)
        outs = []
        for h in (2 * pair, 2 * pair + 1):
            hs = slice(h * HEAD_PAD, (h + 1) * HEAD_PAD)
            qh = q_ref[:, hs]

            def step(j, carry, masked, hs=hs, qh=qh, vs=vs):
                m, l, acc = carry
                start = pl.multiple_of(j * tq, tq)
                kj = k_ref[pl.ds(start, tq), hs]
                vj = v_ref[pl.ds(start, tq), vs]
                s = lax.dot_general(qh, kj, (((1,), (1,)), ((), ())), preferred_element_type=F32)
                if masked:
                    s = jnp.where(causal, s, NEG_INF)
                m_new = jnp.maximum(m, jnp.max(s, axis=-1, keepdims=True))
                alpha = jnp.exp(m - m_new)
                p = jnp.exp(s - m_new)
                l = alpha * l + jnp.sum(p, axis=-1, keepdims=True)
                acc = alpha * acc + jnp.dot(p.astype(BF16), vj, preferred_element_type=F32)
                return m_new, l, acc

            init = (jnp.full((tq, 1), NEG_INF, F32), jnp.zeros((tq, 1), F32), jnp.zeros((tq, LANES), F32))
            carry = lax.fori_loop(0, i, functools.partial(step, masked=False), init)
            m, l, acc = step(i, carry, True)
            outs.append(acc / l)
        o_ref[:, vs] = jnp.where(lane < V_DIM, outs[0], outs[1]).astype(o_ref.dtype)


def _prompt_attn(q, k, v, *, batch, seq, tq):
    nq = seq // tq
    return pl.pallas_call(
        functools.partial(_prompt_attn_kernel, tq=tq),
        grid=(batch, nq),
        in_specs=[pl.BlockSpec((tq, MLA_HEADS * HEAD_PAD), lambda b, i: (b * nq + i, 0)),
                  pl.BlockSpec((seq, MLA_HEADS * HEAD_PAD), lambda b, i: (b, 0)),
                  pl.BlockSpec((seq, MLA_HEADS * V_DIM), lambda b, i: (b, 0))],
        out_specs=pl.BlockSpec((tq, MLA_HEADS * V_DIM), lambda b, i: (b * nq + i, 0)),
        out_shape=jax.ShapeDtypeStruct((batch * seq, MLA_HEADS * V_DIM), BF16),
        compiler_params=_params("parallel", "arbitrary"),
        name="mla_prompt_attn",
    )(q, k, v)


def _gla_intra(q, k, b):
    c = q.shape[0]
    row = lax.broadcasted_iota(jnp.int32, q.shape, 0)
    r2 = lax.broadcasted_iota(jnp.int32, (c, c), 0)
    c2 = lax.broadcasted_iota(jnp.int32, (c, c), 1)
    att = jnp.where(r2 == c2, _bdot_nt(q, k), 0.0)
    before = pltpu.roll(b, 1, 0)
    last = b
    m = 1
    while m < c:
        upper = (row // m) % 2 == 1
        qm = jnp.where(upper, q * jnp.exp(jnp.minimum(b - before, 0.0)), 0.0)
        km = jnp.where(upper, 0.0, k * jnp.exp(jnp.minimum(last - b, 0.0)))
        same = (r2 // (2 * m)) == (c2 // (2 * m))
        att = att + jnp.where(same, _bdot_nt(qm, km), 0.0)
        if 2 * m < c:
            before = jnp.where(upper, pltpu.roll(before, m, 0), before)
            last = jnp.where(upper, last, pltpu.roll(last, c - m, 0))
        m *= 2
    return att


def _cumsum_rows(g):
    c = g.shape[0]
    row = lax.broadcasted_iota(jnp.int32, g.shape, 0)
    b = g
    sh = 1
    while sh < c:
        b = b + jnp.where(row >= sh, pltpu.roll(b, sh, 0), 0.0)
        sh *= 2
    return b


def _gla_prompt_kernel(gq_ref, gk_ref, gv_ref, lg_ref, gr_ref, go_ref, o_ref, st_ref, state_ref, *, n_chunks):
    t = pl.program_id(1)

    @pl.when(t == 0)
    def _():
        state_ref[...] = jnp.zeros_like(state_ref)

    go = go_ref[...]

    def chunk(ci, carry):
        rows = pl.ds(pl.multiple_of(ci * GLA_CHUNK, GLA_CHUNK), GLA_CHUNK)
        for h in range(GLA_HEADS):
            hs = slice(h * GLA_DK, (h + 1) * GLA_DK)
            q = gq_ref[rows, hs]
            k = gk_ref[rows, hs]
            v = gv_ref[rows, hs]
            b = _cumsum_rows(lg_ref[rows, hs])
            st = state_ref[h]
            inter = _bdot_nt(q * jnp.exp(b), st)
            o = inter + _bdot(_gla_intra(q, k, b), v)
            bl = b[GLA_CHUNK - 1:GLA_CHUNK, :]
            state_ref[h] = st * jnp.exp(bl) + _bdot_tn(v, k * jnp.exp(bl - b))
            gr = gr_ref[rows, hs]
            o_ref[rows, hs] = (_rms(o, go) * (gr * _sigmoid(gr))).astype(o_ref.dtype)
        return carry

    lax.fori_loop(0, n_chunks, chunk, 0)

    @pl.when(t == pl.num_programs(1) - 1)
    def _():
        st_ref[0] = state_ref[...]


def _gla_prompt(gq, gk, gv, lg, gr, g_o, *, batch, seq, tg):
    nt = seq // tg
    hk = GLA_HEADS * GLA_DK
    row = pl.BlockSpec((tg, hk), lambda b, t: (b * nt + t, 0))
    return pl.pallas_call(
        functools.partial(_gla_prompt_kernel, n_chunks=tg // GLA_CHUNK),
        grid=(batch, nt),
        in_specs=[row, row, row, row, row, _vmem_spec()],
        out_specs=(row, pl.BlockSpec((1, GLA_HEADS, GLA_DV, GLA_DK), lambda b, t: (b, 0, 0, 0))),
        out_shape=(jax.ShapeDtypeStruct((batch * seq, hk), BF16),
                   jax.ShapeDtypeStruct((batch, GLA_HEADS, GLA_DV, GLA_DK), F32)),
        scratch_shapes=[pltpu.VMEM((GLA_HEADS, GLA_DV, GLA_DK), F32)],
        compiler_params=_params("parallel", "arbitrary"),
        name="gla_prompt",
    )(gq, gk, gv, lg, gr, g_o[None, :])


def _gla_sample_kernel(gq_ref, gk_ref, gv_ref, lg_ref, gr_ref, go_ref, s0_ref, o_ref, s1_ref, oacc_ref,
                       *, nb, t_len):
    rows = nb * t_len
    pad = jnp.zeros((LANES - rows, LANES), F32)

    def cols(x):
        return jnp.transpose(jnp.concatenate([x, pad], axis=0))

    for h in range(GLA_HEADS):
        hs = slice(h * GLA_DK, (h + 1) * GLA_DK)
        qt = cols(gq_ref[:, hs])
        kt = cols(gk_ref[:, hs])
        at = cols(jnp.exp(lg_ref[:, hs]))
        for bi in range(nb):
            s = s0_ref[bi, h]
            for t in range(t_len):
                r = bi * t_len + t
                v = gv_ref[r:r + 1, hs]
                s = s * at[:, r:r + 1] + kt[:, r:r + 1] * v
                oacc_ref[r:r + 1, hs] = jnp.sum(s * qt[:, r:r + 1], axis=0, keepdims=True)
            s1_ref[bi, h] = s
    go = go_ref[...]
    for h in range(GLA_HEADS):
        hs = slice(h * GLA_DK, (h + 1) * GLA_DK)
        gr = gr_ref[:, hs]
        o_ref[:, hs] = (_rms(oacc_ref[:, hs], go) * (gr * _sigmoid(gr))).astype(o_ref.dtype)


def _gla_sample(gq, gk, gv, lg, gr, g_o, s0, *, t_len, nb):
    n = gq.shape[0]
    batch = n // t_len
    hk = GLA_HEADS * GLA_DK
    rows = nb * t_len
    row = pl.BlockSpec((rows, hk), lambda i: (i, 0))
    st = pl.BlockSpec((nb, GLA_HEADS, GLA_DK, GLA_DV), lambda i: (i, 0, 0, 0))
    return pl.pallas_call(
        functools.partial(_gla_sample_kernel, nb=nb, t_len=t_len),
        grid=(batch // nb,),
        in_specs=[row, row, row, row, row, _vmem_spec(), st],
        out_specs=(row, st),
        out_shape=(jax.ShapeDtypeStruct((n, hk), BF16), jax.ShapeDtypeStruct(s0.shape, s0.dtype)),
        scratch_shapes=[pltpu.VMEM((rows, hk), F32)],
        compiler_params=_params("parallel"),
        name="gla_sample",
    )(gq, gk, gv, lg, gr, g_o[None, :], s0)


def _sample_attn_kernel(pt_ref, q_ref, cnew_ref, pnew_ref, kc_ref, ks_ref, kcn_ref, ksn_ref, gk_ref, gpe_ref,
                        wukt_ref, wukp_ref, wuv_ref, ckv_hbm, kpe_hbm, o_ref, s_scr, c_scr, ckv_buf, kpe_buf, sem,
                        *, n_pages, chunk_pages, t_len):
    n_cols = t_len * MLA_HEADS
    chunk = chunk_pages * PAGE_SIZE
    n_chunks = n_pages // chunk_pages
    past = n_pages * PAGE_SIZE
    b = pl.program_id(0)
    slot = b % 2

    def page_copies(batch, sl, i):
        page = pt_ref[batch, i]
        ci = i // chunk_pages
        return (pltpu.make_async_copy(ckv_hbm.at[page], ckv_buf.at[sl, i], sem.at[sl, ci, 0]),
                pltpu.make_async_copy(kpe_hbm.at[page], kpe_buf.at[sl, i], sem.at[sl, ci, 1]))

    def start_pages(batch, sl, pages):
        for i in pages:
            for cp in page_copies(batch, sl, i):
                cp.start()

    def wait_pages(pages):
        for i in pages:
            for cp in page_copies(b, slot, i):
                cp.wait()

    @pl.when(b == 0)
    def _():
        start_pages(0, 0, range(n_pages))

    qg = q_ref[...].astype(F32) * gk_ref[...]
    gpe = gpe_ref[...]
    sub = lax.broadcasted_iota(jnp.int32, (SUBLANES, MLA_HEADS * HEAD_PAD), 0)
    lane_head = lax.broadcasted_iota(jnp.int32, (SUBLANES, MLA_HEADS * HEAD_PAD), 1) // HEAD_PAD
    qexp = jnp.concatenate(
        [jnp.where(lane_head == sub, jnp.broadcast_to(qg[t:t + 1, :], sub.shape), 0.0) for t in range(t_len)],
        axis=0)
    qt = _bdot(qexp, wukt_ref[...])
    lhs = jnp.concatenate([wukp_ref[...], qt[:, :KV_LORA].astype(BF16)], axis=0)
    qr = qt[:, KV_LORA:].astype(BF16)
    n_norm = NOPE_DIM * MLA_HEADS
    half = ROPE_DIM // 2

    def scores(c, kpe, kcos, ksin):
        n = c.shape[0]
        cb = c.astype(BF16)
        r = lax.dot_general(lhs, cb, (((1,), (1,)), ((), ())), preferred_element_type=F32)
        kn = r[:n_norm].reshape(NOPE_DIM, MLA_HEADS, n)
        kt = jnp.transpose(jnp.concatenate([kpe, jnp.zeros((n, LANES - ROPE_DIM), F32)], axis=1))
        k32 = kt[:ROPE_DIM]
        ss = jnp.sum(kn * kn, axis=0) + jnp.sum(k32 * k32, axis=0, keepdims=True)
        rs = lax.rsqrt(ss * (1.0 / QK_DIM) + EPS)
        kg = k32 * gpe
        kr = kg * kcos + jnp.concatenate([kg[half:], kg[:half]], axis=0) * ksin
        krp = jnp.concatenate([kr, kt[ROPE_DIM:]], axis=0).astype(BF16)
        s = r[n_norm:] + jnp.dot(qr, krp, preferred_element_type=F32)
        s = (s.reshape(t_len, MLA_HEADS, n) * rs[None]).reshape(n_cols, n)
        return cb, s

    m = jnp.full((n_cols, 1), NEG_INF, F32)
    for ci in range(n_chunks):
        pages = range(ci * chunk_pages, (ci + 1) * chunk_pages)

        @pl.when(b + 1 < pl.num_programs(0))
        def _(pages=pages):
            start_pages(b + 1, 1 - slot, pages)

        wait_pages(pages)
        c = ckv_buf[slot, pl.ds(ci * chunk_pages, chunk_pages)].reshape(chunk, KV_LORA)
        kpe = kpe_buf[slot, pl.ds(ci * chunk_pages, chunk_pages)].reshape(chunk, ROPE_DIM)
        keys = slice(ci * chunk, (ci + 1) * chunk)
        cb, s = scores(c, kpe, kc_ref[:, keys], ks_ref[:, keys])
        c_scr[keys, :] = cb
        s_scr[:, keys] = s
        m = jnp.maximum(m, jnp.max(s, axis=1, keepdims=True))

    zpad = lambda a: jnp.concatenate([a, jnp.zeros((PAGE_SIZE - t_len, a.shape[1]), a.dtype)], axis=0)
    cb, s = scores(zpad(cnew_ref[...]), zpad(pnew_ref[...]), kcn_ref[...], ksn_ref[...])
    r2 = lax.broadcasted_iota(jnp.int32, (n_cols, PAGE_SIZE), 0)
    c2 = lax.broadcasted_iota(jnp.int32, (n_cols, PAGE_SIZE), 1)
    s = jnp.where((c2 < t_len) & (c2 <= r2 // MLA_HEADS), s, NEG_INF)
    new_keys = slice(past, past + PAGE_SIZE)
    c_scr[new_keys, :] = cb
    s_scr[:, new_keys] = s
    m = jnp.maximum(m, jnp.max(s, axis=1, keepdims=True))

    l = jnp.zeros((n_cols, 1), F32)
    acc = jnp.zeros((n_cols, KV_LORA), F32)
    bounds = [(ci * chunk, chunk) for ci in range(n_chunks)] + [(past, PAGE_SIZE)]
    for start, size in bounds:
        keys = slice(start, start + size)
        p = jnp.exp(s_scr[:, keys] - m)
        l = l + jnp.sum(p, axis=1, keepdims=True)
        acc = acc + jnp.dot(p.astype(BF16), c_scr[keys, :], preferred_element_type=F32)

    lat = acc / l
    res = _bdot(lat, wuv_ref[...])
    r3 = lax.broadcasted_iota(jnp.int32, res.shape, 0)
    c3 = lax.broadcasted_iota(jnp.int32, res.shape, 1)
    res = jnp.where(c3 // V_DIM == r3 % MLA_HEADS, res, 0.0)
    o_ref[...] = jnp.sum(res.reshape(t_len, MLA_HEADS, MLA_HEADS * V_DIM), axis=1).astype(o_ref.dtype)


def _sample_attn(q, ckv_new, kpe_new, cache_ckv, cache_kpe, page_table, kc, ks, g_kh, w_uk, w_uv,
                 *, t_len, chunk_pages):
    batch, n_pages = page_table.shape
    past = n_pages * PAGE_SIZE
    w_ukt = jnp.pad(jnp.transpose(w_uk, (1, 2, 0)), ((0, 0), (0, HEAD_PAD - NOPE_DIM), (0, 0)))
    w_ukt = w_ukt.reshape(MLA_HEADS * HEAD_PAD, KV_LORA)
    sel = jnp.pad(jnp.eye(ROPE_DIM, dtype=F32), ((NOPE_DIM, HEAD_PAD - QK_DIM), (0, LANES - ROPE_DIM)))
    w_ukt_ext = jnp.concatenate([w_ukt, jnp.tile(sel, (MLA_HEADS, 1))], axis=1).astype(BF16)
    w_uk_perm = jnp.transpose(w_uk, (2, 1, 0)).reshape(NOPE_DIM * MLA_HEADS, KV_LORA).astype(BF16)
    g_k_pad = jnp.concatenate([g_kh[:NOPE_DIM], jnp.ones((ROPE_DIM,), F32), jnp.zeros((HEAD_PAD - QK_DIM,), F32)])
    g_k_pad = jnp.tile(g_k_pad, MLA_HEADS)[None, :]
    g_pe = g_kh[NOPE_DIM:][:, None]
    kc, ks = jnp.transpose(kc), jnp.transpose(ks)
    kcn = jnp.pad(kc[:, past:past + t_len], ((0, 0), (0, PAGE_SIZE - t_len)))
    ksn = jnp.pad(ks[:, past:past + t_len], ((0, 0), (0, PAGE_SIZE - t_len)))
    per_b = lambda w: pl.BlockSpec((None, t_len, w), lambda b, pt: (b, 0, 0))
    hbm = pl.BlockSpec(memory_space=pl.ANY)
    in_specs = ([per_b(MLA_HEADS * HEAD_PAD), per_b(KV_LORA), per_b(ROPE_DIM)] + [_vmem_spec()] * 9 + [hbm, hbm])
    grid_spec = pltpu.PrefetchScalarGridSpec(
        num_scalar_prefetch=1,
        grid=(batch,),
        in_specs=in_specs,
        out_specs=pl.BlockSpec((None, t_len, MLA_HEADS * V_DIM), lambda b, pt: (b, 0, 0)),
        scratch_shapes=[pltpu.VMEM((t_len * MLA_HEADS, past + PAGE_SIZE), F32),
                        pltpu.VMEM((past + PAGE_SIZE, KV_LORA), BF16),
                        pltpu.VMEM((2, n_pages, PAGE_SIZE, KV_LORA), F32),
                        pltpu.VMEM((2, n_pages, PAGE_SIZE, ROPE_DIM), F32),
                        pltpu.SemaphoreType.DMA((2, n_pages // chunk_pages, 2))],
    )
    out = pl.pallas_call(
        functools.partial(_sample_attn_kernel, n_pages=n_pages, chunk_pages=chunk_pages, t_len=t_len),
        grid_spec=grid_spec,
        out_shape=jax.ShapeDtypeStruct((batch, t_len, MLA_HEADS * V_DIM), BF16),
        compiler_params=_params("arbitrary"),
        name="mla_sample_attn",
    )(page_table, q.reshape(batch, t_len, -1), ckv_new.reshape(batch, t_len, -1),
      kpe_new.reshape(batch, t_len, -1), kc[:, :past], ks[:, :past], kcn, ksn, g_k_pad, g_pe,
      w_ukt_ext, w_uk_perm, w_uv.astype(BF16), cache_ckv, cache_kpe)
    return out.reshape(batch * t_len, MLA_HEADS * V_DIM)


def _mlp_ple(h, p_ref, gm_ref, wup_ref, wdown_ref, gp_ref, wgate_ref, wproj_ref, o_ref, ff_chunk):
    nb = _rms(h, gm_ref[...]).astype(BF16)
    y = jnp.zeros_like(h)
    for c in range(D_FF // ff_chunk):
        cs = slice(c * ff_chunk, (c + 1) * ff_chunk)
        a = jnp.maximum(jnp.dot(nb, wup_ref[:, cs], preferred_element_type=F32), 0.0)
        y = y + jnp.dot((a * a).astype(BF16), wdown_ref[cs, :], preferred_element_type=F32)
    h = h + y
    gate = _sigmoid(_bdot(_rms(h, gp_ref[...]), wgate_ref[...]))
    o_ref[...] = h + gate * _bdot(p_ref[...], wproj_ref[...])


def _post_mix_kernel(h_ref, a_ref, o2_ref, p_ref, wout_ref, gm_ref, wup_ref, wdown_ref, gp_ref, wgate_ref,
                     wproj_ref, o_ref, *, ff_chunk):
    n_attn = MLA_HEADS * V_DIM
    h = (h_ref[...] + jnp.dot(a_ref[...], wout_ref[:n_attn, :], preferred_element_type=F32)
         + jnp.dot(o2_ref[...], wout_ref[n_attn:, :], preferred_element_type=F32))
    _mlp_ple(h, p_ref, gm_ref, wup_ref, wdown_ref, gp_ref, wgate_ref, wproj_ref, o_ref, ff_chunk)


def _post_kernel(h_ref, p_ref, gm_ref, wup_ref, wdown_ref, gp_ref, wgate_ref, wproj_ref, o_ref, *, ff_chunk):
    _mlp_ple(h_ref[...], p_ref, gm_ref, wup_ref, wdown_ref, gp_ref, wgate_ref, wproj_ref, o_ref, ff_chunk)


def _post(h, p, weights, mix=None, *, tm, ff_chunk=1024):
    n = h.shape[0]
    row = lambda w: pl.BlockSpec((tm, w), lambda i: (i, 0))
    if mix is None:
        body, acts, specs = _post_kernel, (h, p), [row(D_MODEL), row(PLE_DIM)]
    else:
        attn, o_gla, w_out = mix
        body, acts = _post_mix_kernel, (h, attn, o_gla, p)
        specs = [row(D_MODEL), row(attn.shape[1]), row(o_gla.shape[1]), row(PLE_DIM)]
        weights = (w_out,) + tuple(weights)
    return pl.pallas_call(
        functools.partial(body, ff_chunk=ff_chunk),
        grid=(n // tm,),
        in_specs=specs + [_vmem_spec()] * len(weights),
        out_specs=row(D_MODEL),
        out_shape=jax.ShapeDtypeStruct((n, D_MODEL), F32),
        compiler_params=_params("parallel"),
        name="post_mix" if mix is not None else "post",
    )(*acts, *weights)


def _s5_prep_kernel(are_ref, aim_ref, ldt_ref, bre_ref, bim_ref, lr_ref, li_ref, bbr_ref, bbi_ref):
    a_re = are_ref[...]
    a_im = aim_ref[...]
    dt = jnp.exp(ldt_ref[...])
    mag = jnp.exp(a_re * dt)
    lr = mag * jnp.cos(a_im * dt)
    li = mag * jnp.sin(a_im * dt)
    den = a_re * a_re + a_im * a_im
    cr = ((lr - 1.0) * a_re + li * a_im) / den
    ci = (li * a_re - (lr - 1.0) * a_im) / den
    lr_ref[...] = lr
    li_ref[...] = li
    b_re = bre_ref[...]
    b_im = bim_ref[...]
    bbr_ref[...] = cr[:, None, :] * b_re - ci[:, None, :] * b_im
    bbi_ref[...] = cr[:, None, :] * b_im + ci[:, None, :] * b_re


def _s5_weights(a_re, a_im, log_dt, b_re, b_im, c_re, c_im):
    g, p_, ch = S5_GROUPS, S5_STATE, S5_GROUP
    gp = jax.ShapeDtypeStruct((g, p_), F32)
    gcp = jax.ShapeDtypeStruct((g, ch, p_), F32)
    lr, li, bbr, bbi = pl.pallas_call(
        _s5_prep_kernel, out_shape=(gp, gp, gcp, gcp), name="s5_prep",
    )(a_re, a_im, log_dt[:, None], jnp.transpose(b_re, (0, 2, 1)), jnp.transpose(b_im, (0, 2, 1)))
    per_slab = LANES // ch
    eye = jnp.eye(per_slab, dtype=F32)

    def b_blocks(bb):
        bb = bb.reshape(S5_SLABS, per_slab, ch, p_)
        return jnp.einsum('jgcp,gh->jgchp', bb, eye).reshape(S5_SLABS, LANES, S5_SLAB_STATE)

    def c_blocks(cc):
        cc = cc.reshape(S5_SLABS, per_slab, ch, p_)
        return jnp.einsum('jgcp,gh->jgphc', cc, eye).reshape(S5_SLABS, S5_SLAB_STATE, LANES)

    w_b = jnp.concatenate([b_blocks(bbr), b_blocks(bbi)], axis=2).astype(BF16)
    w_c = jnp.concatenate([c_blocks(c_re), -c_blocks(c_im)], axis=1).astype(BF16)
    lam = jnp.stack([lr.reshape(S5_SLABS, S5_SLAB_STATE), li.reshape(S5_SLABS, S5_SLAB_STATE)], axis=1)
    return w_b, w_c, lam


def _s5_kernel(x_ref, x0r_ref, x0i_ref, gn_ref, wb_ref, wc_ref, lam_ref, d_ref, wglu_ref,
               o_ref, xr_ref, xi_ref, bur_ref, bui_ref, sr_ref, si_ref, *, nb, t_len):
    step = pl.program_id(0)
    ns = S5_SLAB_STATE

    @pl.when(step == 0)
    def _():
        for j in range(S5_SLABS):
            sr_ref[j] = x0r_ref[:, j * ns:(j + 1) * ns]
            si_ref[j] = x0i_ref[:, j * ns:(j + 1) * ns]

    x = x_ref[...]
    u = _rms(x, gn_ref[...])
    ub = u.astype(BF16)
    for j in range(S5_SLABS):
        bu = jnp.dot(ub[:, j * LANES:(j + 1) * LANES], wb_ref[j], preferred_element_type=F32)
        bur_ref[j] = bu[:, :ns]
        bui_ref[j] = bu[:, ns:]

    n_groups = nb // SUBLANES

    def slab(j, carry):
        lr = jnp.broadcast_to(lam_ref[j, 0:1, :], (SUBLANES, ns))
        li = jnp.broadcast_to(lam_ref[j, 1:2, :], (SUBLANES, ns))

        def group(gi, carry):
            srow = pl.ds(pl.multiple_of(gi * SUBLANES, SUBLANES), SUBLANES)
            xr0 = sr_ref[j, srow, :]
            xi0 = si_ref[j, srow, :]

            def tok(t, st):
                xr, xi = st
                rows = pl.ds(pl.multiple_of(t * nb + gi * SUBLANES, SUBLANES), SUBLANES)
                nr = lr * xr - li * xi + bur_ref[j, rows, :]
                ni = lr * xi + li * xr + bui_ref[j, rows, :]
                bur_ref[j, rows, :] = nr
                bui_ref[j, rows, :] = ni
                return nr, ni

            xr, xi = lax.fori_loop(0, t_len, tok, (xr0, xi0))
            sr_ref[j, srow, :] = xr
            si_ref[j, srow, :] = xi
            return carry

        return lax.fori_loop(0, n_groups, group, carry)

    lax.fori_loop(0, S5_SLABS, slab, 0)

    ys = []
    for j in range(S5_SLABS):
        xs = jnp.concatenate([bur_ref[j].astype(BF16), bui_ref[j].astype(BF16)], axis=1)
        ys.append(jnp.dot(xs, wc_ref[j], preferred_element_type=F32))
    y = jnp.concatenate(ys, axis=1) + d_ref[...] * u
    gl = jax.nn.gelu(y)
    zg = _bdot(gl, wglu_ref[...])
    o_ref[...] = x + zg[:, :D_MODEL] * _sigmoid(zg[:, D_MODEL:])

    @pl.when(step == pl.num_programs(0) - 1)
    def _():
        for j in range(S5_SLABS):
            xr_ref[:, j * ns:(j + 1) * ns] = sr_ref[j]
            xi_ref[:, j * ns:(j + 1) * ns] = si_ref[j]


def _s5(x_tm, x0r, x0i, g_norm, w_b, w_c, lam, d, w_glu, *, nb, t_tile):
    n = x_tm.shape[0]
    rows = nb * t_tile
    n_state = S5_GROUPS * S5_STATE
    full = lambda shape: pl.BlockSpec(shape, lambda i: (0,) * len(shape))
    st = jax.ShapeDtypeStruct((nb, n_state), F32)
    return pl.pallas_call(
        functools.partial(_s5_kernel, nb=nb, t_len=t_tile),
        grid=(n // rows,),
        in_specs=[pl.BlockSpec((rows, D_MODEL), lambda i: (i, 0)), full((nb, n_state)), full((nb, n_state))]
        + [_vmem_spec()] * 6,
        out_specs=(pl.BlockSpec((rows, D_MODEL), lambda i: (i, 0)), full((nb, n_state)), full((nb, n_state))),
        out_shape=(jax.ShapeDtypeStruct((n, D_MODEL), F32), st, st),
        scratch_shapes=[pltpu.VMEM((S5_SLABS, rows, S5_SLAB_STATE), F32),
                        pltpu.VMEM((S5_SLABS, rows, S5_SLAB_STATE), F32),
                        pltpu.VMEM((S5_SLABS, nb, S5_SLAB_STATE), F32),
                        pltpu.VMEM((S5_SLABS, nb, S5_SLAB_STATE), F32)],
        compiler_params=_params("arbitrary"),
        name="s5_layer",
    )(x_tm, x0r, x0i, g_norm[None, :], w_b, w_c, lam, d[None, :], w_glu.astype(BF16))


def _to_time_major(h, batch, seq):
    return h.reshape(batch, seq, -1).transpose(1, 0, 2).reshape(batch * seq, -1)


def _to_batch_major(h, batch, seq):
    return h.reshape(seq, batch, -1).transpose(1, 0, 2).reshape(batch * seq, -1)


def kernel(x_prompt, x_sample, cache_ckv, cache_kpe, state_gla, state_s5_re, state_s5_im, page_table,
           p_prompt, p_sample, g_norm_ab, w_in_ab, g_qlat, w_uq, g_kvlat, w_ukv, g_qh, g_kh,
           w_gla_a2, b_gla_a, g_gla_o, w_out_ab, g_norm_c, s5_a_re, s5_a_im, s5_log_dt,
           s5_b_re, s5_b_im, s5_c_re, s5_c_im, s5_d, w_glu, g_norm_mlp, w_up, w_down,
           g_norm_ple, w_ple_gate, w_ple_proj):
    bp, sp, _ = x_prompt.shape
    bs, ts_len, _ = x_sample.shape
    n_pages = page_table.shape[1]
    past = n_pages * PAGE_SIZE
    n_p, n_s = bp * sp, bs * ts_len
    bf = lambda a: a.astype(BF16)

    cos16, sin16 = _rope_tables(past + ts_len)
    ones = jnp.ones((cos16.shape[0], 1), F32)
    tc = jnp.concatenate([ones * jnp.ones((1, NOPE_DIM), F32), cos16, cos16,
                          ones * jnp.ones((1, HEAD_PAD - QK_DIM), F32)], axis=1)
    ts = jnp.concatenate([ones * jnp.zeros((1, NOPE_DIM), F32), -sin16, sin16,
                          ones * jnp.zeros((1, HEAD_PAD - QK_DIM), F32)], axis=1)
    kc = jnp.concatenate([cos16, cos16], axis=1)
    ks = jnp.concatenate([-sin16, sin16], axis=1)
    tc_s = jnp.tile(tc[past:past + ts_len], (bs, 1))
    ts_s = jnp.tile(ts[past:past + ts_len], (bs, 1))

    post_w = lambda i: (g_norm_mlp[i][None, :], bf(w_up[i]), bf(w_down[i]), g_norm_ple[i][None, :],
                        bf(w_ple_gate[i]), bf(w_ple_proj[i]))

    fw, w_uk, w_uv = _front_weights(w_in_ab[0], g_norm_ab[0], g_qlat[0], w_uq[0], g_kvlat[0], w_ukv[0],
                                    g_qh[0], g_kh[0], w_gla_a2[0], b_gla_a[0])
    tm_p = 512
    (q_p, k_p, v_p, ckv_p, kpe_p, gq_p, gk_p, gv_p, lg_p, gr_p) = _front(
        x_prompt.reshape(n_p, D_MODEL), tc, ts, fw, tm=tm_p, table_tiles=sp // tm_p)
    (q_s, _, _, ckv_s, kpe_s, gq_s, gk_s, gv_s, lg_s, gr_s) = _front(
        x_sample.reshape(n_s, D_MODEL), tc_s, ts_s, fw, tm=n_s // 2, table_tiles=2)
    kpe_p = kpe_p[:, NOPE_DIM:QK_DIM]
    kpe_s = kpe_s[:, NOPE_DIM:QK_DIM]

    attn_p = _prompt_attn(q_p, k_p, v_p, batch=bp, seq=sp, tq=256)
    o_p, gla_p = _gla_prompt(gq_p, gk_p, gv_p, lg_p, gr_p, g_gla_o[0], batch=bp, seq=sp, tg=256)
    gla_p = jnp.swapaxes(gla_p, -1, -2)

    attn_s = _sample_attn(q_s, ckv_s, kpe_s, cache_ckv[0], cache_kpe[0], page_table, kc, ks, g_kh[0],
                          w_uk, w_uv, t_len=ts_len, chunk_pages=8)
    o_s, gla_s = _gla_sample(gq_s, gk_s, gv_s, lg_s, gr_s, g_gla_o[0], state_gla[0], t_len=ts_len, nb=8)

    w_out = bf(w_out_ab[0])
    hp = _post(x_prompt.reshape(n_p, D_MODEL), p_prompt[0].reshape(n_p, PLE_DIM), post_w(0),
               mix=(attn_p, o_p, w_out), tm=512)
    hs = _post(x_sample.reshape(n_s, D_MODEL), p_sample[0].reshape(n_s, PLE_DIM), post_w(0),
               mix=(attn_s, o_s, w_out), tm=n_s // 2)

    w_b, w_c, lam = _s5_weights(s5_a_re[0], s5_a_im[0], s5_log_dt[0], s5_b_re[0], s5_b_im[0],
                                s5_c_re[0], s5_c_im[0])
    n_state = S5_GROUPS * S5_STATE
    zero_state = jnp.zeros((bp, n_state), F32)
    hp_tm, s5r_p, s5i_p = _s5(_to_time_major(hp, bp, sp), zero_state, zero_state, g_norm_c[0], w_b, w_c, lam,
                              s5_d[0], w_glu[0], nb=bp, t_tile=64)
    hs_tm, s5r_s, s5i_s = _s5(_to_time_major(hs, bs, ts_len), state_s5_re[0].reshape(bs, n_state),
                              state_s5_im[0].reshape(bs, n_state), g_norm_c[0], w_b, w_c, lam,
                              s5_d[0], w_glu[0], nb=bs, t_tile=ts_len)
    hp = _post(_to_batch_major(hp_tm, bp, sp), p_prompt[1].reshape(n_p, PLE_DIM), post_w(1), tm=512)
    hs = _post(_to_batch_major(hs_tm, bs, ts_len), p_sample[1].reshape(n_s, PLE_DIM), post_w(1), tm=n_s // 2)

    st = lambda a, b: a.reshape(1, b, S5_GROUPS, S5_STATE)
    return (hp.reshape(bp, sp, D_MODEL), hs.reshape(bs, ts_len, D_MODEL),
            ckv_p.reshape(1, bp, sp, KV_LORA), kpe_p.reshape(1, bp, sp, ROPE_DIM), gla_p[None],
            st(s5r_p, bp), st(s5i_p, bp),
            ckv_s.reshape(1, bs, ts_len, KV_LORA), kpe_s.reshape(1, bs, ts_len, ROPE_DIM), gla_s[None],
            st(s5r_s, bs), st(s5i_s, bs))
```

```python
import functools

import jax
import jax.numpy as jnp
from jax import lax
from jax.experimental import pallas as pl
from jax.experimental.pallas import tpu as pltpu

F32 = jnp.float32
BF16 = jnp.bfloat16

D_MODEL = 1024
PAGE_SIZE = 128
MLA_HEADS = 8
Q_LORA = 384
KV_LORA = 256
NOPE_DIM = 64
ROPE_DIM = 32
QK_DIM = NOPE_DIM + ROPE_DIM
V_DIM = 64
ROPE_THETA = 10000.0
GLA_HEADS = 4
GLA_DK = 128
GLA_DV = 128
GLA_GATE_RANK = 16
GLA_TAU = 16.0
GLA_CHUNK = 64
S5_GROUP = 16
S5_GROUPS = D_MODEL // S5_GROUP
S5_STATE = 64
D_FF = 4 * D_MODEL
PLE_DIM = 256
EPS = 1e-6
IN_SPLITS = (Q_LORA, KV_LORA, ROPE_DIM, GLA_HEADS * GLA_DK, GLA_HEADS * GLA_DK,
             GLA_HEADS * GLA_DV, GLA_GATE_RANK, GLA_HEADS * GLA_DV)

LANES = 128
SUBLANES = 8
HEAD_PAD = LANES
VMEM_LIMIT = 56 * 1024 * 1024
S5_SLABS = D_MODEL // LANES
S5_SLAB_STATE = (LANES // S5_GROUP) * S5_STATE
NEG_INF = float("-inf")
LOG2_E = 1.4426950408889634


def _vmem_spec():
    return pl.BlockSpec(memory_space=pltpu.VMEM)


def _params(*sem):
    return pltpu.CompilerParams(dimension_semantics=sem, vmem_limit_bytes=VMEM_LIMIT)


def _bdot(a, b):
    return jnp.dot(a.astype(BF16), b.astype(BF16), preferred_element_type=F32)


def _bdot_nt(a, b):
    return lax.dot_general(a.astype(BF16), b.astype(BF16), (((1,), (1,)), ((), ())),
                           preferred_element_type=F32)


def _bdot_tn(a, b):
    return lax.dot_general(a.astype(BF16), b.astype(BF16), (((0,), (0,)), ((), ())),
                           preferred_element_type=F32)


def _rms(x, g):
    return x * lax.rsqrt(jnp.mean(x * x, axis=-1, keepdims=True) + EPS) * g


def _sigmoid(x):
    return 1.0 / (1.0 + jnp.exp(-x))


def _rope_table_kernel(cos_ref, sin_ref):
    shape = cos_ref.shape
    half = ROPE_DIM // 2
    per_row = LANES // half
    row = lax.broadcasted_iota(jnp.int32, shape, 0)
    lane = lax.broadcasted_iota(jnp.int32, shape, 1)
    pos = (row * per_row + lane // half).astype(F32)
    freq = jnp.power(F32(ROPE_THETA), -(lane % half).astype(F32) / half)
    ang = pos * freq
    cos_ref[...] = jnp.cos(ang)
    sin_ref[...] = jnp.sin(ang)


def _rope_tables(n_pos):
    half = ROPE_DIM // 2
    per_row = LANES // half
    rows = -(-n_pos // per_row)
    rows = -(-rows // SUBLANES) * SUBLANES
    cos, sin = pl.pallas_call(
        _rope_table_kernel,
        out_shape=(jax.ShapeDtypeStruct((rows, LANES), F32),) * 2,
        name="rope_table",
    )()
    cos = cos.reshape(rows * per_row, half)[:n_pos]
    sin = sin.reshape(rows * per_row, half)[:n_pos]
    return cos, sin


def _front_kernel(x_ref, tc_ref, ts_ref, gn_ref, wcq_ref, wckv_ref, wkpe_ref, wgq_ref, wgk_ref, wgv_ref,
                  wgr_ref, wga_ref, gql_ref, wuq_ref, gkvl_ref, wuk_ref, wuv_ref, gqh_ref, gkh_ref,
                  wa2_ref, ba_ref,
                  q_ref, k_ref, v_ref, ckv_ref, kpe_ref, gq_ref, gk_ref, gv_ref, lg_ref, gr_ref):
    n = _rms(x_ref[...], gn_ref[...])
    nb = n.astype(BF16)
    tc = tc_ref[...]
    ts = ts_ref[...]
    lane = lax.broadcasted_iota(jnp.int32, tc.shape, 1)
    first_half = lane < NOPE_DIM + ROPE_DIM // 2

    def norm_rope(xh, g):
        ss = jnp.sum(xh * xh, axis=-1, keepdims=True) * (1.0 / QK_DIM)
        y = xh * lax.rsqrt(ss + EPS) * g
        partner = jnp.where(first_half, pltpu.roll(y, LANES - ROPE_DIM // 2, 1), pltpu.roll(y, ROPE_DIM // 2, 1))
        return y * tc + partner * ts

    cq = _rms(jnp.dot(nb, wcq_ref[...], preferred_element_type=F32), gql_ref[...])
    q = _bdot(cq, wuq_ref[...])
    ckv = _rms(jnp.dot(nb, wckv_ref[...], preferred_element_type=F32), gkvl_ref[...])
    ckv_ref[...] = ckv
    ckvb = ckv.astype(BF16)
    kn = jnp.dot(ckvb, wuk_ref[...], preferred_element_type=F32)
    v_ref[...] = jnp.dot(ckvb, wuv_ref[...], preferred_element_type=F32).astype(v_ref.dtype)
    kpe = jnp.dot(nb, wkpe_ref[...], preferred_element_type=F32)
    kpe_ref[...] = kpe
    gqh = gqh_ref[...]
    gkh = gkh_ref[...]
    scale = QK_DIM ** -0.5 * LOG2_E
    for h in range(MLA_HEADS):
        hs = slice(h * HEAD_PAD, (h + 1) * HEAD_PAD)
        q_ref[:, hs] = (norm_rope(q[:, hs], gqh) * scale).astype(q_ref.dtype)
        k_ref[:, hs] = norm_rope(kn[:, hs] + kpe, gkh).astype(k_ref.dtype)

    gq_ref[...] = jnp.dot(nb, wgq_ref[...], preferred_element_type=F32) * (GLA_DK ** -0.5)
    gk_ref[...] = jnp.dot(nb, wgk_ref[...], preferred_element_type=F32)
    gv_ref[...] = jnp.dot(nb, wgv_ref[...], preferred_element_type=F32)
    gr_ref[...] = jnp.dot(nb, wgr_ref[...], preferred_element_type=F32)
    ga = jnp.dot(nb, wga_ref[...], preferred_element_type=F32)
    logit = _bdot(ga, wa2_ref[...]) + ba_ref[...]
    log_sig = jnp.minimum(logit, 0.0) - jnp.log1p(jnp.exp(-jnp.abs(logit)))
    lg_ref[...] = log_sig * (1.0 / GLA_TAU)


def _front_weights(w_in, g_norm, g_qlat, w_uq, g_kvlat, w_ukv, g_qh, g_kh, w_a2, b_a):
    offs = [0]
    for s in IN_SPLITS:
        offs.append(offs[-1] + s)
    cols = [w_in[:, offs[i]:offs[i + 1]] for i in range(len(IN_SPLITS))]
    w_cq, w_ckv, w_kpe, w_gq, w_gk, w_gv, w_ga, w_gr = cols
    pad_head = HEAD_PAD - QK_DIM
    w_kpe_placed = jnp.pad(w_kpe, ((0, 0), (NOPE_DIM, HEAD_PAD - QK_DIM)))
    w_ga_pad = jnp.pad(w_ga, ((0, 0), (0, LANES - GLA_GATE_RANK)))
    w_a2_pad = jnp.pad(w_a2, ((0, LANES - GLA_GATE_RANK), (0, 0)))
    w_uq_pad = jnp.pad(w_uq.reshape(Q_LORA, MLA_HEADS, QK_DIM), ((0, 0), (0, 0), (0, pad_head)))
    w_uq_pad = w_uq_pad.reshape(Q_LORA, MLA_HEADS * HEAD_PAD)
    w_ukv3 = w_ukv.reshape(KV_LORA, MLA_HEADS, NOPE_DIM + V_DIM)
    w_uk = w_ukv3[:, :, :NOPE_DIM]
    w_uv = w_ukv3[:, :, NOPE_DIM:].reshape(KV_LORA, MLA_HEADS * V_DIM)
    w_uk_pad = jnp.pad(w_uk, ((0, 0), (0, 0), (0, HEAD_PAD - NOPE_DIM))).reshape(KV_LORA, MLA_HEADS * HEAD_PAD)
    g_qh_pad = jnp.pad(g_qh, (0, pad_head))[None, :]
    g_kh_pad = jnp.pad(g_kh, (0, pad_head))[None, :]
    bf = lambda a: a.astype(BF16)
    weights = (g_norm[None, :], bf(w_cq), bf(w_ckv), bf(w_kpe_placed), bf(w_gq), bf(w_gk), bf(w_gv), bf(w_gr),
               bf(w_ga_pad), g_qlat[None, :], bf(w_uq_pad), g_kvlat[None, :], bf(w_uk_pad), bf(w_uv),
               g_qh_pad, g_kh_pad, bf(w_a2_pad), b_a[None, :])
    return weights, w_uk, w_uv


def _front(x, tc, ts, weights, *, tm, table_tiles):
    n = x.shape[0]
    hk = GLA_HEADS * GLA_DK
    row = lambda w: pl.BlockSpec((tm, w), lambda i: (i, 0))
    tab = pl.BlockSpec((tm, LANES), lambda i: (i % table_tiles, 0))
    sds = lambda w, dt: jax.ShapeDtypeStruct((n, w), dt)
    out_shape = (sds(MLA_HEADS * HEAD_PAD, BF16), sds(MLA_HEADS * HEAD_PAD, BF16), sds(MLA_HEADS * V_DIM, BF16),
                 sds(KV_LORA, F32), sds(LANES, F32), sds(hk, F32), sds(hk, F32), sds(hk, F32), sds(hk, F32),
                 sds(hk, F32))
    out_specs = (row(MLA_HEADS * HEAD_PAD), row(MLA_HEADS * HEAD_PAD), row(MLA_HEADS * V_DIM), row(KV_LORA),
                 row(LANES), row(hk), row(hk), row(hk), row(hk), row(hk))
    return pl.pallas_call(
        _front_kernel,
        grid=(n // tm,),
        in_specs=[row(D_MODEL), tab, tab] + [_vmem_spec()] * len(weights),
        out_specs=out_specs,
        out_shape=out_shape,
        compiler_params=_params("parallel"),
        name="ab_front",
    )(x, tc, ts, *weights)


def _prompt_attn_kernel(q_ref, k_ref, v_ref, o_ref, m_ref, l_ref, acc_ref, *, tq):
    i = pl.program_id(1)
    row = lax.broadcasted_iota(jnp.int32, (tq, tq), 0)
    col = lax.broadcasted_iota(jnp.int32, (tq, tq), 1)
    causal = col <= row
    m_ref[...] = jnp.full(m_ref.shape, NEG_INF, F32)
    l_ref[...] = jnp.zeros(l_ref.shape, F32)
    acc_ref[...] = jnp.zeros(acc_ref.shape, F32)

    def step(j, masked):
        start = pl.multiple_of(j * tq, tq)
        for h in range(MLA_HEADS):
            hs = slice(h * HEAD_PAD, (h + 1) * HEAD_PAD)
            vs = slice((h // 2) * LANES, (h // 2 + 1) * LANES)
            s = lax.dot_general(q_ref[:, hs], k_ref[pl.ds(start, tq), hs], (((1,), (1,)), ((), ())),
                                preferred_element_type=F32)
            if masked:
                s = jnp.where(causal, s, NEG_INF)
            m_old = m_ref[h]
            m_new = jnp.maximum(m_old, jnp.max(s, axis=-1, keepdims=True))
            alpha = jnp.exp2(m_old - m_new)
            p = jnp.exp2(s - m_new)
            l_ref[h] = alpha * l_ref[h] + jnp.sum(p, axis=-1, keepdims=True)
            acc_ref[h] = alpha * acc_ref[h] + jnp.dot(p.astype(BF16), v_ref[pl.ds(start, tq), vs],
                                                      preferred_element_type=F32)
            m_ref[h] = m_new

    def body(j, carry):
        step(j, False)
        return carry

    lax.fori_loop(0, i, body, 0)
    step(i, True)
    lane = lax.broadcasted_iota(jnp.int32, (tq, LANES), 1)
    for pair in range(MLA_HEADS // 2):
        lo = acc_ref[2 * pair] / l_ref[2 * pair]
        hi = acc_ref[2 * pair + 1] / l_ref[2 * pair + 1]
        o_ref[:, pair * LANES:(pair + 1) * LANES] = jnp.where(lane < V_DIM, lo, hi).astype(o_ref.dtype)


def _prompt_attn(q, k, v, *, batch, seq, tq):
    nq = seq // tq
    return pl.pallas_call(
        functools.partial(_prompt_attn_kernel, tq=tq),
        grid=(batch, nq),
        in_specs=[pl.BlockSpec((tq, MLA_HEADS * HEAD_PAD), lambda b, i: (b * nq + i, 0)),
                  pl.BlockSpec((seq, MLA_HEADS * HEAD_PAD), lambda b, i: (b, 0)),
                  pl.BlockSpec((seq, MLA_HEADS * V_DIM), lambda b, i: (b, 0))],
        out_specs=pl.BlockSpec((tq, MLA_HEADS * V_DIM), lambda b, i: (b * nq + i, 0)),
        out_shape=jax.ShapeDtypeStruct((batch * seq, MLA_HEADS * V_DIM), BF16),
        scratch_shapes=[pltpu.VMEM((MLA_HEADS, tq, 1), F32), pltpu.VMEM((MLA_HEADS, tq, 1), F32),
                        pltpu.VMEM((MLA_HEADS, tq, LANES), F32)],
        compiler_params=_params("parallel", "arbitrary"),
        name="mla_prompt_attn",
    )(q, k, v)


def _gla_intra(q, k, b):
    c = q.shape[0]
    row = lax.broadcasted_iota(jnp.int32, q.shape, 0)
    r2 = lax.broadcasted_iota(jnp.int32, (c, c), 0)
    c2 = lax.broadcasted_iota(jnp.int32, (c, c), 1)
    att = jnp.where(r2 == c2, _bdot_nt(q, k), 0.0)
    before = pltpu.roll(b, 1, 0)
    last = b
    m = 1
    while m < c:
        upper = (row // m) % 2 == 1
        qm = jnp.where(upper, q * jnp.exp(jnp.minimum(b - before, 0.0)), 0.0)
        km = jnp.where(upper, 0.0, k * jnp.exp(jnp.minimum(last - b, 0.0)))
        same = (r2 // (2 * m)) == (c2 // (2 * m))
        att = att + jnp.where(same, _bdot_nt(qm, km), 0.0)
        if 2 * m < c:
            before = jnp.where(upper, pltpu.roll(before, m, 0), before)
            last = jnp.where(upper, last, pltpu.roll(last, c - m, 0))
        m *= 2
    return att


def _cumsum_rows(g):
    c = g.shape[0]
    row = lax.broadcasted_iota(jnp.int32, g.shape, 0)
    b = g
    sh = 1
    while sh < c:
        b = b + jnp.where(row >= sh, pltpu.roll(b, sh, 0), 0.0)
        sh *= 2
    return b


def _gla_prompt_kernel(gq_ref, gk_ref, gv_ref, lg_ref, gr_ref, go_ref, o_ref, st_ref, state_ref, *, n_chunks):
    t = pl.program_id(1)

    @pl.when(t == 0)
    def _():
        state_ref[...] = jnp.zeros_like(state_ref)

    go = go_ref[...]

    def chunk(ci, carry):
        rows = pl.ds(pl.multiple_of(ci * GLA_CHUNK, GLA_CHUNK), GLA_CHUNK)
        for h in range(GLA_HEADS):
            hs = slice(h * GLA_DK, (h + 1) * GLA_DK)
            q = gq_ref[rows, hs]
            k = gk_ref[rows, hs]
            v = gv_ref[rows, hs]
            b = _cumsum_rows(lg_ref[rows, hs])
            st = state_ref[h]
            inter = _bdot_nt(q * jnp.exp(b), st)
            o = inter + _bdot(_gla_intra(q, k, b), v)
            bl = b[GLA_CHUNK - 1:GLA_CHUNK, :]
            state_ref[h] = st * jnp.exp(bl) + _bdot_tn(v, k * jnp.exp(bl - b))
            gr = gr_ref[rows, hs]
            o_ref[rows, hs] = (_rms(o, go) * (gr * _sigmoid(gr))).astype(o_ref.dtype)
        return carry

    lax.fori_loop(0, n_chunks, chunk, 0)

    @pl.when(t == pl.num_programs(1) - 1)
    def _():
        st_ref[0] = state_ref[...]


def _gla_prompt(gq, gk, gv, lg, gr, g_o, *, batch, seq, tg):
    nt = seq // tg
    hk = GLA_HEADS * GLA_DK
    row = pl.BlockSpec((tg, hk), lambda b, t: (b * nt + t, 0))
    return pl.pallas_call(
        functools.partial(_gla_prompt_kernel, n_chunks=tg // GLA_CHUNK),
        grid=(batch, nt),
        in_specs=[row, row, row, row, row, _vmem_spec()],
        out_specs=(row, pl.BlockSpec((1, GLA_HEADS, GLA_DV, GLA_DK), lambda b, t: (b, 0, 0, 0))),
        out_shape=(jax.ShapeDtypeStruct((batch * seq, hk), BF16),
                   jax.ShapeDtypeStruct((batch, GLA_HEADS, GLA_DV, GLA_DK), F32)),
        scratch_shapes=[pltpu.VMEM((GLA_HEADS, GLA_DV, GLA_DK), F32)],
        compiler_params=_params("parallel", "arbitrary"),
        name="gla_prompt",
    )(gq, gk, gv, lg, gr, g_o[None, :])


def _gla_sample_kernel(gq_ref, gk_ref, gv_ref, lg_ref, gr_ref, go_ref, s0_ref, o_ref, s1_ref, oacc_ref,
                       *, nb, t_len):
    rows = nb * t_len
    pad = jnp.zeros((LANES - rows, LANES), F32)

    def cols(x):
        return jnp.transpose(jnp.concatenate([x, pad], axis=0))

    for h in range(GLA_HEADS):
        hs = slice(h * GLA_DK, (h + 1) * GLA_DK)
        qt = cols(gq_ref[:, hs])
        kt = cols(gk_ref[:, hs])
        at = cols(jnp.exp(lg_ref[:, hs]))
        for bi in range(nb):
            s = s0_ref[bi, h]
            for t in range(t_len):
                r = bi * t_len + t
                v = gv_ref[r:r + 1, hs]
                s = s * at[:, r:r + 1] + kt[:, r:r + 1] * v
                oacc_ref[r:r + 1, hs] = jnp.sum(s * qt[:, r:r + 1], axis=0, keepdims=True)
            s1_ref[bi, h] = s
    go = go_ref[...]
    for h in range(GLA_HEADS):
        hs = slice(h * GLA_DK, (h + 1) * GLA_DK)
        gr = gr_ref[:, hs]
        o_ref[:, hs] = (_rms(oacc_ref[:, hs], go) * (gr * _sigmoid(gr))).astype(o_ref.dtype)


def _gla_sample(gq, gk, gv, lg, gr, g_o, s0, *, t_len, nb):
    n = gq.shape[0]
    batch = n // t_len
    hk = GLA_HEADS * GLA_DK
    rows = nb * t_len
    row = pl.BlockSpec((rows, hk), lambda i: (i, 0))
    st = pl.BlockSpec((nb, GLA_HEADS, GLA_DK, GLA_DV), lambda i: (i, 0, 0, 0))
    return pl.pallas_call(
        functools.partial(_gla_sample_kernel, nb=nb, t_len=t_len),
        grid=(batch // nb,),
        in_specs=[row, row, row, row, row, _vmem_spec(), st],
        out_specs=(row, st),
        out_shape=(jax.ShapeDtypeStruct((n, hk), BF16), jax.ShapeDtypeStruct(s0.shape, s0.dtype)),
        scratch_shapes=[pltpu.VMEM((rows, hk), F32)],
        compiler_params=_params("parallel"),
        name="gla_sample",
    )(gq, gk, gv, lg, gr, g_o[None, :], s0)


def _sample_attn_kernel(pt_ref, q_ref, cnew_ref, pnew_ref, kc_ref, ks_ref, kcn_ref, ksn_ref, gk_ref, gpe_ref,
                        wukt_ref, wukp_ref, wuv_ref, ckv_hbm, kpe_hbm, o_ref, s_scr, c_scr, ckv_buf, kpe_buf, sem,
                        *, n_pages, chunk_pages, t_len):
    n_cols = t_len * MLA_HEADS
    chunk = chunk_pages * PAGE_SIZE
    n_chunks = n_pages // chunk_pages
    past = n_pages * PAGE_SIZE
    b = pl.program_id(0)
    slot = b % 2

    def page_copies(batch, sl, i):
        page = pt_ref[batch, i]
        ci = i // chunk_pages
        return (pltpu.make_async_copy(ckv_hbm.at[page], ckv_buf.at[sl, i], sem.at[sl, ci, 0]),
                pltpu.make_async_copy(kpe_hbm.at[page], kpe_buf.at[sl, i], sem.at[sl, ci, 1]))

    def start_pages(batch, sl, pages):
        for i in pages:
            for cp in page_copies(batch, sl, i):
                cp.start()

    def wait_pages(pages):
        for i in pages:
            for cp in page_copies(b, slot, i):
                cp.wait()

    @pl.when(b == 0)
    def _():
        start_pages(0, 0, range(n_pages))

    qg = q_ref[...].astype(F32) * gk_ref[...]
    gpe = gpe_ref[...]
    sub = lax.broadcasted_iota(jnp.int32, (SUBLANES, MLA_HEADS * HEAD_PAD), 0)
    lane_head = lax.broadcasted_iota(jnp.int32, (SUBLANES, MLA_HEADS * HEAD_PAD), 1) // HEAD_PAD
    qexp = jnp.concatenate(
        [jnp.where(lane_head == sub, jnp.broadcast_to(qg[t:t + 1, :], sub.shape), 0.0) for t in range(t_len)],
        axis=0)
    qt = _bdot(qexp, wukt_ref[...])
    lhs = jnp.concatenate([wukp_ref[...], qt[:, :KV_LORA].astype(BF16)], axis=0)
    qr = qt[:, KV_LORA:KV_LORA + ROPE_DIM].astype(BF16)
    n_norm = NOPE_DIM * MLA_HEADS
    half = ROPE_DIM // 2

    def scores(c, kt, kcos, ksin):
        n = c.shape[0]
        cb = c.astype(BF16)
        r = lax.dot_general(lhs, cb, (((1,), (1,)), ((), ())), preferred_element_type=F32)
        kn = r[:n_norm].reshape(NOPE_DIM, MLA_HEADS, n)
        ss = jnp.sum(kn * kn, axis=0) + jnp.sum(kt * kt, axis=0, keepdims=True)
        rs = lax.rsqrt(ss * (1.0 / QK_DIM) + EPS)
        kg = kt * gpe
        kr = kg * kcos + jnp.concatenate([kg[half:], kg[:half]], axis=0) * ksin
        s = r[n_norm:] + jnp.dot(qr, kr.astype(BF16), preferred_element_type=F32)
        s = (s.reshape(t_len, MLA_HEADS, n) * rs[None]).reshape(n_cols, n)
        return cb, s

    m = jnp.full((n_cols, 1), NEG_INF, F32)
    for ci in range(n_chunks):
        pages = range(ci * chunk_pages, (ci + 1) * chunk_pages)

        @pl.when(b + 1 < pl.num_programs(0))
        def _(pages=pages):
            start_pages(b + 1, 1 - slot, pages)

        wait_pages(pages)
        c = ckv_buf[slot, pl.ds(ci * chunk_pages, chunk_pages)].reshape(chunk, KV_LORA)
        kt = jnp.concatenate([kpe_buf[slot, p] for p in pages], axis=1)
        keys = slice(ci * chunk, (ci + 1) * chunk)
        cb, s = scores(c, kt, kc_ref[:, keys], ks_ref[:, keys])
        c_scr[keys, :] = cb
        s_scr[:, keys] = s
        m = jnp.maximum(m, jnp.max(s, axis=1, keepdims=True))

    zpad = lambda a: jnp.concatenate([a, jnp.zeros((PAGE_SIZE - t_len, a.shape[1]), a.dtype)], axis=0)
    pnew = jnp.concatenate([zpad(pnew_ref[...]), jnp.zeros((PAGE_SIZE, LANES - ROPE_DIM), F32)], axis=1)
    cb, s = scores(zpad(cnew_ref[...]), jnp.transpose(pnew)[:ROPE_DIM], kcn_ref[...], ksn_ref[...])
    r2 = lax.broadcasted_iota(jnp.int32, (n_cols, PAGE_SIZE), 0)
    c2 = lax.broadcasted_iota(jnp.int32, (n_cols, PAGE_SIZE), 1)
    s = jnp.where((c2 < t_len) & (c2 <= r2 // MLA_HEADS), s, NEG_INF)
    new_keys = slice(past, past + PAGE_SIZE)
    c_scr[new_keys, :] = cb
    s_scr[:, new_keys] = s
    m = jnp.maximum(m, jnp.max(s, axis=1, keepdims=True))

    l = jnp.zeros((n_cols, 1), F32)
    acc = jnp.zeros((n_cols, KV_LORA), F32)
    bounds = [(ci * chunk, chunk) for ci in range(n_chunks)] + [(past, PAGE_SIZE)]
    for start, size in bounds:
        keys = slice(start, start + size)
        p = jnp.exp2(s_scr[:, keys] - m)
        l = l + jnp.sum(p, axis=1, keepdims=True)
        acc = acc + jnp.dot(p.astype(BF16), c_scr[keys, :], preferred_element_type=F32)

    lat = acc / l
    res = _bdot(lat, wuv_ref[...])
    r3 = lax.broadcasted_iota(jnp.int32, res.shape, 0)
    c3 = lax.broadcasted_iota(jnp.int32, res.shape, 1)
    res = jnp.where(c3 // V_DIM == r3 % MLA_HEADS, res, 0.0)
    o_ref[...] = jnp.sum(res.reshape(t_len, MLA_HEADS, MLA_HEADS * V_DIM), axis=1).astype(o_ref.dtype)


def _sample_attn(q, ckv_new, kpe_new, cache_ckv, cache_kpe, page_table, kc, ks, g_kh, w_uk, w_uv,
                 *, t_len, chunk_pages):
    batch, n_pages = page_table.shape
    past = n_pages * PAGE_SIZE
    w_ukt = jnp.pad(jnp.transpose(w_uk, (1, 2, 0)), ((0, 0), (0, HEAD_PAD - NOPE_DIM), (0, 0)))
    w_ukt = w_ukt.reshape(MLA_HEADS * HEAD_PAD, KV_LORA)
    sel = jnp.pad(jnp.eye(ROPE_DIM, dtype=F32), ((NOPE_DIM, HEAD_PAD - QK_DIM), (0, LANES - ROPE_DIM)))
    w_ukt_ext = jnp.concatenate([w_ukt, jnp.tile(sel, (MLA_HEADS, 1))], axis=1).astype(BF16)
    w_uk_perm = jnp.transpose(w_uk, (2, 1, 0)).reshape(NOPE_DIM * MLA_HEADS, KV_LORA).astype(BF16)
    g_k_pad = jnp.concatenate([g_kh[:NOPE_DIM], jnp.ones((ROPE_DIM,), F32), jnp.zeros((HEAD_PAD - QK_DIM,), F32)])
    g_k_pad = jnp.tile(g_k_pad, MLA_HEADS)[None, :]
    g_pe = g_kh[NOPE_DIM:][:, None]
    kc, ks = jnp.transpose(kc), jnp.transpose(ks)
    kcn = jnp.pad(kc[:, past:past + t_len], ((0, 0), (0, PAGE_SIZE - t_len)))
    ksn = jnp.pad(ks[:, past:past + t_len], ((0, 0), (0, PAGE_SIZE - t_len)))
    per_b = lambda w: pl.BlockSpec((None, t_len, w), lambda b, pt: (b, 0, 0))
    hbm = pl.BlockSpec(memory_space=pl.ANY)
    in_specs = ([per_b(MLA_HEADS * HEAD_PAD), per_b(KV_LORA), per_b(ROPE_DIM)] + [_vmem_spec()] * 9 + [hbm, hbm])
    grid_spec = pltpu.PrefetchScalarGridSpec(
        num_scalar_prefetch=1,
        grid=(batch,),
        in_specs=in_specs,
        out_specs=pl.BlockSpec((None, t_len, MLA_HEADS * V_DIM), lambda b, pt: (b, 0, 0)),
        scratch_shapes=[pltpu.VMEM((t_len * MLA_HEADS, past + PAGE_SIZE), F32),
                        pltpu.VMEM((past + PAGE_SIZE, KV_LORA), BF16),
                        pltpu.VMEM((2, n_pages, PAGE_SIZE, KV_LORA), F32),
                        pltpu.VMEM((2, n_pages, ROPE_DIM, PAGE_SIZE), F32),
                        pltpu.SemaphoreType.DMA((2, n_pages // chunk_pages, 2))],
    )
    out = pl.pallas_call(
        functools.partial(_sample_attn_kernel, n_pages=n_pages, chunk_pages=chunk_pages, t_len=t_len),
        grid_spec=grid_spec,
        out_shape=jax.ShapeDtypeStruct((batch, t_len, MLA_HEADS * V_DIM), BF16),
        compiler_params=_params("arbitrary"),
        name="mla_sample_attn",
    )(page_table, q.reshape(batch, t_len, -1), ckv_new.reshape(batch, t_len, -1),
      kpe_new.reshape(batch, t_len, -1), kc[:, :past], ks[:, :past], kcn, ksn, g_k_pad, g_pe,
      w_ukt_ext, w_uk_perm, w_uv.astype(BF16), cache_ckv, jnp.swapaxes(cache_kpe, 1, 2))
    return out.reshape(batch * t_len, MLA_HEADS * V_DIM)


def _mlp_ple(h, p_ref, gm_ref, wup_ref, wdown_ref, gp_ref, wgate_ref, wproj_ref, o_ref, ff_chunk):
    nb = _rms(h, gm_ref[...]).astype(BF16)
    y = jnp.zeros_like(h)
    for c in range(D_FF // ff_chunk):
        cs = slice(c * ff_chunk, (c + 1) * ff_chunk)
        a = jnp.maximum(jnp.dot(nb, wup_ref[:, cs], preferred_element_type=F32), 0.0)
        y = y + jnp.dot((a * a).astype(BF16), wdown_ref[cs, :], preferred_element_type=F32)
    h = h + y
    gate = _sigmoid(_bdot(_rms(h, gp_ref[...]), wgate_ref[...]))
    o_ref[...] = h + gate * _bdot(p_ref[...], wproj_ref[...])


def _post_mix_kernel(h_ref, a_ref, o2_ref, p_ref, wout_ref, gm_ref, wup_ref, wdown_ref, gp_ref, wgate_ref,
                     wproj_ref, o_ref, *, ff_chunk):
    n_attn = MLA_HEADS * V_DIM
    h = (h_ref[...] + jnp.dot(a_ref[...], wout_ref[:n_attn, :], preferred_element_type=F32)
         + jnp.dot(o2_ref[...], wout_ref[n_attn:, :], preferred_element_type=F32))
    _mlp_ple(h, p_ref, gm_ref, wup_ref, wdown_ref, gp_ref, wgate_ref, wproj_ref, o_ref, ff_chunk)


def _post_kernel(h_ref, p_ref, gm_ref, wup_ref, wdown_ref, gp_ref, wgate_ref, wproj_ref, o_ref, *, ff_chunk):
    _mlp_ple(h_ref[...], p_ref, gm_ref, wup_ref, wdown_ref, gp_ref, wgate_ref, wproj_ref, o_ref, ff_chunk)


def _post(h, p, weights, mix=None, *, tm, ff_chunk=1024):
    n = h.shape[0]
    row = lambda w: pl.BlockSpec((tm, w), lambda i: (i, 0))
    if mix is None:
        body, acts, specs = _post_kernel, (h, p), [row(D_MODEL), row(PLE_DIM)]
    else:
        attn, o_gla, w_out = mix
        body, acts = _post_mix_kernel, (h, attn, o_gla, p)
        specs = [row(D_MODEL), row(attn.shape[1]), row(o_gla.shape[1]), row(PLE_DIM)]
        weights = (w_out,) + tuple(weights)
    return pl.pallas_call(
        functools.partial(body, ff_chunk=ff_chunk),
        grid=(n // tm,),
        in_specs=specs + [_vmem_spec()] * len(weights),
        out_specs=row(D_MODEL),
        out_shape=jax.ShapeDtypeStruct((n, D_MODEL), F32),
        compiler_params=_params("parallel"),
        name="post_mix" if mix is not None else "post",
    )(*acts, *weights)


def _s5_prep_kernel(are_ref, aim_ref, ldt_ref, bre_ref, bim_ref, lr_ref, li_ref, bbr_ref, bbi_ref):
    a_re = are_ref[...]
    a_im = aim_ref[...]
    dt = jnp.exp(ldt_ref[...])
    mag = jnp.exp(a_re * dt)
    lr = mag * jnp.cos(a_im * dt)
    li = mag * jnp.sin(a_im * dt)
    den = a_re * a_re + a_im * a_im
    cr = ((lr - 1.0) * a_re + li * a_im) / den
    ci = (li * a_re - (lr - 1.0) * a_im) / den
    lr_ref[...] = lr
    li_ref[...] = li
    b_re = bre_ref[...]
    b_im = bim_ref[...]
    bbr_ref[...] = cr[:, None, :] * b_re - ci[:, None, :] * b_im
    bbi_ref[...] = cr[:, None, :] * b_im + ci[:, None, :] * b_re


def _s5_weights(a_re, a_im, log_dt, b_re, b_im, c_re, c_im):
    g, p_, ch = S5_GROUPS, S5_STATE, S5_GROUP
    gp = jax.ShapeDtypeStruct((g, p_), F32)
    gcp = jax.ShapeDtypeStruct((g, ch, p_), F32)
    lr, li, bbr, bbi = pl.pallas_call(
        _s5_prep_kernel, out_shape=(gp, gp, gcp, gcp), name="s5_prep",
    )(a_re, a_im, log_dt[:, None], jnp.transpose(b_re, (0, 2, 1)), jnp.transpose(b_im, (0, 2, 1)))
    per_slab = LANES // ch
    eye = jnp.eye(per_slab, dtype=F32)

    def b_blocks(bb):
        bb = bb.reshape(S5_SLABS, per_slab, ch, p_)
        return jnp.einsum('jgcp,gh->jgchp', bb, eye).reshape(S5_SLABS, LANES, S5_SLAB_STATE)

    def c_blocks(cc):
        cc = cc.reshape(S5_SLABS, per_slab, ch, p_)
        return jnp.einsum('jgcp,gh->jgphc', cc, eye).reshape(S5_SLABS, S5_SLAB_STATE, LANES)

    w_b = jnp.concatenate([b_blocks(bbr), b_blocks(bbi)], axis=2).astype(BF16)
    w_c = jnp.concatenate([c_blocks(c_re), -c_blocks(c_im)], axis=1).astype(BF16)
    lam = jnp.stack([lr.reshape(S5_SLABS, S5_SLAB_STATE), li.reshape(S5_SLABS, S5_SLAB_STATE)], axis=1)
    return w_b, w_c, lam


def _s5_kernel(x_ref, x0r_ref, x0i_ref, gn_ref, wb_ref, wc_ref, lam_ref, d_ref, wglu_ref,
               o_ref, xr_ref, xi_ref, bur_ref, bui_ref, sr_ref, si_ref, shuf_ref=None, *, nb, t_len):
    step = pl.program_id(0)
    ns = S5_SLAB_STATE

    @pl.when(step == 0)
    def _():
        for j in range(S5_SLABS):
            sr_ref[j] = x0r_ref[:, j * ns:(j + 1) * ns]
            si_ref[j] = x0i_ref[:, j * ns:(j + 1) * ns]

    x = x_ref[...]
    u = _rms(x, gn_ref[...])
    ub = u.astype(BF16)
    for j in range(S5_SLABS):
        bu = jnp.dot(ub[:, j * LANES:(j + 1) * LANES], wb_ref[j], preferred_element_type=F32)
        bur_ref[j] = bu[:, :ns]
        bui_ref[j] = bu[:, ns:]

    n_groups = nb // SUBLANES

    def slab(j, carry):
        lr = jnp.broadcast_to(lam_ref[j, 0:1, :], (SUBLANES, ns))
        li = jnp.broadcast_to(lam_ref[j, 1:2, :], (SUBLANES, ns))

        def group(gi, carry):
            srow = pl.ds(pl.multiple_of(gi * SUBLANES, SUBLANES), SUBLANES)
            xr0 = sr_ref[j, srow, :]
            xi0 = si_ref[j, srow, :]

            def tok(t, st):
                xr, xi = st
                rows = pl.ds(pl.multiple_of(t * nb + gi * SUBLANES, SUBLANES), SUBLANES)
                nr = lr * xr - li * xi + bur_ref[j, rows, :]
                ni = lr * xi + li * xr + bui_ref[j, rows, :]
                bur_ref[j, rows, :] = nr
                bui_ref[j, rows, :] = ni
                return nr, ni

            xr, xi = lax.fori_loop(0, t_len, tok, (xr0, xi0))
            sr_ref[j, srow, :] = xr
            si_ref[j, srow, :] = xi
            return carry

        return lax.fori_loop(0, n_groups, group, carry)

    lax.fori_loop(0, S5_SLABS, slab, 0)

    ys = []
    for j in range(S5_SLABS):
        xs = jnp.concatenate([bur_ref[j].astype(BF16), bui_ref[j].astype(BF16)], axis=1)
        ys.append(jnp.dot(xs, wc_ref[j], preferred_element_type=F32))
    y = jnp.concatenate(ys, axis=1) + d_ref[...] * u
    gl = jax.nn.gelu(y)
    zg = _bdot(gl, wglu_ref[...])
    h = x + zg[:, :D_MODEL] * _sigmoid(zg[:, D_MODEL:])
    if shuf_ref is None:
        o_ref[...] = h
    else:
        for s in range(D_MODEL // LANES):
            shuf_ref[s] = h[:, s * LANES:(s + 1) * LANES]
        for bi in range(nb):
            for s in range(D_MODEL // LANES):
                o_ref[bi, :, s * LANES:(s + 1) * LANES] = shuf_ref[s, pl.ds(bi, t_len, stride=nb), :]

    @pl.when(step == pl.num_programs(0) - 1)
    def _():
        for j in range(S5_SLABS):
            xr_ref[:, j * ns:(j + 1) * ns] = sr_ref[j]
            xi_ref[:, j * ns:(j + 1) * ns] = si_ref[j]


def _s5(x_tm, x0r, x0i, g_norm, w_b, w_c, lam, d, w_glu, *, nb, t_tile, batch_major_out):
    n = x_tm.shape[0]
    rows = nb * t_tile
    n_state = S5_GROUPS * S5_STATE
    full = lambda shape: pl.BlockSpec(shape, lambda i: (0,) * len(shape))
    st = jax.ShapeDtypeStruct((nb, n_state), F32)
    scratch = [pltpu.VMEM((S5_SLABS, rows, S5_SLAB_STATE), F32),
               pltpu.VMEM((S5_SLABS, rows, S5_SLAB_STATE), F32),
               pltpu.VMEM((S5_SLABS, nb, S5_SLAB_STATE), F32),
               pltpu.VMEM((S5_SLABS, nb, S5_SLAB_STATE), F32)]
    if batch_major_out:
        out_spec = pl.BlockSpec((nb, t_tile, D_MODEL), lambda i: (0, i, 0))
        out_sds = jax.ShapeDtypeStruct((nb, n // nb, D_MODEL), F32)
        scratch.append(pltpu.VMEM((D_MODEL // LANES, rows, LANES), F32))
    else:
        out_spec = pl.BlockSpec((rows, D_MODEL), lambda i: (i, 0))
        out_sds = jax.ShapeDtypeStruct((n, D_MODEL), F32)
    return pl.pallas_call(
        functools.partial(_s5_kernel, nb=nb, t_len=t_tile),
        grid=(n // rows,),
        in_specs=[pl.BlockSpec((rows, D_MODEL), lambda i: (i, 0)), full((nb, n_state)), full((nb, n_state))]
        + [_vmem_spec()] * 6,
        out_specs=(out_spec, full((nb, n_state)), full((nb, n_state))),
        out_shape=(out_sds, st, st),
        scratch_shapes=scratch,
        compiler_params=_params("arbitrary"),
        name="s5_layer",
    )(x_tm, x0r, x0i, g_norm[None, :], w_b, w_c, lam, d[None, :], w_glu.astype(BF16))


def _to_time_major(h, batch, seq):
    return h.reshape(batch, seq, -1).transpose(1, 0, 2).reshape(batch * seq, -1)


def _to_batch_major(h, batch, seq):
    return h.reshape(seq, batch, -1).transpose(1, 0, 2).reshape(batch * seq, -1)


def kernel(x_prompt, x_sample, cache_ckv, cache_kpe, state_gla, state_s5_re, state_s5_im, page_table,
           p_prompt, p_sample, g_norm_ab, w_in_ab, g_qlat, w_uq, g_kvlat, w_ukv, g_qh, g_kh,
           w_gla_a2, b_gla_a, g_gla_o, w_out_ab, g_norm_c, s5_a_re, s5_a_im, s5_log_dt,
           s5_b_re, s5_b_im, s5_c_re, s5_c_im, s5_d, w_glu, g_norm_mlp, w_up, w_down,
           g_norm_ple, w_ple_gate, w_ple_proj):
    bp, sp, _ = x_prompt.shape
    bs, ts_len, _ = x_sample.shape
    n_pages = page_table.shape[1]
    past = n_pages * PAGE_SIZE
    n_p, n_s = bp * sp, bs * ts_len
    bf = lambda a: a.astype(BF16)

    cos16, sin16 = _rope_tables(past + ts_len)
    ones = jnp.ones((cos16.shape[0], 1), F32)
    tc = jnp.concatenate([ones * jnp.ones((1, NOPE_DIM), F32), cos16, cos16,
                          ones * jnp.ones((1, HEAD_PAD - QK_DIM), F32)], axis=1)
    ts = jnp.concatenate([ones * jnp.zeros((1, NOPE_DIM), F32), -sin16, sin16,
                          ones * jnp.zeros((1, HEAD_PAD - QK_DIM), F32)], axis=1)
    kc = jnp.concatenate([cos16, cos16], axis=1)
    ks = jnp.concatenate([-sin16, sin16], axis=1)
    tc_s = jnp.tile(tc[past:past + ts_len], (bs, 1))
    ts_s = jnp.tile(ts[past:past + ts_len], (bs, 1))

    post_w = lambda i: (g_norm_mlp[i][None, :], bf(w_up[i]), bf(w_down[i]), g_norm_ple[i][None, :],
                        bf(w_ple_gate[i]), bf(w_ple_proj[i]))

    fw, w_uk, w_uv = _front_weights(w_in_ab[0], g_norm_ab[0], g_qlat[0], w_uq[0], g_kvlat[0], w_ukv[0],
                                    g_qh[0], g_kh[0], w_gla_a2[0], b_gla_a[0])
    tm_p = 512
    (q_p, k_p, v_p, ckv_p, kpe_p, gq_p, gk_p, gv_p, lg_p, gr_p) = _front(
        x_prompt.reshape(n_p, D_MODEL), tc, ts, fw, tm=tm_p, table_tiles=sp // tm_p)
    (q_s, _, _, ckv_s, kpe_s, gq_s, gk_s, gv_s, lg_s, gr_s) = _front(
        x_sample.reshape(n_s, D_MODEL), tc_s, ts_s, fw, tm=n_s // 2, table_tiles=2)
    kpe_p = kpe_p[:, NOPE_DIM:QK_DIM]
    kpe_s = kpe_s[:, NOPE_DIM:QK_DIM]

    attn_p = _prompt_attn(q_p, k_p, v_p, batch=bp, seq=sp, tq=256)
    o_p, gla_p = _gla_prompt(gq_p, gk_p, gv_p, lg_p, gr_p, g_gla_o[0], batch=bp, seq=sp, tg=256)
    gla_p = jnp.swapaxes(gla_p, -1, -2)

    attn_s = _sample_attn(q_s, ckv_s, kpe_s, cache_ckv[0], cache_kpe[0], page_table, kc, ks, g_kh[0],
                          w_uk, w_uv, t_len=ts_len, chunk_pages=8)
    o_s, gla_s = _gla_sample(gq_s, gk_s, gv_s, lg_s, gr_s, g_gla_o[0], state_gla[0], t_len=ts_len, nb=8)

    w_out = bf(w_out_ab[0])
    hp = _post(x_prompt.reshape(n_p, D_MODEL), p_prompt[0].reshape(n_p, PLE_DIM), post_w(0),
               mix=(attn_p, o_p, w_out), tm=512)
    hs = _post(x_sample.reshape(n_s, D_MODEL), p_sample[0].reshape(n_s, PLE_DIM), post_w(0),
               mix=(attn_s, o_s, w_out), tm=n_s // 2)

    w_b, w_c, lam = _s5_weights(s5_a_re[0], s5_a_im[0], s5_log_dt[0], s5_b_re[0], s5_b_im[0],
                                s5_c_re[0], s5_c_im[0])
    n_state = S5_GROUPS * S5_STATE
    zero_state = jnp.zeros((bp, n_state), F32)
    hp, s5r_p, s5i_p = _s5(_to_time_major(hp, bp, sp), zero_state, zero_state, g_norm_c[0], w_b, w_c, lam,
                           s5_d[0], w_glu[0], nb=bp, t_tile=64, batch_major_out=True)
    hs_tm, s5r_s, s5i_s = _s5(_to_time_major(hs, bs, ts_len), state_s5_re[0].reshape(bs, n_state),
                              state_s5_im[0].reshape(bs, n_state), g_norm_c[0], w_b, w_c, lam,
                              s5_d[0], w_glu[0], nb=bs, t_tile=ts_len, batch_major_out=False)
    hp = _post(hp.reshape(n_p, D_MODEL), p_prompt[1].reshape(n_p, PLE_DIM), post_w(1), tm=512)
    hs = _post(_to_batch_major(hs_tm, bs, ts_len), p_sample[1].reshape(n_s, PLE_DIM), post_w(1), tm=n_s // 2)

    st = lambda a, b: a.reshape(1, b, S5_GROUPS, S5_STATE)
    return (hp.reshape(bp, sp, D_MODEL), hs.reshape(bs, ts_len, D_MODEL),
            ckv_p.reshape(1, bp, sp, KV_LORA), kpe_p.reshape(1, bp, sp, ROPE_DIM), gla_p[None],
            st(s5r_p, bp), st(s5i_p, bp),
            ckv_s.reshape(1, bs, ts_len, KV_LORA), kpe_s.reshape(1, bs, ts_len, ROPE_DIM), gla_s[None],
            st(s5r_s, bs), st(s5i_s, bs))
```

```python
import functools

import jax
import jax.numpy as jnp
from jax import lax
from jax.experimental import pallas as pl
from jax.experimental.pallas import tpu as pltpu

F32 = jnp.float32
BF16 = jnp.bfloat16

D_MODEL = 1024
PAGE_SIZE = 128
MLA_HEADS = 8
Q_LORA = 384
KV_LORA = 256
NOPE_DIM = 64
ROPE_DIM = 32
QK_DIM = NOPE_DIM + ROPE_DIM
V_DIM = 64
ROPE_THETA = 10000.0
GLA_HEADS = 4
GLA_DK = 128
GLA_DV = 128
GLA_GATE_RANK = 16
GLA_TAU = 16.0
GLA_CHUNK = 64
S5_GROUP = 16
S5_GROUPS = D_MODEL // S5_GROUP
S5_STATE = 64
D_FF = 4 * D_MODEL
PLE_DIM = 256
EPS = 1e-6
IN_SPLITS = (Q_LORA, KV_LORA, ROPE_DIM, GLA_HEADS * GLA_DK, GLA_HEADS * GLA_DK,
             GLA_HEADS * GLA_DV, GLA_GATE_RANK, GLA_HEADS * GLA_DV)

LANES = 128
SUBLANES = 8
HEAD_PAD = LANES
VMEM_LIMIT = 56 * 1024 * 1024
S5_SLABS = D_MODEL // LANES
S5_SLAB_STATE = (LANES // S5_GROUP) * S5_STATE
NEG_INF = float("-inf")
LOG2_E = 1.4426950408889634


def _vmem_spec():
    return pl.BlockSpec(memory_space=pltpu.VMEM)


def _params(*sem):
    return pltpu.CompilerParams(dimension_semantics=sem, vmem_limit_bytes=VMEM_LIMIT)


def _bdot(a, b):
    return jnp.dot(a.astype(BF16), b.astype(BF16), preferred_element_type=F32)


def _bdot_nt(a, b):
    return lax.dot_general(a.astype(BF16), b.astype(BF16), (((1,), (1,)), ((), ())),
                           preferred_element_type=F32)


def _bdot_tn(a, b):
    return lax.dot_general(a.astype(BF16), b.astype(BF16), (((0,), (0,)), ((), ())),
                           preferred_element_type=F32)


def _rms(x, g):
    return x * lax.rsqrt(jnp.mean(x * x, axis=-1, keepdims=True) + EPS) * g


def _sigmoid(x):
    return 1.0 / (1.0 + jnp.exp(-x))


def _rope_table_kernel(cos_ref, sin_ref):
    shape = cos_ref.shape
    half = ROPE_DIM // 2
    per_row = LANES // half
    row = lax.broadcasted_iota(jnp.int32, shape, 0)
    lane = lax.broadcasted_iota(jnp.int32, shape, 1)
    pos = (row * per_row + lane // half).astype(F32)
    freq = jnp.power(F32(ROPE_THETA), -(lane % half).astype(F32) / half)
    ang = pos * freq
    cos_ref[...] = jnp.cos(ang)
    sin_ref[...] = jnp.sin(ang)


def _rope_tables(n_pos):
    half = ROPE_DIM // 2
    per_row = LANES // half
    rows = -(-n_pos // per_row)
    rows = -(-rows // SUBLANES) * SUBLANES
    cos, sin = pl.pallas_call(
        _rope_table_kernel,
        out_shape=(jax.ShapeDtypeStruct((rows, LANES), F32),) * 2,
        name="rope_table",
    )()
    cos = cos.reshape(rows * per_row, half)[:n_pos]
    sin = sin.reshape(rows * per_row, half)[:n_pos]
    return cos, sin


def _front_kernel(x_ref, tc_ref, ts_ref, gn_ref, wcq_ref, wckv_ref, wkpe_ref, wgq_ref, wgk_ref, wgv_ref,
                  wgr_ref, wga_ref, gql_ref, wuq_ref, gkvl_ref, wuk_ref, wuv_ref, gqh_ref, gkh_ref,
                  wa2_ref, ba_ref,
                  q_ref, k_ref, v_ref, ckv_ref, kpe_ref, gq_ref, gk_ref, gv_ref, lg_ref, gr_ref):
    n = _rms(x_ref[...], gn_ref[...])
    nb = n.astype(BF16)
    tc = tc_ref[...]
    ts = ts_ref[...]
    lane = lax.broadcasted_iota(jnp.int32, tc.shape, 1)
    first_half = lane < NOPE_DIM + ROPE_DIM // 2

    def norm_rope(xh, g):
        ss = jnp.sum(xh * xh, axis=-1, keepdims=True) * (1.0 / QK_DIM)
        y = xh * lax.rsqrt(ss + EPS) * g
        partner = jnp.where(first_half, pltpu.roll(y, LANES - ROPE_DIM // 2, 1), pltpu.roll(y, ROPE_DIM // 2, 1))
        return y * tc + partner * ts

    cq = _rms(jnp.dot(nb, wcq_ref[...], preferred_element_type=F32), gql_ref[...])
    q = _bdot(cq, wuq_ref[...])
    ckv = _rms(jnp.dot(nb, wckv_ref[...], preferred_element_type=F32), gkvl_ref[...])
    ckv_ref[...] = ckv
    ckvb = ckv.astype(BF16)
    kn = jnp.dot(ckvb, wuk_ref[...], preferred_element_type=F32)
    v_ref[...] = jnp.dot(ckvb, wuv_ref[...], preferred_element_type=F32).astype(v_ref.dtype)
    kpe = jnp.dot(nb, wkpe_ref[...], preferred_element_type=F32)
    kpe_ref[...] = kpe
    gqh = gqh_ref[...]
    gkh = gkh_ref[...]
    scale = QK_DIM ** -0.5 * LOG2_E
    for h in range(MLA_HEADS):
        hs = slice(h * HEAD_PAD, (h + 1) * HEAD_PAD)
        q_ref[:, hs] = (norm_rope(q[:, hs], gqh) * scale).astype(q_ref.dtype)
        k_ref[:, hs] = norm_rope(kn[:, hs] + kpe, gkh).astype(k_ref.dtype)

    gq_ref[...] = jnp.dot(nb, wgq_ref[...], preferred_element_type=F32) * (GLA_DK ** -0.5)
    gk_ref[...] = jnp.dot(nb, wgk_ref[...], preferred_element_type=F32)
    gv_ref[...] = jnp.dot(nb, wgv_ref[...], preferred_element_type=F32)
    gr_ref[...] = jnp.dot(nb, wgr_ref[...], preferred_element_type=F32)
    ga = jnp.dot(nb, wga_ref[...], preferred_element_type=F32)
    logit = _bdot(ga, wa2_ref[...]) + ba_ref[...]
    log_sig = jnp.minimum(logit, 0.0) - jnp.log1p(jnp.exp(-jnp.abs(logit)))
    lg_ref[...] = log_sig * (1.0 / GLA_TAU)


def _front_weights(w_in, g_norm, g_qlat, w_uq, g_kvlat, w_ukv, g_qh, g_kh, w_a2, b_a):
    offs = [0]
    for s in IN_SPLITS:
        offs.append(offs[-1] + s)
    cols = [w_in[:, offs[i]:offs[i + 1]] for i in range(len(IN_SPLITS))]
    w_cq, w_ckv, w_kpe, w_gq, w_gk, w_gv, w_ga, w_gr = cols
    pad_head = HEAD_PAD - QK_DIM
    w_kpe_placed = jnp.pad(w_kpe, ((0, 0), (NOPE_DIM, HEAD_PAD - QK_DIM)))
    w_ga_pad = jnp.pad(w_ga, ((0, 0), (0, LANES - GLA_GATE_RANK)))
    w_a2_pad = jnp.pad(w_a2, ((0, LANES - GLA_GATE_RANK), (0, 0)))
    w_uq_pad = jnp.pad(w_uq.reshape(Q_LORA, MLA_HEADS, QK_DIM), ((0, 0), (0, 0), (0, pad_head)))
    w_uq_pad = w_uq_pad.reshape(Q_LORA, MLA_HEADS * HEAD_PAD)
    w_ukv3 = w_ukv.reshape(KV_LORA, MLA_HEADS, NOPE_DIM + V_DIM)
    w_uk = w_ukv3[:, :, :NOPE_DIM]
    w_uv = w_ukv3[:, :, NOPE_DIM:].reshape(KV_LORA, MLA_HEADS * V_DIM)
    w_uk_pad = jnp.pad(w_uk, ((0, 0), (0, 0), (0, HEAD_PAD - NOPE_DIM))).reshape(KV_LORA, MLA_HEADS * HEAD_PAD)
    g_qh_pad = jnp.pad(g_qh, (0, pad_head))[None, :]
    g_kh_pad = jnp.pad(g_kh, (0, pad_head))[None, :]
    bf = lambda a: a.astype(BF16)
    weights = (g_norm[None, :], bf(w_cq), bf(w_ckv), bf(w_kpe_placed), bf(w_gq), bf(w_gk), bf(w_gv), bf(w_gr),
               bf(w_ga_pad), g_qlat[None, :], bf(w_uq_pad), g_kvlat[None, :], bf(w_uk_pad), bf(w_uv),
               g_qh_pad, g_kh_pad, bf(w_a2_pad), b_a[None, :])
    return weights, w_uk, w_uv


def _front(x, tc, ts, weights, *, tm, table_tiles):
    n = x.shape[0]
    hk = GLA_HEADS * GLA_DK
    row = lambda w: pl.BlockSpec((tm, w), lambda i: (i, 0))
    tab = pl.BlockSpec((tm, LANES), lambda i: (i % table_tiles, 0))
    sds = lambda w, dt: jax.ShapeDtypeStruct((n, w), dt)
    out_shape = (sds(MLA_HEADS * HEAD_PAD, BF16), sds(MLA_HEADS * HEAD_PAD, BF16), sds(MLA_HEADS * V_DIM, BF16),
                 sds(KV_LORA, F32), sds(LANES, F32), sds(hk, F32), sds(hk, F32), sds(hk, F32), sds(hk, F32),
                 sds(hk, F32))
    out_specs = (row(MLA_HEADS * HEAD_PAD), row(MLA_HEADS * HEAD_PAD), row(MLA_HEADS * V_DIM), row(KV_LORA),
                 row(LANES), row(hk), row(hk), row(hk), row(hk), row(hk))
    return pl.pallas_call(
        _front_kernel,
        grid=(n // tm,),
        in_specs=[row(D_MODEL), tab, tab] + [_vmem_spec()] * len(weights),
        out_specs=out_specs,
        out_shape=out_shape,
        compiler_params=_params("parallel"),
        name="ab_front",
    )(x, tc, ts, *weights)


def _prompt_attn_kernel(q_ref, k_ref, v_ref, o_ref, *, tq, n_q):
    i = pl.program_id(1)
    row = lax.broadcasted_iota(jnp.int32, (tq, tq), 0)
    col = lax.broadcasted_iota(jnp.int32, (tq, tq), 1)
    causal = col <= row
    lane = lax.broadcasted_iota(jnp.int32, (tq, LANES), 1)

    def case(c):
        kv = (c + 1) * tq
        outs = []
        for h in range(MLA_HEADS):
            hs = slice(h * HEAD_PAD, (h + 1) * HEAD_PAD)
            vs = slice((h // 2) * LANES, (h // 2 + 1) * LANES)
            s = lax.dot_general(q_ref[:, hs], k_ref[:kv, hs], (((1,), (1,)), ((), ())),
                                preferred_element_type=F32)
            diag = jnp.where(causal, s[:, c * tq:], NEG_INF)
            s = diag if c == 0 else jnp.concatenate([s[:, :c * tq], diag], axis=1)
            p = jnp.exp2(s - jnp.max(s, axis=-1, keepdims=True))
            l = jnp.sum(p, axis=-1, keepdims=True)
            outs.append(jnp.dot(p.astype(BF16), v_ref[:kv, vs], preferred_element_type=F32) / l)
        for pair in range(MLA_HEADS // 2):
            o_ref[:, pair * LANES:(pair + 1) * LANES] = jnp.where(
                lane < V_DIM, outs[2 * pair], outs[2 * pair + 1]).astype(o_ref.dtype)

    for c in range(n_q):
        pl.when(i == c)(functools.partial(case, c))


def _prompt_attn(q, k, v, *, batch, seq, tq):
    nq = seq // tq
    return pl.pallas_call(
        functools.partial(_prompt_attn_kernel, tq=tq, n_q=nq),
        grid=(batch, nq),
        in_specs=[pl.BlockSpec((tq, MLA_HEADS * HEAD_PAD), lambda b, i: (b * nq + i, 0)),
                  pl.BlockSpec((seq, MLA_HEADS * HEAD_PAD), lambda b, i: (b, 0)),
                  pl.BlockSpec((seq, MLA_HEADS * V_DIM), lambda b, i: (b, 0))],
        out_specs=pl.BlockSpec((tq, MLA_HEADS * V_DIM), lambda b, i: (b * nq + i, 0)),
        out_shape=jax.ShapeDtypeStruct((batch * seq, MLA_HEADS * V_DIM), BF16),
        compiler_params=_params("parallel", "arbitrary"),
        name="mla_prompt_attn",
    )(q, k, v)


def _gla_intra(q, k, b):
    c = q.shape[0]
    row = lax.broadcasted_iota(jnp.int32, q.shape, 0)
    r2 = lax.broadcasted_iota(jnp.int32, (c, c), 0)
    c2 = lax.broadcasted_iota(jnp.int32, (c, c), 1)
    att = jnp.where(r2 == c2, _bdot_nt(q, k), 0.0)
    before = pltpu.roll(b, 1, 0)
    last = b
    m = 1
    while m < c:
        upper = (row // m) % 2 == 1
        qm = jnp.where(upper, q * jnp.exp(jnp.minimum(b - before, 0.0)), 0.0)
        km = jnp.where(upper, 0.0, k * jnp.exp(jnp.minimum(last - b, 0.0)))
        same = (r2 // (2 * m)) == (c2 // (2 * m))
        att = att + jnp.where(same, _bdot_nt(qm, km), 0.0)
        if 2 * m < c:
            before = jnp.where(upper, pltpu.roll(before, m, 0), before)
            last = jnp.where(upper, last, pltpu.roll(last, c - m, 0))
        m *= 2
    return att


def _cumsum_rows(g):
    c = g.shape[0]
    row = lax.broadcasted_iota(jnp.int32, g.shape, 0)
    b = g
    sh = 1
    while sh < c:
        b = b + jnp.where(row >= sh, pltpu.roll(b, sh, 0), 0.0)
        sh *= 2
    return b


def _gla_prompt_kernel(gq_ref, gk_ref, gv_ref, lg_ref, gr_ref, go_ref, o_ref, st_ref, state_ref, *, n_chunks):
    t = pl.program_id(1)

    @pl.when(t == 0)
    def _():
        state_ref[...] = jnp.zeros_like(state_ref)

    go = go_ref[...]

    def chunk(ci, carry):
        rows = pl.ds(pl.multiple_of(ci * GLA_CHUNK, GLA_CHUNK), GLA_CHUNK)
        for h in range(GLA_HEADS):
            hs = slice(h * GLA_DK, (h + 1) * GLA_DK)
            q = gq_ref[rows, hs]
            k = gk_ref[rows, hs]
            v = gv_ref[rows, hs]
            b = _cumsum_rows(lg_ref[rows, hs])
            st = state_ref[h]
            inter = _bdot_nt(q * jnp.exp(b), st)
            o = inter + _bdot(_gla_intra(q, k, b), v)
            bl = b[GLA_CHUNK - 1:GLA_CHUNK, :]
            state_ref[h] = st * jnp.exp(bl) + _bdot_tn(v, k * jnp.exp(bl - b))
            gr = gr_ref[rows, hs]
            o_ref[rows, hs] = (_rms(o, go) * (gr * _sigmoid(gr))).astype(o_ref.dtype)
        return carry

    lax.fori_loop(0, n_chunks, chunk, 0)

    @pl.when(t == pl.num_programs(1) - 1)
    def _():
        st_ref[0] = state_ref[...]


def _gla_prompt(gq, gk, gv, lg, gr, g_o, *, batch, seq, tg):
    nt = seq // tg
    hk = GLA_HEADS * GLA_DK
    row = pl.BlockSpec((tg, hk), lambda b, t: (b * nt + t, 0))
    return pl.pallas_call(
        functools.partial(_gla_prompt_kernel, n_chunks=tg // GLA_CHUNK),
        grid=(batch, nt),
        in_specs=[row, row, row, row, row, _vmem_spec()],
        out_specs=(row, pl.BlockSpec((1, GLA_HEADS, GLA_DV, GLA_DK), lambda b, t: (b, 0, 0, 0))),
        out_shape=(jax.ShapeDtypeStruct((batch * seq, hk), BF16),
                   jax.ShapeDtypeStruct((batch, GLA_HEADS, GLA_DV, GLA_DK), F32)),
        scratch_shapes=[pltpu.VMEM((GLA_HEADS, GLA_DV, GLA_DK), F32)],
        compiler_params=_params("parallel", "arbitrary"),
        name="gla_prompt",
    )(gq, gk, gv, lg, gr, g_o[None, :])


def _gla_sample_kernel(gq_ref, gk_ref, gv_ref, lg_ref, gr_ref, go_ref, s0_ref, o_ref, s1_ref, oacc_ref,
                       *, nb, t_len):
    rows = nb * t_len
    pad = jnp.zeros((LANES - rows, LANES), F32)

    def cols(x):
        return jnp.transpose(jnp.concatenate([x, pad], axis=0))

    for h in range(GLA_HEADS):
        hs = slice(h * GLA_DK, (h + 1) * GLA_DK)
        qt = cols(gq_ref[:, hs])
        kt = cols(gk_ref[:, hs])
        at = cols(jnp.exp(lg_ref[:, hs]))
        for bi in range(nb):
            s = s0_ref[bi, h]
            for t in range(t_len):
                r = bi * t_len + t
                v = gv_ref[r:r + 1, hs]
                s = s * at[:, r:r + 1] + kt[:, r:r + 1] * v
                oacc_ref[r:r + 1, hs] = jnp.sum(s * qt[:, r:r + 1], axis=0, keepdims=True)
            s1_ref[bi, h] = s
    go = go_ref[...]
    for h in range(GLA_HEADS):
        hs = slice(h * GLA_DK, (h + 1) * GLA_DK)
        gr = gr_ref[:, hs]
        o_ref[:, hs] = (_rms(oacc_ref[:, hs], go) * (gr * _sigmoid(gr))).astype(o_ref.dtype)


def _gla_sample(gq, gk, gv, lg, gr, g_o, s0, *, t_len, nb):
    n = gq.shape[0]
    batch = n // t_len
    hk = GLA_HEADS * GLA_DK
    rows = nb * t_len
    row = pl.BlockSpec((rows, hk), lambda i: (i, 0))
    st = pl.BlockSpec((nb, GLA_HEADS, GLA_DK, GLA_DV), lambda i: (i, 0, 0, 0))
    return pl.pallas_call(
        functools.partial(_gla_sample_kernel, nb=nb, t_len=t_len),
        grid=(batch // nb,),
        in_specs=[row, row, row, row, row, _vmem_spec(), st],
        out_specs=(row, st),
        out_shape=(jax.ShapeDtypeStruct((n, hk), BF16), jax.ShapeDtypeStruct(s0.shape, s0.dtype)),
        scratch_shapes=[pltpu.VMEM((rows, hk), F32)],
        compiler_params=_params("parallel"),
        name="gla_sample",
    )(gq, gk, gv, lg, gr, g_o[None, :], s0)


def _sample_attn_kernel(pt_ref, q_ref, cnew_ref, pnew_ref, kc_ref, ks_ref, kcn_ref, ksn_ref, gk_ref, gpe_ref,
                        wukt_ref, wukp_ref, wuv_ref, ckv_hbm, kpe_hbm, o_ref, ckv_buf, kpe_buf, sem,
                        *, n_pages, chunk_pages, t_len):
    n_cols = t_len * MLA_HEADS
    chunk = chunk_pages * PAGE_SIZE
    n_chunks = n_pages // chunk_pages
    past = n_pages * PAGE_SIZE
    b = pl.program_id(0)
    slot = b % 2

    def page_copies(batch, sl, i):
        page = pt_ref[batch, i]
        ci = i // chunk_pages
        return (pltpu.make_async_copy(ckv_hbm.at[page], ckv_buf.at[sl, i], sem.at[sl, ci, 0]),
                pltpu.make_async_copy(kpe_hbm.at[page], kpe_buf.at[sl, i], sem.at[sl, ci, 1]))

    def start_pages(batch, sl, pages):
        for i in pages:
            for cp in page_copies(batch, sl, i):
                cp.start()

    def wait_pages(pages):
        for i in pages:
            for cp in page_copies(b, slot, i):
                cp.wait()

    @pl.when(b == 0)
    def _():
        start_pages(0, 0, range(n_pages))

    qg = q_ref[...].astype(F32) * gk_ref[...]
    gpe = gpe_ref[...]
    sub = lax.broadcasted_iota(jnp.int32, (SUBLANES, MLA_HEADS * HEAD_PAD), 0)
    lane_head = lax.broadcasted_iota(jnp.int32, (SUBLANES, MLA_HEADS * HEAD_PAD), 1) // HEAD_PAD
    qexp = jnp.concatenate(
        [jnp.where(lane_head == sub, jnp.broadcast_to(qg[t:t + 1, :], sub.shape), 0.0) for t in range(t_len)],
        axis=0)
    qt = _bdot(qexp, wukt_ref[...])
    lhs = jnp.concatenate([wukp_ref[...], qt[:, :KV_LORA].astype(BF16)], axis=0)
    qr = qt[:, KV_LORA:KV_LORA + ROPE_DIM].astype(BF16)
    n_norm = NOPE_DIM * MLA_HEADS
    half = ROPE_DIM // 2

    def scores(c, kt, kcos, ksin):
        n = c.shape[0]
        cb = c.astype(BF16)
        r = lax.dot_general(lhs, cb, (((1,), (1,)), ((), ())), preferred_element_type=F32)
        kn = r[:n_norm].reshape(NOPE_DIM, MLA_HEADS, n)
        ss = jnp.sum(kn * kn, axis=0) + jnp.sum(kt * kt, axis=0, keepdims=True)
        rs = lax.rsqrt(ss * (1.0 / QK_DIM) + EPS)
        kg = kt * gpe
        kr = kg * kcos + jnp.concatenate([kg[half:], kg[:half]], axis=0) * ksin
        s = r[n_norm:] + jnp.dot(qr, kr.astype(BF16), preferred_element_type=F32)
        s = (s.reshape(t_len, MLA_HEADS, n) * rs[None]).reshape(n_cols, n)
        return cb, s

    def fold(state, cb, s):
        m, l, acc = state
        m_new = jnp.maximum(m, jnp.max(s, axis=1, keepdims=True))
        alpha = jnp.exp2(m - m_new)
        p = jnp.exp2(s - m_new)
        l = alpha * l + jnp.sum(p, axis=1, keepdims=True)
        acc = alpha * acc + jnp.dot(p.astype(BF16), cb, preferred_element_type=F32)
        return m_new, l, acc

    state = (jnp.full((n_cols, 1), NEG_INF, F32), jnp.zeros((n_cols, 1), F32), jnp.zeros((n_cols, KV_LORA), F32))
    @pl.when(b + 1 < pl.num_programs(0))
    def _():
        start_pages(b + 1, 1 - slot, range(n_pages))

    wait_pages(range(n_pages))
    for ci in range(n_chunks):
        pages = range(ci * chunk_pages, (ci + 1) * chunk_pages)
        c = ckv_buf[slot, pl.ds(ci * chunk_pages, chunk_pages)].reshape(chunk, KV_LORA)
        kt = jnp.concatenate([kpe_buf[slot, p] for p in pages], axis=1)
        keys = slice(ci * chunk, (ci + 1) * chunk)
        state = fold(state, *scores(c, kt, kc_ref[:, keys], ks_ref[:, keys]))

    zpad = lambda a: jnp.concatenate([a, jnp.zeros((PAGE_SIZE - t_len, a.shape[1]), a.dtype)], axis=0)
    pnew = jnp.concatenate([zpad(pnew_ref[...]), jnp.zeros((PAGE_SIZE, LANES - ROPE_DIM), F32)], axis=1)
    cb, s = scores(zpad(cnew_ref[...]), jnp.transpose(pnew)[:ROPE_DIM], kcn_ref[...], ksn_ref[...])
    r2 = lax.broadcasted_iota(jnp.int32, (n_cols, PAGE_SIZE), 0)
    c2 = lax.broadcasted_iota(jnp.int32, (n_cols, PAGE_SIZE), 1)
    s = jnp.where((c2 < t_len) & (c2 <= r2 // MLA_HEADS), s, NEG_INF)
    _, l, acc = fold(state, cb, s)

    lat = acc / l
    res = _bdot(lat, wuv_ref[...])
    r3 = lax.broadcasted_iota(jnp.int32, res.shape, 0)
    c3 = lax.broadcasted_iota(jnp.int32, res.shape, 1)
    res = jnp.where(c3 // V_DIM == r3 % MLA_HEADS, res, 0.0)
    o_ref[...] = jnp.sum(res.reshape(t_len, MLA_HEADS, MLA_HEADS * V_DIM), axis=1).astype(o_ref.dtype)


def _sample_attn(q, ckv_new, kpe_new, cache_ckv, cache_kpe, page_table, kc, ks, g_kh, w_uk, w_uv,
                 *, t_len, chunk_pages):
    batch, n_pages = page_table.shape
    past = n_pages * PAGE_SIZE
    w_ukt = jnp.pad(jnp.transpose(w_uk, (1, 2, 0)), ((0, 0), (0, HEAD_PAD - NOPE_DIM), (0, 0)))
    w_ukt = w_ukt.reshape(MLA_HEADS * HEAD_PAD, KV_LORA)
    sel = jnp.pad(jnp.eye(ROPE_DIM, dtype=F32), ((NOPE_DIM, HEAD_PAD - QK_DIM), (0, LANES - ROPE_DIM)))
    w_ukt_ext = jnp.concatenate([w_ukt, jnp.tile(sel, (MLA_HEADS, 1))], axis=1).astype(BF16)
    w_uk_perm = jnp.transpose(w_uk, (2, 1, 0)).reshape(NOPE_DIM * MLA_HEADS, KV_LORA).astype(BF16)
    g_k_pad = jnp.concatenate([g_kh[:NOPE_DIM], jnp.ones((ROPE_DIM,), F32), jnp.zeros((HEAD_PAD - QK_DIM,), F32)])
    g_k_pad = jnp.tile(g_k_pad, MLA_HEADS)[None, :]
    g_pe = g_kh[NOPE_DIM:][:, None]
    kc, ks = jnp.transpose(kc), jnp.transpose(ks)
    kcn = jnp.pad(kc[:, past:past + t_len], ((0, 0), (0, PAGE_SIZE - t_len)))
    ksn = jnp.pad(ks[:, past:past + t_len], ((0, 0), (0, PAGE_SIZE - t_len)))
    per_b = lambda w: pl.BlockSpec((None, t_len, w), lambda b, pt: (b, 0, 0))
    hbm = pl.BlockSpec(memory_space=pl.ANY)
    in_specs = ([per_b(MLA_HEADS * HEAD_PAD), per_b(KV_LORA), per_b(ROPE_DIM)] + [_vmem_spec()] * 9 + [hbm, hbm])
    grid_spec = pltpu.PrefetchScalarGridSpec(
        num_scalar_prefetch=1,
        grid=(batch,),
        in_specs=in_specs,
        out_specs=pl.BlockSpec((None, t_len, MLA_HEADS * V_DIM), lambda b, pt: (b, 0, 0)),
        scratch_shapes=[pltpu.VMEM((2, n_pages, PAGE_SIZE, KV_LORA), F32),
                        pltpu.VMEM((2, n_pages, ROPE_DIM, PAGE_SIZE), F32),
                        pltpu.SemaphoreType.DMA((2, n_pages // chunk_pages, 2))],
    )
    out = pl.pallas_call(
        functools.partial(_sample_attn_kernel, n_pages=n_pages, chunk_pages=chunk_pages, t_len=t_len),
        grid_spec=grid_spec,
        out_shape=jax.ShapeDtypeStruct((batch, t_len, MLA_HEADS * V_DIM), BF16),
        compiler_params=_params("arbitrary"),
        name="mla_sample_attn",
    )(page_table, q.reshape(batch, t_len, -1), ckv_new.reshape(batch, t_len, -1),
      kpe_new.reshape(batch, t_len, -1), kc[:, :past], ks[:, :past], kcn, ksn, g_k_pad, g_pe,
      w_ukt_ext, w_uk_perm, w_uv.astype(BF16), cache_ckv, jnp.swapaxes(cache_kpe, 1, 2))
    return out.reshape(batch * t_len, MLA_HEADS * V_DIM)


def _mlp_ple(h, p_ref, gm_ref, wup_ref, wdown_ref, gp_ref, wgate_ref, wproj_ref, o_ref, ff_chunk):
    nb = _rms(h, gm_ref[...]).astype(BF16)
    y = jnp.zeros_like(h)
    for c in range(D_FF // ff_chunk):
        cs = slice(c * ff_chunk, (c + 1) * ff_chunk)
        a = jnp.maximum(jnp.dot(nb, wup_ref[:, cs], preferred_element_type=F32), 0.0)
        y = y + jnp.dot((a * a).astype(BF16), wdown_ref[cs, :], preferred_element_type=F32)
    h = h + y
    gate = _sigmoid(_bdot(_rms(h, gp_ref[...]), wgate_ref[...]))
    o_ref[...] = h + gate * _bdot(p_ref[...], wproj_ref[...])


def _post_mix_kernel(h_ref, a_ref, o2_ref, p_ref, wout_ref, gm_ref, wup_ref, wdown_ref, gp_ref, wgate_ref,
                     wproj_ref, o_ref, *, ff_chunk):
    n_attn = MLA_HEADS * V_DIM
    h = (h_ref[...] + jnp.dot(a_ref[...], wout_ref[:n_attn, :], preferred_element_type=F32)
         + jnp.dot(o2_ref[...], wout_ref[n_attn:, :], preferred_element_type=F32))
    _mlp_ple(h, p_ref, gm_ref, wup_ref, wdown_ref, gp_ref, wgate_ref, wproj_ref, o_ref, ff_chunk)


def _post_kernel(h_ref, p_ref, gm_ref, wup_ref, wdown_ref, gp_ref, wgate_ref, wproj_ref, o_ref, *, ff_chunk):
    _mlp_ple(h_ref[...], p_ref, gm_ref, wup_ref, wdown_ref, gp_ref, wgate_ref, wproj_ref, o_ref, ff_chunk)


def _post(h, p, weights, mix=None, *, tm, ff_chunk=1024):
    n = h.shape[0]
    row = lambda w: pl.BlockSpec((tm, w), lambda i: (i, 0))
    if mix is None:
        body, acts, specs = _post_kernel, (h, p), [row(D_MODEL), row(PLE_DIM)]
    else:
        attn, o_gla, w_out = mix
        body, acts = _post_mix_kernel, (h, attn, o_gla, p)
        specs = [row(D_MODEL), row(attn.shape[1]), row(o_gla.shape[1]), row(PLE_DIM)]
        weights = (w_out,) + tuple(weights)
    return pl.pallas_call(
        functools.partial(body, ff_chunk=ff_chunk),
        grid=(n // tm,),
        in_specs=specs + [_vmem_spec()] * len(weights),
        out_specs=row(D_MODEL),
        out_shape=jax.ShapeDtypeStruct((n, D_MODEL), F32),
        compiler_params=_params("parallel"),
        name="post_mix" if mix is not None else "post",
    )(*acts, *weights)


def _s5_prep_kernel(are_ref, aim_ref, ldt_ref, bre_ref, bim_ref, lr_ref, li_ref, bbr_ref, bbi_ref):
    a_re = are_ref[...]
    a_im = aim_ref[...]
    dt = jnp.exp(ldt_ref[...])
    mag = jnp.exp(a_re * dt)
    lr = mag * jnp.cos(a_im * dt)
    li = mag * jnp.sin(a_im * dt)
    den = a_re * a_re + a_im * a_im
    cr = ((lr - 1.0) * a_re + li * a_im) / den
    ci = (li * a_re - (lr - 1.0) * a_im) / den
    lr_ref[...] = lr
    li_ref[...] = li
    b_re = bre_ref[...]
    b_im = bim_ref[...]
    bbr_ref[...] = cr[:, None, :] * b_re - ci[:, None, :] * b_im
    bbi_ref[...] = cr[:, None, :] * b_im + ci[:, None, :] * b_re


def _s5_weights(a_re, a_im, log_dt, b_re, b_im, c_re, c_im):
    g, p_, ch = S5_GROUPS, S5_STATE, S5_GROUP
    gp = jax.ShapeDtypeStruct((g, p_), F32)
    gcp = jax.ShapeDtypeStruct((g, ch, p_), F32)
    lr, li, bbr, bbi = pl.pallas_call(
        _s5_prep_kernel, out_shape=(gp, gp, gcp, gcp), name="s5_prep",
    )(a_re, a_im, log_dt[:, None], jnp.transpose(b_re, (0, 2, 1)), jnp.transpose(b_im, (0, 2, 1)))
    per_slab = LANES // ch
    eye = jnp.eye(per_slab, dtype=F32)

    def b_blocks(bb):
        bb = bb.reshape(S5_SLABS, per_slab, ch, p_)
        return jnp.einsum('jgcp,gh->jgchp', bb, eye).reshape(S5_SLABS, LANES, S5_SLAB_STATE)

    def c_blocks(cc):
        cc = cc.reshape(S5_SLABS, per_slab, ch, p_)
        return jnp.einsum('jgcp,gh->jgphc', cc, eye).reshape(S5_SLABS, S5_SLAB_STATE, LANES)

    w_b = jnp.concatenate([b_blocks(bbr), b_blocks(bbi)], axis=2).astype(BF16)
    w_c = jnp.concatenate([c_blocks(c_re), -c_blocks(c_im)], axis=1).astype(BF16)
    lam = jnp.stack([lr.reshape(S5_SLABS, S5_SLAB_STATE), li.reshape(S5_SLABS, S5_SLAB_STATE)], axis=1)
    return w_b, w_c, lam


def _s5_kernel(x_ref, x0r_ref, x0i_ref, gn_ref, wb_ref, wc_ref, lam_ref, d_ref, wglu_ref,
               o_ref, xr_ref, xi_ref, bur_ref, bui_ref, sr_ref, si_ref, shuf_ref=None, *, nb, t_len):
    step = pl.program_id(0)
    ns = S5_SLAB_STATE

    @pl.when(step == 0)
    def _():
        for j in range(S5_SLABS):
            sr_ref[j] = x0r_ref[:, j * ns:(j + 1) * ns]
            si_ref[j] = x0i_ref[:, j * ns:(j + 1) * ns]

    x = x_ref[...]
    u = _rms(x, gn_ref[...])
    ub = u.astype(BF16)
    for j in range(S5_SLABS):
        bu = jnp.dot(ub[:, j * LANES:(j + 1) * LANES], wb_ref[j], preferred_element_type=F32)
        bur_ref[j] = bu[:, :ns]
        bui_ref[j] = bu[:, ns:]

    n_groups = nb // SUBLANES

    def slab(j, carry):
        lr = jnp.broadcast_to(lam_ref[j, 0:1, :], (SUBLANES, ns))
        li = jnp.broadcast_to(lam_ref[j, 1:2, :], (SUBLANES, ns))

        def group(gi, carry):
            srow = pl.ds(pl.multiple_of(gi * SUBLANES, SUBLANES), SUBLANES)
            xr0 = sr_ref[j, srow, :]
            xi0 = si_ref[j, srow, :]

            def tok(t, st):
                xr, xi = st
                rows = pl.ds(pl.multiple_of(t * nb + gi * SUBLANES, SUBLANES), SUBLANES)
                nr = lr * xr - li * xi + bur_ref[j, rows, :]
                ni = lr * xi + li * xr + bui_ref[j, rows, :]
                bur_ref[j, rows, :] = nr
                bui_ref[j, rows, :] = ni
                return nr, ni

            xr, xi = lax.fori_loop(0, t_len, tok, (xr0, xi0))
            sr_ref[j, srow, :] = xr
            si_ref[j, srow, :] = xi
            return carry

        return lax.fori_loop(0, n_groups, group, carry)

    lax.fori_loop(0, S5_SLABS, slab, 0)

    ys = []
    for j in range(S5_SLABS):
        xs = jnp.concatenate([bur_ref[j].astype(BF16), bui_ref[j].astype(BF16)], axis=1)
        ys.append(jnp.dot(xs, wc_ref[j], preferred_element_type=F32))
    y = jnp.concatenate(ys, axis=1) + d_ref[...] * u
    gl = jax.nn.gelu(y)
    zg = _bdot(gl, wglu_ref[...])
    h = x + zg[:, :D_MODEL] * _sigmoid(zg[:, D_MODEL:])
    if shuf_ref is None:
        o_ref[...] = h
    else:
        for s in range(D_MODEL // LANES):
            shuf_ref[s] = h[:, s * LANES:(s + 1) * LANES]
        for bi in range(nb):
            for s in range(D_MODEL // LANES):
                o_ref[bi, :, s * LANES:(s + 1) * LANES] = shuf_ref[s, pl.ds(bi, t_len, stride=nb), :]

    @pl.when(step == pl.num_programs(0) - 1)
    def _():
        for j in range(S5_SLABS):
            xr_ref[:, j * ns:(j + 1) * ns] = sr_ref[j]
            xi_ref[:, j * ns:(j + 1) * ns] = si_ref[j]


def _s5(x_tm, x0r, x0i, g_norm, w_b, w_c, lam, d, w_glu, *, nb, t_tile, batch_major_out):
    n = x_tm.shape[0]
    rows = nb * t_tile
    n_state = S5_GROUPS * S5_STATE
    full = lambda shape: pl.BlockSpec(shape, lambda i: (0,) * len(shape))
    st = jax.ShapeDtypeStruct((nb, n_state), F32)
    scratch = [pltpu.VMEM((S5_SLABS, rows, S5_SLAB_STATE), F32),
               pltpu.VMEM((S5_SLABS, rows, S5_SLAB_STATE), F32),
               pltpu.VMEM((S5_SLABS, nb, S5_SLAB_STATE), F32),
               pltpu.VMEM((S5_SLABS, nb, S5_SLAB_STATE), F32)]
    if batch_major_out:
        out_spec = pl.BlockSpec((nb, t_tile, D_MODEL), lambda i: (0, i, 0))
        out_sds = jax.ShapeDtypeStruct((nb, n // nb, D_MODEL), F32)
        scratch.append(pltpu.VMEM((D_MODEL // LANES, rows, LANES), F32))
    else:
        out_spec = pl.BlockSpec((rows, D_MODEL), lambda i: (i, 0))
        out_sds = jax.ShapeDtypeStruct((n, D_MODEL), F32)
    return pl.pallas_call(
        functools.partial(_s5_kernel, nb=nb, t_len=t_tile),
        grid=(n // rows,),
        in_specs=[pl.BlockSpec((rows, D_MODEL), lambda i: (i, 0)), full((nb, n_state)), full((nb, n_state))]
        + [_vmem_spec()] * 6,
        out_specs=(out_spec, full((nb, n_state)), full((nb, n_state))),
        out_shape=(out_sds, st, st),
        scratch_shapes=scratch,
        compiler_params=_params("arbitrary"),
        name="s5_layer",
    )(x_tm, x0r, x0i, g_norm[None, :], w_b, w_c, lam, d[None, :], w_glu.astype(BF16))


def _to_time_major(h, batch, seq):
    return h.reshape(batch, seq, -1).transpose(1, 0, 2).reshape(batch * seq, -1)


def _to_batch_major(h, batch, seq):
    return h.reshape(seq, batch, -1).transpose(1, 0, 2).reshape(batch * seq, -1)


def kernel(x_prompt, x_sample, cache_ckv, cache_kpe, state_gla, state_s5_re, state_s5_im, page_table,
           p_prompt, p_sample, g_norm_ab, w_in_ab, g_qlat, w_uq, g_kvlat, w_ukv, g_qh, g_kh,
           w_gla_a2, b_gla_a, g_gla_o, w_out_ab, g_norm_c, s5_a_re, s5_a_im, s5_log_dt,
           s5_b_re, s5_b_im, s5_c_re, s5_c_im, s5_d, w_glu, g_norm_mlp, w_up, w_down,
           g_norm_ple, w_ple_gate, w_ple_proj):
    bp, sp, _ = x_prompt.shape
    bs, ts_len, _ = x_sample.shape
    n_pages = page_table.shape[1]
    past = n_pages * PAGE_SIZE
    n_p, n_s = bp * sp, bs * ts_len
    bf = lambda a: a.astype(BF16)

    cos16, sin16 = _rope_tables(past + ts_len)
    ones = jnp.ones((cos16.shape[0], 1), F32)
    tc = jnp.concatenate([ones * jnp.ones((1, NOPE_DIM), F32), cos16, cos16,
                          ones * jnp.ones((1, HEAD_PAD - QK_DIM), F32)], axis=1)
    ts = jnp.concatenate([ones * jnp.zeros((1, NOPE_DIM), F32), -sin16, sin16,
                          ones * jnp.zeros((1, HEAD_PAD - QK_DIM), F32)], axis=1)
    kc = jnp.concatenate([cos16, cos16], axis=1)
    ks = jnp.concatenate([-sin16, sin16], axis=1)
    tc_s = jnp.tile(tc[past:past + ts_len], (bs, 1))
    ts_s = jnp.tile(ts[past:past + ts_len], (bs, 1))

    post_w = lambda i: (g_norm_mlp[i][None, :], bf(w_up[i]), bf(w_down[i]), g_norm_ple[i][None, :],
                        bf(w_ple_gate[i]), bf(w_ple_proj[i]))

    fw, w_uk, w_uv = _front_weights(w_in_ab[0], g_norm_ab[0], g_qlat[0], w_uq[0], g_kvlat[0], w_ukv[0],
                                    g_qh[0], g_kh[0], w_gla_a2[0], b_gla_a[0])
    tm_p = 512
    (q_p, k_p, v_p, ckv_p, kpe_p, gq_p, gk_p, gv_p, lg_p, gr_p) = _front(
        x_prompt.reshape(n_p, D_MODEL), tc, ts, fw, tm=tm_p, table_tiles=sp // tm_p)
    (q_s, _, _, ckv_s, kpe_s, gq_s, gk_s, gv_s, lg_s, gr_s) = _front(
        x_sample.reshape(n_s, D_MODEL), tc_s, ts_s, fw, tm=n_s // 2, table_tiles=2)
    kpe_p = kpe_p[:, NOPE_DIM:QK_DIM]
    kpe_s = kpe_s[:, NOPE_DIM:QK_DIM]

    attn_p = _prompt_attn(q_p, k_p, v_p, batch=bp, seq=sp, tq=256)
    o_p, gla_p = _gla_prompt(gq_p, gk_p, gv_p, lg_p, gr_p, g_gla_o[0], batch=bp, seq=sp, tg=256)
    gla_p = jnp.swapaxes(gla_p, -1, -2)

    attn_s = _sample_attn(q_s, ckv_s, kpe_s, cache_ckv[0], cache_kpe[0], page_table, kc, ks, g_kh[0],
                          w_uk, w_uv, t_len=ts_len, chunk_pages=8)
    o_s, gla_s = _gla_sample(gq_s, gk_s, gv_s, lg_s, gr_s, g_gla_o[0], state_gla[0], t_len=ts_len, nb=8)

    w_out = bf(w_out_ab[0])
    hp = _post(x_prompt.reshape(n_p, D_MODEL), p_prompt[0].reshape(n_p, PLE_DIM), post_w(0),
               mix=(attn_p, o_p, w_out), tm=512)
    hs = _post(x_sample.reshape(n_s, D_MODEL), p_sample[0].reshape(n_s, PLE_DIM), post_w(0),
               mix=(attn_s, o_s, w_out), tm=n_s // 2)

    w_b, w_c, lam = _s5_weights(s5_a_re[0], s5_a_im[0], s5_log_dt[0], s5_b_re[0], s5_b_im[0],
                                s5_c_re[0], s5_c_im[0])
    n_state = S5_GROUPS * S5_STATE
    zero_state = jnp.zeros((bp, n_state), F32)
    hp, s5r_p, s5i_p = _s5(_to_time_major(hp, bp, sp), zero_state, zero_state, g_norm_c[0], w_b, w_c, lam,
                           s5_d[0], w_glu[0], nb=bp, t_tile=64, batch_major_out=True)
    hs_tm, s5r_s, s5i_s = _s5(_to_time_major(hs, bs, ts_len), state_s5_re[0].reshape(bs, n_state),
                              state_s5_im[0].reshape(bs, n_state), g_norm_c[0], w_b, w_c, lam,
                              s5_d[0], w_glu[0], nb=bs, t_tile=ts_len, batch_major_out=False)
    hp = _post(hp.reshape(n_p, D_MODEL), p_prompt[1].reshape(n_p, PLE_DIM), post_w(1), tm=512)
    hs = _post(_to_batch_major(hs_tm, bs, ts_len), p_sample[1].reshape(n_s, PLE_DIM), post_w(1), tm=n_s // 2)

    st = lambda a, b: a.reshape(1, b, S5_GROUPS, S5_STATE)
    return (hp.reshape(bp, sp, D_MODEL), hs.reshape(bs, ts_len, D_MODEL),
            ckv_p.reshape(1, bp, sp, KV_LORA), kpe_p.reshape(1, bp, sp, ROPE_DIM), gla_p[None],
            st(s5r_p, bp), st(s5i_p, bp),
            ckv_s.reshape(1, bs, ts_len, KV_LORA), kpe_s.reshape(1, bs, ts_len, ROPE_DIM), gla_s[None],
            st(s5r_s, bs), st(s5i_s, bs))
```

```python
import functools

import jax
import jax.numpy as jnp
from jax import lax
from jax.experimental import pallas as pl
from jax.experimental.pallas import tpu as pltpu

F32 = jnp.float32
BF16 = jnp.bfloat16

D_MODEL = 1024
PAGE_SIZE = 128
MLA_HEADS = 8
Q_LORA = 384
KV_LORA = 256
NOPE_DIM = 64
ROPE_DIM = 32
QK_DIM = NOPE_DIM + ROPE_DIM
V_DIM = 64
ROPE_THETA = 10000.0
GLA_HEADS = 4
GLA_DK = 128
GLA_DV = 128
GLA_GATE_RANK = 16
GLA_TAU = 16.0
GLA_CHUNK = 64
S5_GROUP = 16
S5_GROUPS = D_MODEL // S5_GROUP
S5_STATE = 64
D_FF = 4 * D_MODEL
PLE_DIM = 256
EPS = 1e-6
IN_SPLITS = (Q_LORA, KV_LORA, ROPE_DIM, GLA_HEADS * GLA_DK, GLA_HEADS * GLA_DK,
             GLA_HEADS * GLA_DV, GLA_GATE_RANK, GLA_HEADS * GLA_DV)

LANES = 128
SUBLANES = 8
HEAD_PAD = LANES
VMEM_LIMIT = 56 * 1024 * 1024
S5_SLABS = D_MODEL // LANES
S5_SLAB_STATE = (LANES // S5_GROUP) * S5_STATE
NEG_INF = float("-inf")
LOG2_E = 1.4426950408889634


def _vmem_spec():
    return pl.BlockSpec(memory_space=pltpu.VMEM)


def _params(*sem):
    return pltpu.CompilerParams(dimension_semantics=sem, vmem_limit_bytes=VMEM_LIMIT)


def _bdot(a, b):
    return jnp.dot(a.astype(BF16), b.astype(BF16), preferred_element_type=F32)


def _bdot_nt(a, b):
    return lax.dot_general(a.astype(BF16), b.astype(BF16), (((1,), (1,)), ((), ())),
                           preferred_element_type=F32)


def _bdot_tn(a, b):
    return lax.dot_general(a.astype(BF16), b.astype(BF16), (((0,), (0,)), ((), ())),
                           preferred_element_type=F32)


def _rms(x, g):
    return x * lax.rsqrt(jnp.mean(x * x, axis=-1, keepdims=True) + EPS) * g


def _sigmoid(x):
    return 1.0 / (1.0 + jnp.exp(-x))


def _rope_table_kernel(cos_ref, sin_ref):
    shape = cos_ref.shape
    half = ROPE_DIM // 2
    per_row = LANES // half
    row = lax.broadcasted_iota(jnp.int32, shape, 0)
    lane = lax.broadcasted_iota(jnp.int32, shape, 1)
    pos = (row * per_row + lane // half).astype(F32)
    freq = jnp.power(F32(ROPE_THETA), -(lane % half).astype(F32) / half)
    ang = pos * freq
    cos_ref[...] = jnp.cos(ang)
    sin_ref[...] = jnp.sin(ang)


def _rope_tables(n_pos):
    half = ROPE_DIM // 2
    per_row = LANES // half
    rows = -(-n_pos // per_row)
    rows = -(-rows // SUBLANES) * SUBLANES
    cos, sin = pl.pallas_call(
        _rope_table_kernel,
        out_shape=(jax.ShapeDtypeStruct((rows, LANES), F32),) * 2,
        name="rope_table",
    )()
    cos = cos.reshape(rows * per_row, half)[:n_pos]
    sin = sin.reshape(rows * per_row, half)[:n_pos]
    return cos, sin


def _front_kernel(x_ref, tc_ref, ts_ref, gn_ref, wcq_ref, wckv_ref, wkpe_ref, wgq_ref, wgk_ref, wgv_ref,
                  wgr_ref, wga_ref, gql_ref, wuq_ref, gkvl_ref, wuk_ref, wuv_ref, gqh_ref, gkh_ref,
                  wa2_ref, ba_ref,
                  q_ref, k_ref, v_ref, ckv_ref, kpe_ref, gq_ref, gk_ref, gv_ref, lg_ref, gr_ref):
    n = _rms(x_ref[...], gn_ref[...])
    nb = n.astype(BF16)
    tc = tc_ref[...]
    ts = ts_ref[...]
    lane = lax.broadcasted_iota(jnp.int32, tc.shape, 1)
    first_half = lane < NOPE_DIM + ROPE_DIM // 2

    def norm_rope(xh, g):
        ss = jnp.sum(xh * xh, axis=-1, keepdims=True) * (1.0 / QK_DIM)
        y = xh * lax.rsqrt(ss + EPS) * g
        partner = jnp.where(first_half, pltpu.roll(y, LANES - ROPE_DIM // 2, 1), pltpu.roll(y, ROPE_DIM // 2, 1))
        return y * tc + partner * ts

    cq = _rms(jnp.dot(nb, wcq_ref[...], preferred_element_type=F32), gql_ref[...])
    q = _bdot(cq, wuq_ref[...])
    ckv = _rms(jnp.dot(nb, wckv_ref[...], preferred_element_type=F32), gkvl_ref[...])
    ckv_ref[...] = ckv
    ckvb = ckv.astype(BF16)
    kn = jnp.dot(ckvb, wuk_ref[...], preferred_element_type=F32)
    v_ref[...] = jnp.dot(ckvb, wuv_ref[...], preferred_element_type=F32).astype(v_ref.dtype)
    kpe = jnp.dot(nb, wkpe_ref[...], preferred_element_type=F32)
    kpe_ref[...] = kpe
    gqh = gqh_ref[...]
    gkh = gkh_ref[...]
    scale = QK_DIM ** -0.5 * LOG2_E
    for h in range(MLA_HEADS):
        hs = slice(h * HEAD_PAD, (h + 1) * HEAD_PAD)
        q_ref[:, hs] = (norm_rope(q[:, hs], gqh) * scale).astype(q_ref.dtype)
        k_ref[:, hs] = norm_rope(kn[:, hs] + kpe, gkh).astype(k_ref.dtype)

    gq_ref[...] = jnp.dot(nb, wgq_ref[...], preferred_element_type=F32) * (GLA_DK ** -0.5)
    gk_ref[...] = jnp.dot(nb, wgk_ref[...], preferred_element_type=F32)
    gv_ref[...] = jnp.dot(nb, wgv_ref[...], preferred_element_type=F32)
    gr_ref[...] = jnp.dot(nb, wgr_ref[...], preferred_element_type=F32)
    ga = jnp.dot(nb, wga_ref[...], preferred_element_type=F32)
    logit = _bdot(ga, wa2_ref[...]) + ba_ref[...]
    log_sig = jnp.minimum(logit, 0.0) - jnp.log1p(jnp.exp(-jnp.abs(logit)))
    lg_ref[...] = log_sig * (1.0 / GLA_TAU)


def _front_weights(w_in, g_norm, g_qlat, w_uq, g_kvlat, w_ukv, g_qh, g_kh, w_a2, b_a):
    offs = [0]
    for s in IN_SPLITS:
        offs.append(offs[-1] + s)
    cols = [w_in[:, offs[i]:offs[i + 1]] for i in range(len(IN_SPLITS))]
    w_cq, w_ckv, w_kpe, w_gq, w_gk, w_gv, w_ga, w_gr = cols
    pad_head = HEAD_PAD - QK_DIM
    w_kpe_placed = jnp.pad(w_kpe, ((0, 0), (NOPE_DIM, HEAD_PAD - QK_DIM)))
    w_ga_pad = jnp.pad(w_ga, ((0, 0), (0, LANES - GLA_GATE_RANK)))
    w_a2_pad = jnp.pad(w_a2, ((0, LANES - GLA_GATE_RANK), (0, 0)))
    w_uq_pad = jnp.pad(w_uq.reshape(Q_LORA, MLA_HEADS, QK_DIM), ((0, 0), (0, 0), (0, pad_head)))
    w_uq_pad = w_uq_pad.reshape(Q_LORA, MLA_HEADS * HEAD_PAD)
    w_ukv3 = w_ukv.reshape(KV_LORA, MLA_HEADS, NOPE_DIM + V_DIM)
    w_uk = w_ukv3[:, :, :NOPE_DIM]
    w_uv = w_ukv3[:, :, NOPE_DIM:].reshape(KV_LORA, MLA_HEADS * V_DIM)
    w_uk_pad = jnp.pad(w_uk, ((0, 0), (0, 0), (0, HEAD_PAD - NOPE_DIM))).reshape(KV_LORA, MLA_HEADS * HEAD_PAD)
    g_qh_pad = jnp.pad(g_qh, (0, pad_head))[None, :]
    g_kh_pad = jnp.pad(g_kh, (0, pad_head))[None, :]
    bf = lambda a: a.astype(BF16)
    weights = (g_norm[None, :], bf(w_cq), bf(w_ckv), bf(w_kpe_placed), bf(w_gq), bf(w_gk), bf(w_gv), bf(w_gr),
               bf(w_ga_pad), g_qlat[None, :], bf(w_uq_pad), g_kvlat[None, :], bf(w_uk_pad), bf(w_uv),
               g_qh_pad, g_kh_pad, bf(w_a2_pad), b_a[None, :])
    return weights, w_uk, w_uv


def _front(x, tc, ts, weights, *, tm, table_tiles):
    n = x.shape[0]
    hk = GLA_HEADS * GLA_DK
    row = lambda w: pl.BlockSpec((tm, w), lambda i: (i, 0))
    tab = pl.BlockSpec((tm, LANES), lambda i: (i % table_tiles, 0))
    sds = lambda w, dt: jax.ShapeDtypeStruct((n, w), dt)
    out_shape = (sds(MLA_HEADS * HEAD_PAD, BF16), sds(MLA_HEADS * HEAD_PAD, BF16), sds(MLA_HEADS * V_DIM, BF16),
                 sds(KV_LORA, F32), sds(LANES, F32), sds(hk, F32), sds(hk, F32), sds(hk, F32), sds(hk, F32),
                 sds(hk, F32))
    out_specs = (row(MLA_HEADS * HEAD_PAD), row(MLA_HEADS * HEAD_PAD), row(MLA_HEADS * V_DIM), row(KV_LORA),
                 row(LANES), row(hk), row(hk), row(hk), row(hk), row(hk))
    return pl.pallas_call(
        _front_kernel,
        grid=(n // tm,),
        in_specs=[row(D_MODEL), tab, tab] + [_vmem_spec()] * len(weights),
        out_specs=out_specs,
        out_shape=out_shape,
        compiler_params=_params("parallel"),
        name="ab_front",
    )(x, tc, ts, *weights)


def _prompt_attn_kernel(q_ref, k_ref, v_ref, o_ref, *, tq, n_q):
    i = pl.program_id(1)
    row = lax.broadcasted_iota(jnp.int32, (tq, tq), 0)
    col = lax.broadcasted_iota(jnp.int32, (tq, tq), 1)
    causal = col <= row
    lane = lax.broadcasted_iota(jnp.int32, (tq, LANES), 1)

    def case(c):
        kv = (c + 1) * tq
        outs = []
        for h in range(MLA_HEADS):
            hs = slice(h * HEAD_PAD, (h + 1) * HEAD_PAD)
            vs = slice((h // 2) * LANES, (h // 2 + 1) * LANES)
            s = lax.dot_general(q_ref[:, hs], k_ref[:kv, hs], (((1,), (1,)), ((), ())),
                                preferred_element_type=F32)
            diag = jnp.where(causal, s[:, c * tq:], NEG_INF)
            s = diag if c == 0 else jnp.concatenate([s[:, :c * tq], diag], axis=1)
            p = jnp.exp2(s - jnp.max(s, axis=-1, keepdims=True))
            l = jnp.sum(p, axis=-1, keepdims=True)
            outs.append(jnp.dot(p.astype(BF16), v_ref[:kv, vs], preferred_element_type=F32) / l)
        for pair in range(MLA_HEADS // 2):
            o_ref[:, pair * LANES:(pair + 1) * LANES] = jnp.where(
                lane < V_DIM, outs[2 * pair], outs[2 * pair + 1]).astype(o_ref.dtype)

    for c in range(n_q):
        pl.when(i == c)(functools.partial(case, c))


def _prompt_attn(q, k, v, *, batch, seq, tq):
    nq = seq // tq
    return pl.pallas_call(
        functools.partial(_prompt_attn_kernel, tq=tq, n_q=nq),
        grid=(batch, nq),
        in_specs=[pl.BlockSpec((tq, MLA_HEADS * HEAD_PAD), lambda b, i: (b * nq + i, 0)),
                  pl.BlockSpec((seq, MLA_HEADS * HEAD_PAD), lambda b, i: (b, 0)),
                  pl.BlockSpec((seq, MLA_HEADS * V_DIM), lambda b, i: (b, 0))],
        out_specs=pl.BlockSpec((tq, MLA_HEADS * V_DIM), lambda b, i: (b * nq + i, 0)),
        out_shape=jax.ShapeDtypeStruct((batch * seq, MLA_HEADS * V_DIM), BF16),
        compiler_params=_params("parallel", "arbitrary"),
        name="mla_prompt_attn",
    )(q, k, v)


def _gla_intra(q, k, b):
    c = q.shape[0]
    row = lax.broadcasted_iota(jnp.int32, q.shape, 0)
    r2 = lax.broadcasted_iota(jnp.int32, (c, c), 0)
    c2 = lax.broadcasted_iota(jnp.int32, (c, c), 1)
    att = jnp.where(r2 == c2, _bdot_nt(q, k), 0.0)
    before = pltpu.roll(b, 1, 0)
    last = b
    m = 1
    while m < c:
        upper = (row // m) % 2 == 1
        qm = jnp.where(upper, q * jnp.exp(b - before), 0.0)
        km = jnp.where(upper, 0.0, k * jnp.exp(last - b))
        same = (r2 // (2 * m)) == (c2 // (2 * m))
        att = att + jnp.where(same, _bdot_nt(qm, km), 0.0)
        if 2 * m < c:
            before = jnp.where(upper, pltpu.roll(before, m, 0), before)
            last = jnp.where(upper, last, pltpu.roll(last, c - m, 0))
        m *= 2
    return att


def _cumsum_rows(g):
    c = g.shape[0]
    row = lax.broadcasted_iota(jnp.int32, g.shape, 0)
    b = g
    sh = 1
    while sh < c:
        b = b + jnp.where(row >= sh, pltpu.roll(b, sh, 0), 0.0)
        sh *= 2
    return b


def _gla_prompt_kernel(gq_ref, gk_ref, gv_ref, lg_ref, gr_ref, go_ref, o_ref, st_ref, state_ref, *, n_chunks):
    t = pl.program_id(1)

    @pl.when(t == 0)
    def _():
        state_ref[...] = jnp.zeros_like(state_ref)

    go = go_ref[...]

    for h in range(GLA_HEADS):
        hs = slice(h * GLA_DK, (h + 1) * GLA_DK)
        st = state_ref[h]
        for ci in range(n_chunks):
            rows = slice(ci * GLA_CHUNK, (ci + 1) * GLA_CHUNK)
            q = gq_ref[rows, hs]
            k = gk_ref[rows, hs]
            v = gv_ref[rows, hs]
            b = _cumsum_rows(lg_ref[rows, hs])
            o = _bdot_nt(q * jnp.exp(b), st) + _bdot(_gla_intra(q, k, b), v)
            bl = b[GLA_CHUNK - 1:GLA_CHUNK, :]
            st = st * jnp.exp(bl) + _bdot_tn(v, k * jnp.exp(bl - b))
            gr = gr_ref[rows, hs]
            o_ref[rows, hs] = (_rms(o, go) * (gr * _sigmoid(gr))).astype(o_ref.dtype)
        state_ref[h] = st

    @pl.when(t == pl.num_programs(1) - 1)
    def _():
        st_ref[0] = state_ref[...]


def _gla_prompt(gq, gk, gv, lg, gr, g_o, *, batch, seq, tg):
    nt = seq // tg
    hk = GLA_HEADS * GLA_DK
    row = pl.BlockSpec((tg, hk), lambda b, t: (b * nt + t, 0))
    return pl.pallas_call(
        functools.partial(_gla_prompt_kernel, n_chunks=tg // GLA_CHUNK),
        grid=(batch, nt),
        in_specs=[row, row, row, row, row, _vmem_spec()],
        out_specs=(row, pl.BlockSpec((1, GLA_HEADS, GLA_DV, GLA_DK), lambda b, t: (b, 0, 0, 0))),
        out_shape=(jax.ShapeDtypeStruct((batch * seq, hk), BF16),
                   jax.ShapeDtypeStruct((batch, GLA_HEADS, GLA_DV, GLA_DK), F32)),
        scratch_shapes=[pltpu.VMEM((GLA_HEADS, GLA_DV, GLA_DK), F32)],
        compiler_params=_params("parallel", "arbitrary"),
        name="gla_prompt",
    )(gq, gk, gv, lg, gr, g_o[None, :])


def _gla_sample_kernel(gq_ref, gk_ref, gv_ref, lg_ref, gr_ref, go_ref, s0_ref, o_ref, s1_ref, *, nb, t_len):
    rows = nb * t_len
    per_tile = SUBLANES // t_len
    tok = lax.broadcasted_iota(jnp.int32, (rows, GLA_DK), 0) % t_len
    tile_batch = lax.broadcasted_iota(jnp.int32, (SUBLANES, GLA_DK), 0) // t_len
    pad = jnp.zeros((LANES - rows, LANES), F32)
    go = go_ref[...]
    for h in range(GLA_HEADS):
        hs = slice(h * GLA_DK, (h + 1) * GLA_DK)
        q = gq_ref[:, hs]
        k = gk_ref[:, hs]
        v = gv_ref[:, hs]
        b = lg_ref[:, hs]
        sh = 1
        while sh < t_len:
            b = b + jnp.where(tok >= sh, pltpu.roll(b, sh, 0), 0.0)
            sh *= 2
        bl = jnp.where(tok == t_len - 1, b, 0.0)
        for d in range(1, t_len):
            bl = bl + jnp.where(tok == t_len - 1 - d, pltpu.roll(b, rows - d, 0), 0.0)
        qe = q * jnp.exp(b)
        kd = k * jnp.exp(bl - b)
        o = jnp.sum(q * k, axis=-1, keepdims=True) * v
        for d in range(1, t_len):
            w = jnp.sum(q * pltpu.roll(k, d, 0) * jnp.exp(b - pltpu.roll(b, d, 0)), axis=-1, keepdims=True)
            o = o + jnp.where(tok[:, :1] >= d, w, 0.0) * pltpu.roll(v, d, 0)
        decay_cols = jnp.transpose(jnp.concatenate([jnp.exp(bl), pad], axis=0))
        inter = []
        for tile in range(rows // SUBLANES):
            ts = slice(tile * SUBLANES, (tile + 1) * SUBLANES)
            acc = jnp.zeros((SUBLANES, GLA_DV), F32)
            for e in range(per_tile):
                bi = tile * per_tile + e
                s0 = s0_ref[bi, h]
                mine = tile_batch == e
                acc = jnp.where(mine, _bdot(qe[ts], s0), acc)
                r_last = bi * t_len + t_len - 1
                s1_ref[bi, h] = s0 * decay_cols[:, r_last:r_last + 1] + _bdot_tn(jnp.where(mine, kd[ts], 0.0), v[ts])
            inter.append(acc)
        o = o + jnp.concatenate(inter, axis=0)
        gr = gr_ref[:, hs]
        o_ref[:, hs] = (_rms(o, go) * (gr * _sigmoid(gr))).astype(o_ref.dtype)


def _gla_sample(gq, gk, gv, lg, gr, g_o, s0, *, t_len, nb):
    n = gq.shape[0]
    batch = n // t_len
    hk = GLA_HEADS * GLA_DK
    rows = nb * t_len
    row = pl.BlockSpec((rows, hk), lambda i: (i, 0))
    st = pl.BlockSpec((nb, GLA_HEADS, GLA_DK, GLA_DV), lambda i: (i, 0, 0, 0))
    return pl.pallas_call(
        functools.partial(_gla_sample_kernel, nb=nb, t_len=t_len),
        grid=(batch // nb,),
        in_specs=[row, row, row, row, row, _vmem_spec(), st],
        out_specs=(row, st),
        out_shape=(jax.ShapeDtypeStruct((n, hk), BF16), jax.ShapeDtypeStruct(s0.shape, s0.dtype)),
        compiler_params=_params("parallel"),
        name="gla_sample",
    )(gq, gk, gv, lg, gr, g_o[None, :], s0)


def _sample_attn_kernel(pt_ref, q_ref, cnew_ref, pnew_ref, kc_ref, ks_ref, kcn_ref, ksn_ref, gk_ref, gpe_ref,
                        wukt_ref, wukp_ref, wuv_ref, ckv_hbm, kpe_hbm, o_ref, ckv_buf, kpe_buf, sem,
                        *, n_pages, chunk_pages, t_len):
    n_cols = t_len * MLA_HEADS
    chunk = chunk_pages * PAGE_SIZE
    n_chunks = n_pages // chunk_pages
    past = n_pages * PAGE_SIZE
    b = pl.program_id(0)
    slot = b % 2

    def page_copies(batch, sl, i):
        page = pt_ref[batch, i]
        ci = i // chunk_pages
        return (pltpu.make_async_copy(ckv_hbm.at[page], ckv_buf.at[sl, i], sem.at[sl, ci, 0]),
                pltpu.make_async_copy(kpe_hbm.at[page], kpe_buf.at[sl, i], sem.at[sl, ci, 1]))

    def start_pages(batch, sl, pages):
        for i in pages:
            for cp in page_copies(batch, sl, i):
                cp.start()

    def wait_pages(pages):
        for i in pages:
            for cp in page_copies(b, slot, i):
                cp.wait()

    @pl.when(b == 0)
    def _():
        start_pages(0, 0, range(n_pages))

    qg = q_ref[...].astype(F32) * gk_ref[...]
    gpe = gpe_ref[...]
    sub = lax.broadcasted_iota(jnp.int32, (SUBLANES, MLA_HEADS * HEAD_PAD), 0)
    lane_head = lax.broadcasted_iota(jnp.int32, (SUBLANES, MLA_HEADS * HEAD_PAD), 1) // HEAD_PAD
    qexp = jnp.concatenate(
        [jnp.where(lane_head == sub, jnp.broadcast_to(qg[t:t + 1, :], sub.shape), 0.0) for t in range(t_len)],
        axis=0)
    qt = _bdot(qexp, wukt_ref[...])
    lhs = jnp.concatenate([wukp_ref[...], qt[:, :KV_LORA].astype(BF16)], axis=0)
    qr = qt[:, KV_LORA:KV_LORA + ROPE_DIM].astype(BF16)
    n_norm = NOPE_DIM * MLA_HEADS
    half = ROPE_DIM // 2

    def scores(c, kt, kcos, ksin):
        n = c.shape[0]
        cb = c.astype(BF16)
        r = lax.dot_general(lhs, cb, (((1,), (1,)), ((), ())), preferred_element_type=F32)
        kn = r[:n_norm].reshape(NOPE_DIM, MLA_HEADS, n)
        ss = jnp.sum(kn * kn, axis=0) + jnp.sum(kt * kt, axis=0, keepdims=True)
        rs = lax.rsqrt(ss * (1.0 / QK_DIM) + EPS)
        kg = kt * gpe
        kr = kg * kcos + jnp.concatenate([kg[half:], kg[:half]], axis=0) * ksin
        s = r[n_norm:] + jnp.dot(qr, kr.astype(BF16), preferred_element_type=F32)
        s = (s.reshape(t_len, MLA_HEADS, n) * rs[None]).reshape(n_cols, n)
        return cb, s

    def fold(state, cb, s):
        m, l, acc = state
        m_new = jnp.maximum(m, jnp.max(s, axis=1, keepdims=True))
        alpha = jnp.exp2(m - m_new)
        p = jnp.exp2(s - m_new)
        l = alpha * l + jnp.sum(p, axis=1, keepdims=True)
        acc = alpha * acc + jnp.dot(p.astype(BF16), cb, preferred_element_type=F32)
        return m_new, l, acc

    state = (jnp.full((n_cols, 1), NEG_INF, F32), jnp.zeros((n_cols, 1), F32), jnp.zeros((n_cols, KV_LORA), F32))
    @pl.when(b + 1 < pl.num_programs(0))
    def _():
        start_pages(b + 1, 1 - slot, range(n_pages))

    wait_pages(range(n_pages))
    for ci in range(n_chunks):
        pages = range(ci * chunk_pages, (ci + 1) * chunk_pages)
        c = ckv_buf[slot, pl.ds(ci * chunk_pages, chunk_pages)].reshape(chunk, KV_LORA)
        kt = jnp.concatenate([kpe_buf[slot, p] for p in pages], axis=1)
        keys = slice(ci * chunk, (ci + 1) * chunk)
        state = fold(state, *scores(c, kt, kc_ref[:, keys], ks_ref[:, keys]))

    zpad = lambda a: jnp.concatenate([a, jnp.zeros((PAGE_SIZE - t_len, a.shape[1]), a.dtype)], axis=0)
    pnew = jnp.concatenate([zpad(pnew_ref[...]), jnp.zeros((PAGE_SIZE, LANES - ROPE_DIM), F32)], axis=1)
    cb, s = scores(zpad(cnew_ref[...]), jnp.transpose(pnew)[:ROPE_DIM], kcn_ref[...], ksn_ref[...])
    r2 = lax.broadcasted_iota(jnp.int32, (n_cols, PAGE_SIZE), 0)
    c2 = lax.broadcasted_iota(jnp.int32, (n_cols, PAGE_SIZE), 1)
    s = jnp.where((c2 < t_len) & (c2 <= r2 // MLA_HEADS), s, NEG_INF)
    _, l, acc = fold(state, cb, s)

    lat = acc / l
    res = _bdot(lat, wuv_ref[...])
    r3 = lax.broadcasted_iota(jnp.int32, res.shape, 0)
    c3 = lax.broadcasted_iota(jnp.int32, res.shape, 1)
    res = jnp.where(c3 // V_DIM == r3 % MLA_HEADS, res, 0.0)
    o_ref[...] = jnp.sum(res.reshape(t_len, MLA_HEADS, MLA_HEADS * V_DIM), axis=1).astype(o_ref.dtype)


def _sample_attn(q, ckv_new, kpe_new, cache_ckv, cache_kpe, page_table, kc, ks, g_kh, w_uk, w_uv,
                 *, t_len, chunk_pages):
    batch, n_pages = page_table.shape
    past = n_pages * PAGE_SIZE
    w_ukt = jnp.pad(jnp.transpose(w_uk, (1, 2, 0)), ((0, 0), (0, HEAD_PAD - NOPE_DIM), (0, 0)))
    w_ukt = w_ukt.reshape(MLA_HEADS * HEAD_PAD, KV_LORA)
    sel = jnp.pad(jnp.eye(ROPE_DIM, dtype=F32), ((NOPE_DIM, HEAD_PAD - QK_DIM), (0, LANES - ROPE_DIM)))
    w_ukt_ext = jnp.concatenate([w_ukt, jnp.tile(sel, (MLA_HEADS, 1))], axis=1).astype(BF16)
    w_uk_perm = jnp.transpose(w_uk, (2, 1, 0)).reshape(NOPE_DIM * MLA_HEADS, KV_LORA).astype(BF16)
    g_k_pad = jnp.concatenate([g_kh[:NOPE_DIM], jnp.ones((ROPE_DIM,), F32), jnp.zeros((HEAD_PAD - QK_DIM,), F32)])
    g_k_pad = jnp.tile(g_k_pad, MLA_HEADS)[None, :]
    g_pe = g_kh[NOPE_DIM:][:, None]
    kc, ks = jnp.transpose(kc), jnp.transpose(ks)
    kcn = jnp.pad(kc[:, past:past + t_len], ((0, 0), (0, PAGE_SIZE - t_len)))
    ksn = jnp.pad(ks[:, past:past + t_len], ((0, 0), (0, PAGE_SIZE - t_len)))
    per_b = lambda w: pl.BlockSpec((None, t_len, w), lambda b, pt: (b, 0, 0))
    hbm = pl.BlockSpec(memory_space=pl.ANY)
    in_specs = ([per_b(MLA_HEADS * HEAD_PAD), per_b(KV_LORA), per_b(ROPE_DIM)] + [_vmem_spec()] * 9 + [hbm, hbm])
    grid_spec = pltpu.PrefetchScalarGridSpec(
        num_scalar_prefetch=1,
        grid=(batch,),
        in_specs=in_specs,
        out_specs=pl.BlockSpec((None, t_len, MLA_HEADS * V_DIM), lambda b, pt: (b, 0, 0)),
        scratch_shapes=[pltpu.VMEM((2, n_pages, PAGE_SIZE, KV_LORA), F32),
                        pltpu.VMEM((2, n_pages, ROPE_DIM, PAGE_SIZE), F32),
                        pltpu.SemaphoreType.DMA((2, n_pages // chunk_pages, 2))],
    )
    out = pl.pallas_call(
        functools.partial(_sample_attn_kernel, n_pages=n_pages, chunk_pages=chunk_pages, t_len=t_len),
        grid_spec=grid_spec,
        out_shape=jax.ShapeDtypeStruct((batch, t_len, MLA_HEADS * V_DIM), BF16),
        compiler_params=_params("arbitrary"),
        name="mla_sample_attn",
    )(page_table, q.reshape(batch, t_len, -1), ckv_new.reshape(batch, t_len, -1),
      kpe_new.reshape(batch, t_len, -1), kc[:, :past], ks[:, :past], kcn, ksn, g_k_pad, g_pe,
      w_ukt_ext, w_uk_perm, w_uv.astype(BF16), cache_ckv, jnp.swapaxes(cache_kpe, 1, 2))
    return out.reshape(batch * t_len, MLA_HEADS * V_DIM)


def _mlp_ple(h, p_ref, gm_ref, wup_ref, wdown_ref, gp_ref, wgate_ref, wproj_ref, o_ref, ff_chunk):
    nb = _rms(h, gm_ref[...]).astype(BF16)
    y = jnp.zeros_like(h)
    for c in range(D_FF // ff_chunk):
        cs = slice(c * ff_chunk, (c + 1) * ff_chunk)
        a = jnp.maximum(jnp.dot(nb, wup_ref[:, cs], preferred_element_type=F32), 0.0)
        y = y + jnp.dot((a * a).astype(BF16), wdown_ref[cs, :], preferred_element_type=F32)
    h = h + y
    gate = _sigmoid(_bdot(_rms(h, gp_ref[...]), wgate_ref[...]))
    o_ref[...] = h + gate * _bdot(p_ref[...], wproj_ref[...])


def _post_mix_kernel(h_ref, a_ref, o2_ref, p_ref, wout_ref, gm_ref, wup_ref, wdown_ref, gp_ref, wgate_ref,
                     wproj_ref, o_ref, *, ff_chunk):
    n_attn = MLA_HEADS * V_DIM
    h = (h_ref[...] + jnp.dot(a_ref[...], wout_ref[:n_attn, :], preferred_element_type=F32)
         + jnp.dot(o2_ref[...], wout_ref[n_attn:, :], preferred_element_type=F32))
    _mlp_ple(h, p_ref, gm_ref, wup_ref, wdown_ref, gp_ref, wgate_ref, wproj_ref, o_ref, ff_chunk)


def _post_kernel(h_ref, p_ref, gm_ref, wup_ref, wdown_ref, gp_ref, wgate_ref, wproj_ref, o_ref, *, ff_chunk):
    _mlp_ple(h_ref[...], p_ref, gm_ref, wup_ref, wdown_ref, gp_ref, wgate_ref, wproj_ref, o_ref, ff_chunk)


def _post(h, p, weights, mix=None, *, tm, ff_chunk=1024):
    n = h.shape[0]
    row = lambda w: pl.BlockSpec((tm, w), lambda i: (i, 0))
    if mix is None:
        body, acts, specs = _post_kernel, (h, p), [row(D_MODEL), row(PLE_DIM)]
    else:
        attn, o_gla, w_out = mix
        body, acts = _post_mix_kernel, (h, attn, o_gla, p)
        specs = [row(D_MODEL), row(attn.shape[1]), row(o_gla.shape[1]), row(PLE_DIM)]
        weights = (w_out,) + tuple(weights)
    return pl.pallas_call(
        functools.partial(body, ff_chunk=ff_chunk),
        grid=(n // tm,),
        in_specs=specs + [_vmem_spec()] * len(weights),
        out_specs=row(D_MODEL),
        out_shape=jax.ShapeDtypeStruct((n, D_MODEL), F32),
        compiler_params=_params("parallel"),
        name="post_mix" if mix is not None else "post",
    )(*acts, *weights)


def _s5_prep_kernel(are_ref, aim_ref, ldt_ref, bre_ref, bim_ref, lr_ref, li_ref, bbr_ref, bbi_ref):
    a_re = are_ref[...]
    a_im = aim_ref[...]
    dt = jnp.exp(ldt_ref[...])
    mag = jnp.exp(a_re * dt)
    lr = mag * jnp.cos(a_im * dt)
    li = mag * jnp.sin(a_im * dt)
    den = a_re * a_re + a_im * a_im
    cr = ((lr - 1.0) * a_re + li * a_im) / den
    ci = (li * a_re - (lr - 1.0) * a_im) / den
    lr_ref[...] = lr
    li_ref[...] = li
    b_re = bre_ref[...]
    b_im = bim_ref[...]
    bbr_ref[...] = cr[:, None, :] * b_re - ci[:, None, :] * b_im
    bbi_ref[...] = cr[:, None, :] * b_im + ci[:, None, :] * b_re


def _s5_weights(a_re, a_im, log_dt, b_re, b_im, c_re, c_im):
    g, p_, ch = S5_GROUPS, S5_STATE, S5_GROUP
    gp = jax.ShapeDtypeStruct((g, p_), F32)
    gcp = jax.ShapeDtypeStruct((g, ch, p_), F32)
    lr, li, bbr, bbi = pl.pallas_call(
        _s5_prep_kernel, out_shape=(gp, gp, gcp, gcp), name="s5_prep",
    )(a_re, a_im, log_dt[:, None], jnp.transpose(b_re, (0, 2, 1)), jnp.transpose(b_im, (0, 2, 1)))
    per_slab = LANES // ch
    eye = jnp.eye(per_slab, dtype=F32)

    def b_blocks(bb):
        bb = bb.reshape(S5_SLABS, per_slab, ch, p_)
        return jnp.einsum('jgcp,gh->jgchp', bb, eye).reshape(S5_SLABS, LANES, S5_SLAB_STATE)

    def c_blocks(cc):
        cc = cc.reshape(S5_SLABS, per_slab, ch, p_)
        return jnp.einsum('jgcp,gh->jgphc', cc, eye).reshape(S5_SLABS, S5_SLAB_STATE, LANES)

    w_b = jnp.concatenate([b_blocks(bbr), b_blocks(bbi)], axis=2).astype(BF16)
    w_c = jnp.concatenate([c_blocks(c_re), -c_blocks(c_im)], axis=1).astype(BF16)
    lam = jnp.stack([lr.reshape(S5_SLABS, S5_SLAB_STATE), li.reshape(S5_SLABS, S5_SLAB_STATE)], axis=1)
    return w_b, w_c, lam


def _s5_kernel(x_ref, x0r_ref, x0i_ref, gn_ref, wb_ref, wc_ref, lam_ref, d_ref, wglu_ref,
               o_ref, xr_ref, xi_ref, bur_ref, bui_ref, sr_ref, si_ref, shuf_ref=None, *, nb, t_len):
    step = pl.program_id(0)
    ns = S5_SLAB_STATE

    @pl.when(step == 0)
    def _():
        for j in range(S5_SLABS):
            sr_ref[j] = x0r_ref[:, j * ns:(j + 1) * ns]
            si_ref[j] = x0i_ref[:, j * ns:(j + 1) * ns]

    x = x_ref[...]
    u = _rms(x, gn_ref[...])
    ub = u.astype(BF16)
    ys = []
    for j in range(S5_SLABS):
        bu = jnp.dot(ub[:, j * LANES:(j + 1) * LANES], wb_ref[j], preferred_element_type=F32)
        bur_ref[j] = bu[:, :ns]
        bui_ref[j] = bu[:, ns:]
        lr = jnp.broadcast_to(lam_ref[j, 0:1, :], (SUBLANES, ns))
        li = jnp.broadcast_to(lam_ref[j, 1:2, :], (SUBLANES, ns))
        for gi in range(nb // SUBLANES):
            srow = slice(gi * SUBLANES, (gi + 1) * SUBLANES)
            xr = sr_ref[j, srow, :]
            xi = si_ref[j, srow, :]
            for t in range(t_len):
                rows = slice(t * nb + gi * SUBLANES, t * nb + (gi + 1) * SUBLANES)
                xr, xi = (lr * xr - li * xi + bur_ref[j, rows, :], lr * xi + li * xr + bui_ref[j, rows, :])
                bur_ref[j, rows, :] = xr
                bui_ref[j, rows, :] = xi
            sr_ref[j, srow, :] = xr
            si_ref[j, srow, :] = xi
        xs = jnp.concatenate([bur_ref[j].astype(BF16), bui_ref[j].astype(BF16)], axis=1)
        ys.append(jnp.dot(xs, wc_ref[j], preferred_element_type=F32))
    y = jnp.concatenate(ys, axis=1) + d_ref[...] * u
    gl = jax.nn.gelu(y)
    zg = _bdot(gl, wglu_ref[...])
    h = x + zg[:, :D_MODEL] * _sigmoid(zg[:, D_MODEL:])
    if shuf_ref is None:
        o_ref[...] = h
    else:
        for s in range(D_MODEL // LANES):
            shuf_ref[s] = h[:, s * LANES:(s + 1) * LANES]
        for bi in range(nb):
            for s in range(D_MODEL // LANES):
                o_ref[bi, :, s * LANES:(s + 1) * LANES] = shuf_ref[s, pl.ds(bi, t_len, stride=nb), :]

    @pl.when(step == pl.num_programs(0) - 1)
    def _():
        for j in range(S5_SLABS):
            xr_ref[:, j * ns:(j + 1) * ns] = sr_ref[j]
            xi_ref[:, j * ns:(j + 1) * ns] = si_ref[j]


def _s5(x_tm, x0r, x0i, g_norm, w_b, w_c, lam, d, w_glu, *, nb, t_tile, batch_major_out):
    n = x_tm.shape[0]
    rows = nb * t_tile
    n_state = S5_GROUPS * S5_STATE
    full = lambda shape: pl.BlockSpec(shape, lambda i: (0,) * len(shape))
    st = jax.ShapeDtypeStruct((nb, n_state), F32)
    scratch = [pltpu.VMEM((S5_SLABS, rows, S5_SLAB_STATE), F32),
               pltpu.VMEM((S5_SLABS, rows, S5_SLAB_STATE), F32),
               pltpu.VMEM((S5_SLABS, nb, S5_SLAB_STATE), F32),
               pltpu.VMEM((S5_SLABS, nb, S5_SLAB_STATE), F32)]
    if batch_major_out:
        out_spec = pl.BlockSpec((nb, t_tile, D_MODEL), lambda i: (0, i, 0))
        out_sds = jax.ShapeDtypeStruct((nb, n // nb, D_MODEL), F32)
        scratch.append(pltpu.VMEM((D_MODEL // LANES, rows, LANES), F32))
    else:
        out_spec = pl.BlockSpec((rows, D_MODEL), lambda i: (i, 0))
        out_sds = jax.ShapeDtypeStruct((n, D_MODEL), F32)
    return pl.pallas_call(
        functools.partial(_s5_kernel, nb=nb, t_len=t_tile),
        grid=(n // rows,),
        in_specs=[pl.BlockSpec((rows, D_MODEL), lambda i: (i, 0)), full((nb, n_state)), full((nb, n_state))]
        + [_vmem_spec()] * 6,
        out_specs=(out_spec, full((nb, n_state)), full((nb, n_state))),
        out_shape=(out_sds, st, st),
        scratch_shapes=scratch,
        compiler_params=_params("arbitrary"),
        name="s5_layer",
    )(x_tm, x0r, x0i, g_norm[None, :], w_b, w_c, lam, d[None, :], w_glu.astype(BF16))


def _to_time_major(h, batch, seq):
    return h.reshape(batch, seq, -1).transpose(1, 0, 2).reshape(batch * seq, -1)


def _to_batch_major(h, batch, seq):
    return h.reshape(seq, batch, -1).transpose(1, 0, 2).reshape(batch * seq, -1)


def kernel(x_prompt, x_sample, cache_ckv, cache_kpe, state_gla, state_s5_re, state_s5_im, page_table,
           p_prompt, p_sample, g_norm_ab, w_in_ab, g_qlat, w_uq, g_kvlat, w_ukv, g_qh, g_kh,
           w_gla_a2, b_gla_a, g_gla_o, w_out_ab, g_norm_c, s5_a_re, s5_a_im, s5_log_dt,
           s5_b_re, s5_b_im, s5_c_re, s5_c_im, s5_d, w_glu, g_norm_mlp, w_up, w_down,
           g_norm_ple, w_ple_gate, w_ple_proj):
    bp, sp, _ = x_prompt.shape
    bs, ts_len, _ = x_sample.shape
    n_pages = page_table.shape[1]
    past = n_pages * PAGE_SIZE
    n_p, n_s = bp * sp, bs * ts_len
    bf = lambda a: a.astype(BF16)

    cos16, sin16 = _rope_tables(past + ts_len)
    ones = jnp.ones((cos16.shape[0], 1), F32)
    tc = jnp.concatenate([ones * jnp.ones((1, NOPE_DIM), F32), cos16, cos16,
                          ones * jnp.ones((1, HEAD_PAD - QK_DIM), F32)], axis=1)
    ts = jnp.concatenate([ones * jnp.zeros((1, NOPE_DIM), F32), -sin16, sin16,
                          ones * jnp.zeros((1, HEAD_PAD - QK_DIM), F32)], axis=1)
    kc = jnp.concatenate([cos16, cos16], axis=1)
    ks = jnp.concatenate([-sin16, sin16], axis=1)
    tc_s = jnp.tile(tc[past:past + ts_len], (bs, 1))
    ts_s = jnp.tile(ts[past:past + ts_len], (bs, 1))

    post_w = lambda i: (g_norm_mlp[i][None, :], bf(w_up[i]), bf(w_down[i]), g_norm_ple[i][None, :],
                        bf(w_ple_gate[i]), bf(w_ple_proj[i]))

    fw, w_uk, w_uv = _front_weights(w_in_ab[0], g_norm_ab[0], g_qlat[0], w_uq[0], g_kvlat[0], w_ukv[0],
                                    g_qh[0], g_kh[0], w_gla_a2[0], b_gla_a[0])
    tm_p = 512
    (q_p, k_p, v_p, ckv_p, kpe_p, gq_p, gk_p, gv_p, lg_p, gr_p) = _front(
        x_prompt.reshape(n_p, D_MODEL), tc, ts, fw, tm=tm_p, table_tiles=sp // tm_p)
    (q_s, _, _, ckv_s, kpe_s, gq_s, gk_s, gv_s, lg_s, gr_s) = _front(
        x_sample.reshape(n_s, D_MODEL), tc_s, ts_s, fw, tm=n_s // 2, table_tiles=2)
    kpe_p = kpe_p[:, NOPE_DIM:QK_DIM]
    kpe_s = kpe_s[:, NOPE_DIM:QK_DIM]

    attn_p = _prompt_attn(q_p, k_p, v_p, batch=bp, seq=sp, tq=256)
    o_p, gla_p = _gla_prompt(gq_p, gk_p, gv_p, lg_p, gr_p, g_gla_o[0], batch=bp, seq=sp, tg=256)
    gla_p = jnp.swapaxes(gla_p, -1, -2)

    attn_s = _sample_attn(q_s, ckv_s, kpe_s, cache_ckv[0], cache_kpe[0], page_table, kc, ks, g_kh[0],
                          w_uk, w_uv, t_len=ts_len, chunk_pages=8)
    o_s, gla_s = _gla_sample(gq_s, gk_s, gv_s, lg_s, gr_s, g_gla_o[0], state_gla[0], t_len=ts_len, nb=8)

    w_out = bf(w_out_ab[0])
    hp = _post(x_prompt.reshape(n_p, D_MODEL), p_prompt[0].reshape(n_p, PLE_DIM), post_w(0),
               mix=(attn_p, o_p, w_out), tm=512)
    hs = _post(x_sample.reshape(n_s, D_MODEL), p_sample[0].reshape(n_s, PLE_DIM), post_w(0),
               mix=(attn_s, o_s, w_out), tm=n_s // 2)

    w_b, w_c, lam = _s5_weights(s5_a_re[0], s5_a_im[0], s5_log_dt[0], s5_b_re[0], s5_b_im[0],
                                s5_c_re[0], s5_c_im[0])
    n_state = S5_GROUPS * S5_STATE
    zero_state = jnp.zeros((bp, n_state), F32)
    hp, s5r_p, s5i_p = _s5(_to_time_major(hp, bp, sp), zero_state, zero_state, g_norm_c[0], w_b, w_c, lam,
                           s5_d[0], w_glu[0], nb=bp, t_tile=64, batch_major_out=True)
    hs_tm, s5r_s, s5i_s = _s5(_to_time_major(hs, bs, ts_len), state_s5_re[0].reshape(bs, n_state),
                              state_s5_im[0].reshape(bs, n_state), g_norm_c[0], w_b, w_c, lam,
                              s5_d[0], w_glu[0], nb=bs, t_tile=ts_len, batch_major_out=False)
    hp = _post(hp.reshape(n_p, D_MODEL), p_prompt[1].reshape(n_p, PLE_DIM), post_w(1), tm=512)
    hs = _post(_to_batch_major(hs_tm, bs, ts_len), p_sample[1].reshape(n_s, PLE_DIM), post_w(1), tm=n_s // 2)

    st = lambda a, b: a.reshape(1, b, S5_GROUPS, S5_STATE)
    return (hp.reshape(bp, sp, D_MODEL), hs.reshape(bs, ts_len, D_MODEL),
            ckv_p.reshape(1, bp, sp, KV_LORA), kpe_p.reshape(1, bp, sp, ROPE_DIM), gla_p[None],
            st(s5r_p, bp), st(s5i_p, bp),
            ckv_s.reshape(1, bs, ts_len, KV_LORA), kpe_s.reshape(1, bs, ts_len, ROPE_DIM), gla_s[None],
            st(s5r_s, bs), st(s5i_s, bs))
```

```python
import functools

import jax
import jax.numpy as jnp
from jax import lax
from jax.experimental import pallas as pl
from jax.experimental.pallas import tpu as pltpu

F32 = jnp.float32
BF16 = jnp.bfloat16

D_MODEL = 1024
PAGE_SIZE = 128
MLA_HEADS = 8
Q_LORA = 384
KV_LORA = 256
NOPE_DIM = 64
ROPE_DIM = 32
QK_DIM = NOPE_DIM + ROPE_DIM
V_DIM = 64
ROPE_THETA = 10000.0
GLA_HEADS = 4
GLA_DK = 128
GLA_DV = 128
GLA_GATE_RANK = 16
GLA_TAU = 16.0
GLA_CHUNK = 64
S5_GROUP = 16
S5_GROUPS = D_MODEL // S5_GROUP
S5_STATE = 64
D_FF = 4 * D_MODEL
PLE_DIM = 256
EPS = 1e-6
IN_SPLITS = (Q_LORA, KV_LORA, ROPE_DIM, GLA_HEADS * GLA_DK, GLA_HEADS * GLA_DK,
             GLA_HEADS * GLA_DV, GLA_GATE_RANK, GLA_HEADS * GLA_DV)

LANES = 128
SUBLANES = 8
HEAD_PAD = LANES
VMEM_LIMIT = 56 * 1024 * 1024
S5_SLABS = D_MODEL // LANES
S5_SLAB_STATE = (LANES // S5_GROUP) * S5_STATE
NEG_INF = float("-inf")
LOG2_E = 1.4426950408889634


def _vmem_spec():
    return pl.BlockSpec(memory_space=pltpu.VMEM)


def _params(*sem):
    return pltpu.CompilerParams(dimension_semantics=sem, vmem_limit_bytes=VMEM_LIMIT)


def _bdot(a, b):
    return jnp.dot(a.astype(BF16), b.astype(BF16), preferred_element_type=F32)


def _bdot_nt(a, b):
    return lax.dot_general(a.astype(BF16), b.astype(BF16), (((1,), (1,)), ((), ())),
                           preferred_element_type=F32)


def _bdot_tn(a, b):
    return lax.dot_general(a.astype(BF16), b.astype(BF16), (((0,), (0,)), ((), ())),
                           preferred_element_type=F32)


def _rms(x, g):
    return x * lax.rsqrt(jnp.mean(x * x, axis=-1, keepdims=True) + EPS) * g


def _sigmoid(x):
    return 1.0 / (1.0 + jnp.exp(-x))


def _rope_table_kernel(cos_ref, sin_ref):
    shape = cos_ref.shape
    half = ROPE_DIM // 2
    per_row = LANES // half
    row = lax.broadcasted_iota(jnp.int32, shape, 0)
    lane = lax.broadcasted_iota(jnp.int32, shape, 1)
    pos = (row * per_row + lane // half).astype(F32)
    freq = jnp.power(F32(ROPE_THETA), -(lane % half).astype(F32) / half)
    ang = pos * freq
    cos_ref[...] = jnp.cos(ang)
    sin_ref[...] = jnp.sin(ang)


def _rope_tables(n_pos):
    half = ROPE_DIM // 2
    per_row = LANES // half
    rows = -(-n_pos // per_row)
    rows = -(-rows // SUBLANES) * SUBLANES
    cos, sin = pl.pallas_call(
        _rope_table_kernel,
        out_shape=(jax.ShapeDtypeStruct((rows, LANES), F32),) * 2,
        name="rope_table",
    )()
    cos = cos.reshape(rows * per_row, half)[:n_pos]
    sin = sin.reshape(rows * per_row, half)[:n_pos]
    return cos, sin


def _front_kernel(x_ref, tc_ref, ts_ref, gn_ref, wcq_ref, wckv_ref, wkpe_ref, wgq_ref, wgk_ref, wgv_ref,
                  wgr_ref, wga_ref, gql_ref, wuq_ref, gkvl_ref, wuk_ref, wuv_ref, gqh_ref, gkh_ref,
                  wa2_ref, ba_ref,
                  q_ref, k_ref, v_ref, ckv_ref, kpe_ref, gq_ref, gk_ref, gv_ref, lg_ref, gr_ref):
    n = _rms(x_ref[...], gn_ref[...])
    nb = n.astype(BF16)
    tc = tc_ref[...]
    ts = ts_ref[...]
    lane = lax.broadcasted_iota(jnp.int32, tc.shape, 1)
    first_half = lane < NOPE_DIM + ROPE_DIM // 2

    def norm_rope(xh, g):
        ss = jnp.sum(xh * xh, axis=-1, keepdims=True) * (1.0 / QK_DIM)
        y = xh * lax.rsqrt(ss + EPS) * g
        partner = jnp.where(first_half, pltpu.roll(y, LANES - ROPE_DIM // 2, 1), pltpu.roll(y, ROPE_DIM // 2, 1))
        return y * tc + partner * ts

    cq = _rms(jnp.dot(nb, wcq_ref[...], preferred_element_type=F32), gql_ref[...])
    q = _bdot(cq, wuq_ref[...])
    ckv = _rms(jnp.dot(nb, wckv_ref[...], preferred_element_type=F32), gkvl_ref[...])
    ckv_ref[...] = ckv
    ckvb = ckv.astype(BF16)
    kn = jnp.dot(ckvb, wuk_ref[...], preferred_element_type=F32)
    v_ref[...] = jnp.dot(ckvb, wuv_ref[...], preferred_element_type=F32).astype(v_ref.dtype)
    kpe = jnp.dot(nb, wkpe_ref[...], preferred_element_type=F32)
    kpe_ref[...] = kpe
    gqh = gqh_ref[...]
    gkh = gkh_ref[...]
    scale = QK_DIM ** -0.5 * LOG2_E
    for h in range(MLA_HEADS):
        hs = slice(h * HEAD_PAD, (h + 1) * HEAD_PAD)
        q_ref[:, hs] = (norm_rope(q[:, hs], gqh) * scale).astype(q_ref.dtype)
        k_ref[:, hs] = norm_rope(kn[:, hs] + kpe, gkh).astype(k_ref.dtype)

    gq_ref[...] = jnp.dot(nb, wgq_ref[...], preferred_element_type=F32) * (GLA_DK ** -0.5)
    gk_ref[...] = jnp.dot(nb, wgk_ref[...], preferred_element_type=F32)
    gv_ref[...] = jnp.dot(nb, wgv_ref[...], preferred_element_type=F32)
    gr_ref[...] = jnp.dot(nb, wgr_ref[...], preferred_element_type=F32)
    ga = jnp.dot(nb, wga_ref[...], preferred_element_type=F32)
    logit = _bdot(ga, wa2_ref[...]) + ba_ref[...]
    log_sig = jnp.minimum(logit, 0.0) - jnp.log1p(jnp.exp(-jnp.abs(logit)))
    lg_ref[...] = log_sig * (1.0 / GLA_TAU)


def _front_weights(w_in, g_norm, g_qlat, w_uq, g_kvlat, w_ukv, g_qh, g_kh, w_a2, b_a):
    offs = [0]
    for s in IN_SPLITS:
        offs.append(offs[-1] + s)
    cols = [w_in[:, offs[i]:offs[i + 1]] for i in range(len(IN_SPLITS))]
    w_cq, w_ckv, w_kpe, w_gq, w_gk, w_gv, w_ga, w_gr = cols
    pad_head = HEAD_PAD - QK_DIM
    w_kpe_placed = jnp.pad(w_kpe, ((0, 0), (NOPE_DIM, HEAD_PAD - QK_DIM)))
    w_ga_pad = jnp.pad(w_ga, ((0, 0), (0, LANES - GLA_GATE_RANK)))
    w_a2_pad = jnp.pad(w_a2, ((0, LANES - GLA_GATE_RANK), (0, 0)))
    w_uq_pad = jnp.pad(w_uq.reshape(Q_LORA, MLA_HEADS, QK_DIM), ((0, 0), (0, 0), (0, pad_head)))
    w_uq_pad = w_uq_pad.reshape(Q_LORA, MLA_HEADS * HEAD_PAD)
    w_ukv3 = w_ukv.reshape(KV_LORA, MLA_HEADS, NOPE_DIM + V_DIM)
    w_uk = w_ukv3[:, :, :NOPE_DIM]
    w_uv = w_ukv3[:, :, NOPE_DIM:].reshape(KV_LORA, MLA_HEADS * V_DIM)
    w_uk_pad = jnp.pad(w_uk, ((0, 0), (0, 0), (0, HEAD_PAD - NOPE_DIM))).reshape(KV_LORA, MLA_HEADS * HEAD_PAD)
    g_qh_pad = jnp.pad(g_qh, (0, pad_head))[None, :]
    g_kh_pad = jnp.pad(g_kh, (0, pad_head))[None, :]
    bf = lambda a: a.astype(BF16)
    weights = (g_norm[None, :], bf(w_cq), bf(w_ckv), bf(w_kpe_placed), bf(w_gq), bf(w_gk), bf(w_gv), bf(w_gr),
               bf(w_ga_pad), g_qlat[None, :], bf(w_uq_pad), g_kvlat[None, :], bf(w_uk_pad), bf(w_uv),
               g_qh_pad, g_kh_pad, bf(w_a2_pad), b_a[None, :])
    return weights, w_uk, w_uv


def _front(x, tc, ts, weights, *, tm, table_tiles):
    n = x.shape[0]
    hk = GLA_HEADS * GLA_DK
    row = lambda w: pl.BlockSpec((tm, w), lambda i: (i, 0))
    tab = pl.BlockSpec((tm, LANES), lambda i: (i % table_tiles, 0))
    sds = lambda w, dt: jax.ShapeDtypeStruct((n, w), dt)
    out_shape = (sds(MLA_HEADS * HEAD_PAD, BF16), sds(MLA_HEADS * HEAD_PAD, BF16), sds(MLA_HEADS * V_DIM, BF16),
                 sds(KV_LORA, F32), sds(LANES, F32), sds(hk, F32), sds(hk, F32), sds(hk, F32), sds(hk, F32),
                 sds(hk, F32))
    out_specs = (row(MLA_HEADS * HEAD_PAD), row(MLA_HEADS * HEAD_PAD), row(MLA_HEADS * V_DIM), row(KV_LORA),
                 row(LANES), row(hk), row(hk), row(hk), row(hk), row(hk))
    return pl.pallas_call(
        _front_kernel,
        grid=(n // tm,),
        in_specs=[row(D_MODEL), tab, tab] + [_vmem_spec()] * len(weights),
        out_specs=out_specs,
        out_shape=out_shape,
        compiler_params=_params("parallel"),
        name="ab_front",
    )(x, tc, ts, *weights)


def _prompt_attn_kernel(q_ref, k_ref, v_ref, o_ref, *, tq, n_q):
    i = pl.program_id(1)
    row = lax.broadcasted_iota(jnp.int32, (tq, tq), 0)
    col = lax.broadcasted_iota(jnp.int32, (tq, tq), 1)
    causal = col <= row
    lane = lax.broadcasted_iota(jnp.int32, (tq, LANES), 1)

    def case(c):
        kv = (c + 1) * tq
        outs = []
        for h in range(MLA_HEADS):
            hs = slice(h * HEAD_PAD, (h + 1) * HEAD_PAD)
            vs = slice((h // 2) * LANES, (h // 2 + 1) * LANES)
            s = lax.dot_general(q_ref[:, hs], k_ref[:kv, hs], (((1,), (1,)), ((), ())),
                                preferred_element_type=F32)
            diag = jnp.where(causal, s[:, c * tq:], NEG_INF)
            s = diag if c == 0 else jnp.concatenate([s[:, :c * tq], diag], axis=1)
            p = jnp.exp2(s - jnp.max(s, axis=-1, keepdims=True))
            l = jnp.sum(p, axis=-1, keepdims=True)
            outs.append(jnp.dot(p.astype(BF16), v_ref[:kv, vs], preferred_element_type=F32) / l)
        for pair in range(MLA_HEADS // 2):
            o_ref[:, pair * LANES:(pair + 1) * LANES] = jnp.where(
                lane < V_DIM, outs[2 * pair], outs[2 * pair + 1]).astype(o_ref.dtype)

    for c in range(n_q):
        pl.when(i == c)(functools.partial(case, c))


def _prompt_attn(q, k, v, *, batch, seq, tq):
    nq = seq // tq
    return pl.pallas_call(
        functools.partial(_prompt_attn_kernel, tq=tq, n_q=nq),
        grid=(batch, nq),
        in_specs=[pl.BlockSpec((tq, MLA_HEADS * HEAD_PAD), lambda b, i: (b * nq + i, 0)),
                  pl.BlockSpec((seq, MLA_HEADS * HEAD_PAD), lambda b, i: (b, 0)),
                  pl.BlockSpec((seq, MLA_HEADS * V_DIM), lambda b, i: (b, 0))],
        out_specs=pl.BlockSpec((tq, MLA_HEADS * V_DIM), lambda b, i: (b * nq + i, 0)),
        out_shape=jax.ShapeDtypeStruct((batch * seq, MLA_HEADS * V_DIM), BF16),
        compiler_params=_params("parallel", "arbitrary"),
        name="mla_prompt_attn",
    )(q, k, v)


def _gla_intra(q, k, b):
    c = q.shape[0]
    row = lax.broadcasted_iota(jnp.int32, q.shape, 0)
    r2 = lax.broadcasted_iota(jnp.int32, (c, c), 0)
    c2 = lax.broadcasted_iota(jnp.int32, (c, c), 1)
    att = jnp.where(r2 == c2, _bdot_nt(q, k), 0.0)
    before = pltpu.roll(b, 1, 0)
    last = b
    m = 1
    while m < c:
        upper = (row // m) % 2 == 1
        qm = jnp.where(upper, q * jnp.exp(b - before), 0.0)
        km = jnp.where(upper, 0.0, k * jnp.exp(last - b))
        same = (r2 // (2 * m)) == (c2 // (2 * m))
        att = att + jnp.where(same, _bdot_nt(qm, km), 0.0)
        if 2 * m < c:
            before = jnp.where(upper, pltpu.roll(before, m, 0), before)
            last = jnp.where(upper, last, pltpu.roll(last, c - m, 0))
        m *= 2
    return att


def _cumsum_rows(g):
    c = g.shape[0]
    row = lax.broadcasted_iota(jnp.int32, g.shape, 0)
    b = g
    sh = 1
    while sh < c:
        b = b + jnp.where(row >= sh, pltpu.roll(b, sh, 0), 0.0)
        sh *= 2
    return b


def _gla_prompt_kernel(gq_ref, gk_ref, gv_ref, lg_ref, gr_ref, go_ref, o_ref, st_ref, state_ref, *, n_chunks):
    t = pl.program_id(1)

    @pl.when(t == 0)
    def _():
        state_ref[...] = jnp.zeros_like(state_ref)

    go = go_ref[...]

    for h in range(GLA_HEADS):
        hs = slice(h * GLA_DK, (h + 1) * GLA_DK)
        st = state_ref[h]
        for ci in range(n_chunks):
            rows = slice(ci * GLA_CHUNK, (ci + 1) * GLA_CHUNK)
            q = gq_ref[rows, hs]
            k = gk_ref[rows, hs]
            v = gv_ref[rows, hs]
            b = _cumsum_rows(lg_ref[rows, hs])
            o = _bdot_nt(q * jnp.exp(b), st) + _bdot(_gla_intra(q, k, b), v)
            bl = b[GLA_CHUNK - 1:GLA_CHUNK, :]
            st = st * jnp.exp(bl) + _bdot_tn(v, k * jnp.exp(bl - b))
            gr = gr_ref[rows, hs]
            o_ref[rows, hs] = (_rms(o, go) * (gr * _sigmoid(gr))).astype(o_ref.dtype)
        state_ref[h] = st

    @pl.when(t == pl.num_programs(1) - 1)
    def _():
        st_ref[0] = state_ref[...]


def _gla_prompt(gq, gk, gv, lg, gr, g_o, *, batch, seq, tg):
    nt = seq // tg
    hk = GLA_HEADS * GLA_DK
    row = pl.BlockSpec((tg, hk), lambda b, t: (b * nt + t, 0))
    return pl.pallas_call(
        functools.partial(_gla_prompt_kernel, n_chunks=tg // GLA_CHUNK),
        grid=(batch, nt),
        in_specs=[row, row, row, row, row, _vmem_spec()],
        out_specs=(row, pl.BlockSpec((1, GLA_HEADS, GLA_DV, GLA_DK), lambda b, t: (b, 0, 0, 0))),
        out_shape=(jax.ShapeDtypeStruct((batch * seq, hk), BF16),
                   jax.ShapeDtypeStruct((batch, GLA_HEADS, GLA_DV, GLA_DK), F32)),
        scratch_shapes=[pltpu.VMEM((GLA_HEADS, GLA_DV, GLA_DK), F32)],
        compiler_params=_params("parallel", "arbitrary"),
        name="gla_prompt",
    )(gq, gk, gv, lg, gr, g_o[None, :])


def _gla_sample_kernel(gq_ref, gk_ref, gv_ref, lg_ref, gr_ref, go_ref, s0_ref, o_ref, s1_ref, *, nb, t_len):
    rows = nb * t_len
    per_tile = SUBLANES // t_len
    tok = lax.broadcasted_iota(jnp.int32, (rows, GLA_DK), 0) % t_len
    tile_batch = lax.broadcasted_iota(jnp.int32, (SUBLANES, GLA_DK), 0) // t_len
    pad = jnp.zeros((LANES - rows, LANES), F32)
    go = go_ref[...]
    for h in range(GLA_HEADS):
        hs = slice(h * GLA_DK, (h + 1) * GLA_DK)
        q = gq_ref[:, hs]
        k = gk_ref[:, hs]
        v = gv_ref[:, hs]
        b = lg_ref[:, hs]
        sh = 1
        while sh < t_len:
            b = b + jnp.where(tok >= sh, pltpu.roll(b, sh, 0), 0.0)
            sh *= 2
        bl = jnp.where(tok == t_len - 1, b, 0.0)
        for d in range(1, t_len):
            bl = bl + jnp.where(tok == t_len - 1 - d, pltpu.roll(b, rows - d, 0), 0.0)
        qe = q * jnp.exp(b)
        kd = k * jnp.exp(bl - b)
        o = jnp.sum(q * k, axis=-1, keepdims=True) * v
        for d in range(1, t_len):
            w = jnp.sum(q * pltpu.roll(k, d, 0) * jnp.exp(b - pltpu.roll(b, d, 0)), axis=-1, keepdims=True)
            o = o + jnp.where(tok[:, :1] >= d, w, 0.0) * pltpu.roll(v, d, 0)
        decay_cols = jnp.transpose(jnp.concatenate([jnp.exp(bl), pad], axis=0))
        inter = []
        for tile in range(rows // SUBLANES):
            ts = slice(tile * SUBLANES, (tile + 1) * SUBLANES)
            acc = jnp.zeros((SUBLANES, GLA_DV), F32)
            for e in range(per_tile):
                bi = tile * per_tile + e
                s0 = s0_ref[bi, h]
                mine = tile_batch == e
                acc = jnp.where(mine, _bdot(qe[ts], s0), acc)
                r_last = bi * t_len + t_len - 1
                s1_ref[bi, h] = s0 * decay_cols[:, r_last:r_last + 1] + _bdot_tn(jnp.where(mine, kd[ts], 0.0), v[ts])
            inter.append(acc)
        o = o + jnp.concatenate(inter, axis=0)
        gr = gr_ref[:, hs]
        o_ref[:, hs] = (_rms(o, go) * (gr * _sigmoid(gr))).astype(o_ref.dtype)


def _gla_sample(gq, gk, gv, lg, gr, g_o, s0, *, t_len, nb):
    n = gq.shape[0]
    batch = n // t_len
    hk = GLA_HEADS * GLA_DK
    rows = nb * t_len
    row = pl.BlockSpec((rows, hk), lambda i: (i, 0))
    st = pl.BlockSpec((nb, GLA_HEADS, GLA_DK, GLA_DV), lambda i: (i, 0, 0, 0))
    return pl.pallas_call(
        functools.partial(_gla_sample_kernel, nb=nb, t_len=t_len),
        grid=(batch // nb,),
        in_specs=[row, row, row, row, row, _vmem_spec(), st],
        out_specs=(row, st),
        out_shape=(jax.ShapeDtypeStruct((n, hk), BF16), jax.ShapeDtypeStruct(s0.shape, s0.dtype)),
        compiler_params=_params("parallel"),
        name="gla_sample",
    )(gq, gk, gv, lg, gr, g_o[None, :], s0)


def _sample_attn_kernel(pt_ref, q_ref, cnew_ref, pnew_ref, kc_ref, ks_ref, kcn_ref, ksn_ref, gk_ref, gpe_ref,
                        wukt_ref, wukp_ref, wuv_ref, ckv_hbm, kpe_hbm, o_ref, ckv_buf, kpe_buf, sem,
                        *, n_pages, chunk_pages, t_len):
    n_cols = t_len * MLA_HEADS
    chunk = chunk_pages * PAGE_SIZE
    n_chunks = n_pages // chunk_pages
    past = n_pages * PAGE_SIZE
    b = pl.program_id(0)
    slot = b % 2

    def page_copies(batch, sl, i):
        page = pt_ref[batch, i]
        ci = i // chunk_pages
        return (pltpu.make_async_copy(ckv_hbm.at[page], ckv_buf.at[sl, i], sem.at[sl, ci, 0]),
                pltpu.make_async_copy(kpe_hbm.at[page], kpe_buf.at[sl, i], sem.at[sl, ci, 1]))

    def start_pages(batch, sl, pages):
        for i in pages:
            for cp in page_copies(batch, sl, i):
                cp.start()

    def wait_pages(batch, sl, pages):
        for i in pages:
            for cp in page_copies(batch, sl, i):
                cp.wait()

    @pl.when(b == 0)
    def _():
        start_pages(0, 0, range(n_pages))

    qg = q_ref[...].astype(F32) * gk_ref[...]
    gpe = gpe_ref[...]
    sub = lax.broadcasted_iota(jnp.int32, (SUBLANES, MLA_HEADS * HEAD_PAD), 0)
    lane_head = lax.broadcasted_iota(jnp.int32, (SUBLANES, MLA_HEADS * HEAD_PAD), 1) // HEAD_PAD
    qexp = jnp.concatenate(
        [jnp.where(lane_head == sub, jnp.broadcast_to(qg[t:t + 1, :], sub.shape), 0.0) for t in range(t_len)],
        axis=0)
    qt = _bdot(qexp, wukt_ref[...])
    lhs = jnp.concatenate([wukp_ref[...], qt[:, :KV_LORA].astype(BF16)], axis=0)
    qr = qt[:, KV_LORA:KV_LORA + ROPE_DIM].astype(BF16)
    n_norm = NOPE_DIM * MLA_HEADS
    half = ROPE_DIM // 2

    def scores(c, kt, kcos, ksin):
        n = c.shape[0]
        cb = c.astype(BF16)
        r = lax.dot_general(lhs, cb, (((1,), (1,)), ((), ())), preferred_element_type=F32)
        kn = r[:n_norm].reshape(NOPE_DIM, MLA_HEADS, n)
        ss = jnp.sum(kn * kn, axis=0) + jnp.sum(kt * kt, axis=0, keepdims=True)
        rs = lax.rsqrt(ss * (1.0 / QK_DIM) + EPS)
        kg = kt * gpe
        kr = kg * kcos + jnp.concatenate([kg[half:], kg[:half]], axis=0) * ksin
        s = r[n_norm:] + jnp.dot(qr, kr.astype(BF16), preferred_element_type=F32)
        s = (s.reshape(t_len, MLA_HEADS, n) * rs[None]).reshape(n_cols, n)
        return cb, s

    def fold(state, cb, s):
        m, l, acc = state
        m_new = jnp.maximum(m, jnp.max(s, axis=1, keepdims=True))
        alpha = jnp.exp2(m - m_new)
        p = jnp.exp2(s - m_new)
        l = alpha * l + jnp.sum(p, axis=1, keepdims=True)
        acc = alpha * acc + jnp.dot(p.astype(BF16), cb, preferred_element_type=F32)
        return m_new, l, acc

    @pl.when(b + 1 < pl.num_programs(0))
    def _():
        start_pages(b + 1, 1 - slot, range(n_pages))

    wait_pages(b, slot, range(n_pages))
    state = (jnp.full((n_cols, 1), NEG_INF, F32), jnp.zeros((n_cols, 1), F32), jnp.zeros((n_cols, KV_LORA), F32))
    for ci in range(n_chunks):
        pages = range(ci * chunk_pages, (ci + 1) * chunk_pages)
        c = ckv_buf[slot, pl.ds(ci * chunk_pages, chunk_pages)].reshape(chunk, KV_LORA)
        kt = jnp.concatenate([kpe_buf[slot, p] for p in pages], axis=1)
        keys = slice(ci * chunk, (ci + 1) * chunk)
        state = fold(state, *scores(c, kt, kc_ref[:, keys], ks_ref[:, keys]))

    zpad = lambda a: jnp.concatenate([a, jnp.zeros((PAGE_SIZE - t_len, a.shape[1]), a.dtype)], axis=0)
    pnew = jnp.concatenate([zpad(pnew_ref[...]), jnp.zeros((PAGE_SIZE, LANES - ROPE_DIM), F32)], axis=1)
    cb, s = scores(zpad(cnew_ref[...]), jnp.transpose(pnew)[:ROPE_DIM], kcn_ref[...], ksn_ref[...])
    r2 = lax.broadcasted_iota(jnp.int32, (n_cols, PAGE_SIZE), 0)
    c2 = lax.broadcasted_iota(jnp.int32, (n_cols, PAGE_SIZE), 1)
    s = jnp.where((c2 < t_len) & (c2 <= r2 // MLA_HEADS), s, NEG_INF)
    _, l, acc = fold(state, cb, s)

    lat = acc / l
    res = _bdot(lat, wuv_ref[...])
    r3 = lax.broadcasted_iota(jnp.int32, res.shape, 0)
    c3 = lax.broadcasted_iota(jnp.int32, res.shape, 1)
    res = jnp.where(c3 // V_DIM == r3 % MLA_HEADS, res, 0.0)
    o_ref[...] = jnp.sum(res.reshape(t_len, MLA_HEADS, MLA_HEADS * V_DIM), axis=1).astype(o_ref.dtype)


def _sample_attn(q, ckv_new, kpe_new, cache_ckv, cache_kpe, page_table, kc, ks, g_kh, w_uk, w_uv,
                 *, t_len, chunk_pages):
    batch, n_pages = page_table.shape
    past = n_pages * PAGE_SIZE
    w_ukt = jnp.pad(jnp.transpose(w_uk, (1, 2, 0)), ((0, 0), (0, HEAD_PAD - NOPE_DIM), (0, 0)))
    w_ukt = w_ukt.reshape(MLA_HEADS * HEAD_PAD, KV_LORA)
    sel = jnp.pad(jnp.eye(ROPE_DIM, dtype=F32), ((NOPE_DIM, HEAD_PAD - QK_DIM), (0, LANES - ROPE_DIM)))
    w_ukt_ext = jnp.concatenate([w_ukt, jnp.tile(sel, (MLA_HEADS, 1))], axis=1).astype(BF16)
    w_uk_perm = jnp.transpose(w_uk, (2, 1, 0)).reshape(NOPE_DIM * MLA_HEADS, KV_LORA).astype(BF16)
    g_k_pad = jnp.concatenate([g_kh[:NOPE_DIM], jnp.ones((ROPE_DIM,), F32), jnp.zeros((HEAD_PAD - QK_DIM,), F32)])
    g_k_pad = jnp.tile(g_k_pad, MLA_HEADS)[None, :]
    g_pe = g_kh[NOPE_DIM:][:, None]
    kc, ks = jnp.transpose(kc), jnp.transpose(ks)
    kcn = jnp.pad(kc[:, past:past + t_len], ((0, 0), (0, PAGE_SIZE - t_len)))
    ksn = jnp.pad(ks[:, past:past + t_len], ((0, 0), (0, PAGE_SIZE - t_len)))
    per_b = lambda w: pl.BlockSpec((None, t_len, w), lambda b, pt: (b, 0, 0))
    hbm = pl.BlockSpec(memory_space=pl.ANY)
    in_specs = ([per_b(MLA_HEADS * HEAD_PAD), per_b(KV_LORA), per_b(ROPE_DIM)] + [_vmem_spec()] * 9 + [hbm, hbm])
    grid_spec = pltpu.PrefetchScalarGridSpec(
        num_scalar_prefetch=1,
        grid=(batch,),
        in_specs=in_specs,
        out_specs=pl.BlockSpec((None, t_len, MLA_HEADS * V_DIM), lambda b, pt: (b, 0, 0)),
        scratch_shapes=[pltpu.VMEM((2, n_pages, PAGE_SIZE, KV_LORA), F32),
                        pltpu.VMEM((2, n_pages, ROPE_DIM, PAGE_SIZE), F32),
                        pltpu.SemaphoreType.DMA((2, n_pages // chunk_pages, 2))],
    )
    out = pl.pallas_call(
        functools.partial(_sample_attn_kernel, n_pages=n_pages, chunk_pages=chunk_pages, t_len=t_len),
        grid_spec=grid_spec,
        out_shape=jax.ShapeDtypeStruct((batch, t_len, MLA_HEADS * V_DIM), BF16),
        compiler_params=_params("arbitrary"),
        name="mla_sample_attn",
    )(page_table, q.reshape(batch, t_len, -1), ckv_new.reshape(batch, t_len, -1),
      kpe_new.reshape(batch, t_len, -1), kc[:, :past], ks[:, :past], kcn, ksn, g_k_pad, g_pe,
      w_ukt_ext, w_uk_perm, w_uv.astype(BF16), cache_ckv, jnp.swapaxes(cache_kpe, 1, 2))
    return out.reshape(batch * t_len, MLA_HEADS * V_DIM)


def _mlp_ple(h, p_ref, gm_ref, wup_ref, wdown_ref, gp_ref, wgate_ref, wproj_ref, o_ref, ff_chunk):
    nb = _rms(h, gm_ref[...]).astype(BF16)
    y = jnp.zeros_like(h)
    for c in range(D_FF // ff_chunk):
        cs = slice(c * ff_chunk, (c + 1) * ff_chunk)
        a = jnp.maximum(jnp.dot(nb, wup_ref[:, cs], preferred_element_type=F32), 0.0)
        y = y + jnp.dot((a * a).astype(BF16), wdown_ref[cs, :], preferred_element_type=F32)
    h = h + y
    gate = _sigmoid(_bdot(_rms(h, gp_ref[...]), wgate_ref[...]))
    o_ref[...] = h + gate * _bdot(p_ref[...], wproj_ref[...])


def _post_mix_kernel(h_ref, a_ref, o2_ref, p_ref, wout_ref, gm_ref, wup_ref, wdown_ref, gp_ref, wgate_ref,
                     wproj_ref, o_ref, *, ff_chunk):
    n_attn = MLA_HEADS * V_DIM
    h = (h_ref[...] + jnp.dot(a_ref[...], wout_ref[:n_attn, :], preferred_element_type=F32)
         + jnp.dot(o2_ref[...], wout_ref[n_attn:, :], preferred_element_type=F32))
    _mlp_ple(h, p_ref, gm_ref, wup_ref, wdown_ref, gp_ref, wgate_ref, wproj_ref, o_ref, ff_chunk)


def _post_kernel(h_ref, p_ref, gm_ref, wup_ref, wdown_ref, gp_ref, wgate_ref, wproj_ref, o_ref, *, ff_chunk):
    _mlp_ple(h_ref[...], p_ref, gm_ref, wup_ref, wdown_ref, gp_ref, wgate_ref, wproj_ref, o_ref, ff_chunk)


def _post(h, p, weights, mix=None, *, tm, ff_chunk=1024):
    n = h.shape[0]
    row = lambda w: pl.BlockSpec((tm, w), lambda i: (i, 0))
    if mix is None:
        body, acts, specs = _post_kernel, (h, p), [row(D_MODEL), row(PLE_DIM)]
    else:
        attn, o_gla, w_out = mix
        body, acts = _post_mix_kernel, (h, attn, o_gla, p)
        specs = [row(D_MODEL), row(attn.shape[1]), row(o_gla.shape[1]), row(PLE_DIM)]
        weights = (w_out,) + tuple(weights)
    return pl.pallas_call(
        functools.partial(body, ff_chunk=ff_chunk),
        grid=(n // tm,),
        in_specs=specs + [_vmem_spec()] * len(weights),
        out_specs=row(D_MODEL),
        out_shape=jax.ShapeDtypeStruct((n, D_MODEL), F32),
        compiler_params=_params("parallel"),
        name="post_mix" if mix is not None else "post",
    )(*acts, *weights)


def _s5_prep_kernel(are_ref, aim_ref, ldt_ref, bre_ref, bim_ref, lr_ref, li_ref, bbr_ref, bbi_ref):
    a_re = are_ref[...]
    a_im = aim_ref[...]
    dt = jnp.exp(ldt_ref[...])
    mag = jnp.exp(a_re * dt)
    lr = mag * jnp.cos(a_im * dt)
    li = mag * jnp.sin(a_im * dt)
    den = a_re * a_re + a_im * a_im
    cr = ((lr - 1.0) * a_re + li * a_im) / den
    ci = (li * a_re - (lr - 1.0) * a_im) / den
    lr_ref[...] = lr
    li_ref[...] = li
    b_re = bre_ref[...]
    b_im = bim_ref[...]
    bbr_ref[...] = cr[:, None, :] * b_re - ci[:, None, :] * b_im
    bbi_ref[...] = cr[:, None, :] * b_im + ci[:, None, :] * b_re


def _s5_weights(a_re, a_im, log_dt, b_re, b_im, c_re, c_im):
    g, p_, ch = S5_GROUPS, S5_STATE, S5_GROUP
    gp = jax.ShapeDtypeStruct((g, p_), F32)
    gcp = jax.ShapeDtypeStruct((g, ch, p_), F32)
    lr, li, bbr, bbi = pl.pallas_call(
        _s5_prep_kernel, out_shape=(gp, gp, gcp, gcp), name="s5_prep",
    )(a_re, a_im, log_dt[:, None], jnp.transpose(b_re, (0, 2, 1)), jnp.transpose(b_im, (0, 2, 1)))
    per_slab = LANES // ch
    eye = jnp.eye(per_slab, dtype=F32)

    def b_blocks(bb):
        bb = bb.reshape(S5_SLABS, per_slab, ch, p_)
        return jnp.einsum('jgcp,gh->jgchp', bb, eye).reshape(S5_SLABS, LANES, S5_SLAB_STATE)

    def c_blocks(cc):
        cc = cc.reshape(S5_SLABS, per_slab, ch, p_)
        return jnp.einsum('jgcp,gh->jgphc', cc, eye).reshape(S5_SLABS, S5_SLAB_STATE, LANES)

    w_b = jnp.concatenate([b_blocks(bbr), b_blocks(bbi)], axis=2).astype(BF16)
    w_c = jnp.concatenate([c_blocks(c_re), -c_blocks(c_im)], axis=1).astype(BF16)
    lam = jnp.stack([lr.reshape(S5_SLABS, S5_SLAB_STATE), li.reshape(S5_SLABS, S5_SLAB_STATE)], axis=1)
    return w_b, w_c, lam


def _s5_kernel(x_ref, x0r_ref, x0i_ref, gn_ref, wb_ref, wc_ref, lam_ref, d_ref, wglu_ref,
               o_ref, xr_ref, xi_ref, bur_ref, bui_ref, sr_ref, si_ref, shuf_ref=None, *, nb, t_len):
    step = pl.program_id(0)
    ns = S5_SLAB_STATE

    @pl.when(step == 0)
    def _():
        for j in range(S5_SLABS):
            sr_ref[j] = x0r_ref[:, j * ns:(j + 1) * ns]
            si_ref[j] = x0i_ref[:, j * ns:(j + 1) * ns]

    x = x_ref[...]
    u = _rms(x, gn_ref[...])
    ub = u.astype(BF16)
    ys = []
    for j in range(S5_SLABS):
        bu = jnp.dot(ub[:, j * LANES:(j + 1) * LANES], wb_ref[j], preferred_element_type=F32)
        bur_ref[j] = bu[:, :ns]
        bui_ref[j] = bu[:, ns:]
        lr = jnp.broadcast_to(lam_ref[j, 0:1, :], (SUBLANES, ns))
        li = jnp.broadcast_to(lam_ref[j, 1:2, :], (SUBLANES, ns))
        for gi in range(nb // SUBLANES):
            srow = slice(gi * SUBLANES, (gi + 1) * SUBLANES)
            xr = sr_ref[j, srow, :]
            xi = si_ref[j, srow, :]
            for t in range(t_len):
                rows = slice(t * nb + gi * SUBLANES, t * nb + (gi + 1) * SUBLANES)
                xr, xi = (lr * xr - li * xi + bur_ref[j, rows, :], lr * xi + li * xr + bui_ref[j, rows, :])
                bur_ref[j, rows, :] = xr
                bui_ref[j, rows, :] = xi
            sr_ref[j, srow, :] = xr
            si_ref[j, srow, :] = xi
        xs = jnp.concatenate([bur_ref[j].astype(BF16), bui_ref[j].astype(BF16)], axis=1)
        ys.append(jnp.dot(xs, wc_ref[j], preferred_element_type=F32))
    y = jnp.concatenate(ys, axis=1) + d_ref[...] * u
    gl = jax.nn.gelu(y)
    zg = _bdot(gl, wglu_ref[...])
    h = x + zg[:, :D_MODEL] * _sigmoid(zg[:, D_MODEL:])
    if shuf_ref is None:
        o_ref[...] = h
    else:
        for s in range(D_MODEL // LANES):
            shuf_ref[s] = h[:, s * LANES:(s + 1) * LANES]
        for bi in range(nb):
            for s in range(D_MODEL // LANES):
                o_ref[bi, :, s * LANES:(s + 1) * LANES] = shuf_ref[s, pl.ds(bi, t_len, stride=nb), :]

    @pl.when(step == pl.num_programs(0) - 1)
    def _():
        for j in range(S5_SLABS):
            xr_ref[:, j * ns:(j + 1) * ns] = sr_ref[j]
            xi_ref[:, j * ns:(j + 1) * ns] = si_ref[j]


def _s5(x_tm, x0r, x0i, g_norm, w_b, w_c, lam, d, w_glu, *, nb, t_tile, batch_major_out):
    n = x_tm.shape[0]
    rows = nb * t_tile
    n_state = S5_GROUPS * S5_STATE
    full = lambda shape: pl.BlockSpec(shape, lambda i: (0,) * len(shape))
    st = jax.ShapeDtypeStruct((nb, n_state), F32)
    scratch = [pltpu.VMEM((S5_SLABS, rows, S5_SLAB_STATE), F32),
               pltpu.VMEM((S5_SLABS, rows, S5_SLAB_STATE), F32),
               pltpu.VMEM((S5_SLABS, nb, S5_SLAB_STATE), F32),
               pltpu.VMEM((S5_SLABS, nb, S5_SLAB_STATE), F32)]
    if batch_major_out:
        out_spec = pl.BlockSpec((nb, t_tile, D_MODEL), lambda i: (0, i, 0))
        out_sds = jax.ShapeDtypeStruct((nb, n // nb, D_MODEL), F32)
        scratch.append(pltpu.VMEM((D_MODEL // LANES, rows, LANES), F32))
    else:
        out_spec = pl.BlockSpec((rows, D_MODEL), lambda i: (i, 0))
        out_sds = jax.ShapeDtypeStruct((n, D_MODEL), F32)
    return pl.pallas_call(
        functools.partial(_s5_kernel, nb=nb, t_len=t_tile),
        grid=(n // rows,),
        in_specs=[pl.BlockSpec((rows, D_MODEL), lambda i: (i, 0)), full((nb, n_state)), full((nb, n_state))]
        + [_vmem_spec()] * 6,
        out_specs=(out_spec, full((nb, n_state)), full((nb, n_state))),
        out_shape=(out_sds, st, st),
        scratch_shapes=scratch,
        compiler_params=_params("arbitrary"),
        name="s5_layer",
    )(x_tm, x0r, x0i, g_norm[None, :], w_b, w_c, lam, d[None, :], w_glu.astype(BF16))


def _to_time_major(h, batch, seq):
    return h.reshape(batch, seq, -1).transpose(1, 0, 2).reshape(batch * seq, -1)


def _to_batch_major(h, batch, seq):
    return h.reshape(seq, batch, -1).transpose(1, 0, 2).reshape(batch * seq, -1)


def kernel(x_prompt, x_sample, cache_ckv, cache_kpe, state_gla, state_s5_re, state_s5_im, page_table,
           p_prompt, p_sample, g_norm_ab, w_in_ab, g_qlat, w_uq, g_kvlat, w_ukv, g_qh, g_kh,
           w_gla_a2, b_gla_a, g_gla_o, w_out_ab, g_norm_c, s5_a_re, s5_a_im, s5_log_dt,
           s5_b_re, s5_b_im, s5_c_re, s5_c_im, s5_d, w_glu, g_norm_mlp, w_up, w_down,
           g_norm_ple, w_ple_gate, w_ple_proj):
    bp, sp, _ = x_prompt.shape
    bs, ts_len, _ = x_sample.shape
    n_pages = page_table.shape[1]
    past = n_pages * PAGE_SIZE
    n_p, n_s = bp * sp, bs * ts_len
    bf = lambda a: a.astype(BF16)

    cos16, sin16 = _rope_tables(past + ts_len)
    ones = jnp.ones((cos16.shape[0], 1), F32)
    tc = jnp.concatenate([ones * jnp.ones((1, NOPE_DIM), F32), cos16, cos16,
                          ones * jnp.ones((1, HEAD_PAD - QK_DIM), F32)], axis=1)
    ts = jnp.concatenate([ones * jnp.zeros((1, NOPE_DIM), F32), -sin16, sin16,
                          ones * jnp.zeros((1, HEAD_PAD - QK_DIM), F32)], axis=1)
    kc = jnp.concatenate([cos16, cos16], axis=1)
    ks = jnp.concatenate([-sin16, sin16], axis=1)
    tc_s = jnp.tile(tc[past:past + ts_len], (bs, 1))
    ts_s = jnp.tile(ts[past:past + ts_len], (bs, 1))

    post_w = lambda i: (g_norm_mlp[i][None, :], bf(w_up[i]), bf(w_down[i]), g_norm_ple[i][None, :],
                        bf(w_ple_gate[i]), bf(w_ple_proj[i]))

    fw, w_uk, w_uv = _front_weights(w_in_ab[0], g_norm_ab[0], g_qlat[0], w_uq[0], g_kvlat[0], w_ukv[0],
                                    g_qh[0], g_kh[0], w_gla_a2[0], b_gla_a[0])
    tm_p = 512
    (q_p, k_p, v_p, ckv_p, kpe_p, gq_p, gk_p, gv_p, lg_p, gr_p) = _front(
        x_prompt.reshape(n_p, D_MODEL), tc, ts, fw, tm=tm_p, table_tiles=sp // tm_p)
    (q_s, _, _, ckv_s, kpe_s, gq_s, gk_s, gv_s, lg_s, gr_s) = _front(
        x_sample.reshape(n_s, D_MODEL), tc_s, ts_s, fw, tm=n_s // 2, table_tiles=2)
    kpe_p = kpe_p[:, NOPE_DIM:QK_DIM]
    kpe_s = kpe_s[:, NOPE_DIM:QK_DIM]

    attn_p = _prompt_attn(q_p, k_p, v_p, batch=bp, seq=sp, tq=256)
    o_p, gla_p = _gla_prompt(gq_p, gk_p, gv_p, lg_p, gr_p, g_gla_o[0], batch=bp, seq=sp, tg=256)
    gla_p = jnp.swapaxes(gla_p, -1, -2)

    attn_s = _sample_attn(q_s, ckv_s, kpe_s, cache_ckv[0], cache_kpe[0], page_table, kc, ks, g_kh[0],
                          w_uk, w_uv, t_len=ts_len, chunk_pages=32)
    o_s, gla_s = _gla_sample(gq_s, gk_s, gv_s, lg_s, gr_s, g_gla_o[0], state_gla[0], t_len=ts_len, nb=8)

    w_out = bf(w_out_ab[0])
    hp = _post(x_prompt.reshape(n_p, D_MODEL), p_prompt[0].reshape(n_p, PLE_DIM), post_w(0),
               mix=(attn_p, o_p, w_out), tm=512)
    hs = _post(x_sample.reshape(n_s, D_MODEL), p_sample[0].reshape(n_s, PLE_DIM), post_w(0),
               mix=(attn_s, o_s, w_out), tm=n_s // 2)

    w_b, w_c, lam = _s5_weights(s5_a_re[0], s5_a_im[0], s5_log_dt[0], s5_b_re[0], s5_b_im[0],
                                s5_c_re[0], s5_c_im[0])
    n_state = S5_GROUPS * S5_STATE
    zero_state = jnp.zeros((bp, n_state), F32)
    hp, s5r_p, s5i_p = _s5(_to_time_major(hp, bp, sp), zero_state, zero_state, g_norm_c[0], w_b, w_c, lam,
                           s5_d[0], w_glu[0], nb=bp, t_tile=64, batch_major_out=True)
    hs_tm, s5r_s, s5i_s = _s5(_to_time_major(hs, bs, ts_len), state_s5_re[0].reshape(bs, n_state),
                              state_s5_im[0].reshape(bs, n_state), g_norm_c[0], w_b, w_c, lam,
                              s5_d[0], w_glu[0], nb=bs, t_tile=ts_len, batch_major_out=False)
    hp = _post(hp.reshape(n_p, D_MODEL), p_prompt[1].reshape(n_p, PLE_DIM), post_w(1), tm=512)
    hs = _post(_to_batch_major(hs_tm, bs, ts_len), p_sample[1].reshape(n_s, PLE_DIM), post_w(1), tm=n_s // 2)

    st = lambda a, b: a.reshape(1, b, S5_GROUPS, S5_STATE)
    return (hp.reshape(bp, sp, D_MODEL), hs.reshape(bs, ts_len, D_MODEL),
            ckv_p.reshape(1, bp, sp, KV_LORA), kpe_p.reshape(1, bp, sp, ROPE_DIM), gla_p[None],
            st(s5r_p, bp), st(s5i_p, bp),
            ckv_s.reshape(1, bs, ts_len, KV_LORA), kpe_s.reshape(1, bs, ts_len, ROPE_DIM), gla_s[None],
            st(s5r_s, bs), st(s5i_s, bs))
```

```python
import functools

import jax
import jax.numpy as jnp
from jax import lax
from jax.experimental import pallas as pl
from jax.experimental.pallas import tpu as pltpu

F32 = jnp.float32
BF16 = jnp.bfloat16

D_MODEL = 1024
PAGE_SIZE = 128
MLA_HEADS = 8
Q_LORA = 384
KV_LORA = 256
NOPE_DIM = 64
ROPE_DIM = 32
QK_DIM = NOPE_DIM + ROPE_DIM
V_DIM = 64
ROPE_THETA = 10000.0
GLA_HEADS = 4
GLA_DK = 128
GLA_DV = 128
GLA_GATE_RANK = 16
GLA_TAU = 16.0
GLA_CHUNK = 64
S5_GROUP = 16
S5_GROUPS = D_MODEL // S5_GROUP
S5_STATE = 64
D_FF = 4 * D_MODEL
PLE_DIM = 256
EPS = 1e-6
IN_SPLITS = (Q_LORA, KV_LORA, ROPE_DIM, GLA_HEADS * GLA_DK, GLA_HEADS * GLA_DK,
             GLA_HEADS * GLA_DV, GLA_GATE_RANK, GLA_HEADS * GLA_DV)

LANES = 128
SUBLANES = 8
HEAD_PAD = LANES
VMEM_LIMIT = 56 * 1024 * 1024
S5_SLABS = D_MODEL // LANES
S5_SLAB_STATE = (LANES // S5_GROUP) * S5_STATE
NEG_INF = float("-inf")
LOG2_E = 1.4426950408889634


def _vmem_spec():
    return pl.BlockSpec(memory_space=pltpu.VMEM)


def _params(*sem):
    return pltpu.CompilerParams(dimension_semantics=sem, vmem_limit_bytes=VMEM_LIMIT)


def _bdot(a, b):
    return jnp.dot(a.astype(BF16), b.astype(BF16), preferred_element_type=F32)


def _bdot_nt(a, b):
    return lax.dot_general(a.astype(BF16), b.astype(BF16), (((1,), (1,)), ((), ())),
                           preferred_element_type=F32)


def _bdot_tn(a, b):
    return lax.dot_general(a.astype(BF16), b.astype(BF16), (((0,), (0,)), ((), ())),
                           preferred_element_type=F32)


def _rms(x, g):
    return x * lax.rsqrt(jnp.mean(x * x, axis=-1, keepdims=True) + EPS) * g


def _sigmoid(x):
    return 1.0 / (1.0 + jnp.exp(-x))


def _rope_table_kernel(cos_ref, sin_ref):
    shape = cos_ref.shape
    half = ROPE_DIM // 2
    per_row = LANES // half
    row = lax.broadcasted_iota(jnp.int32, shape, 0)
    lane = lax.broadcasted_iota(jnp.int32, shape, 1)
    pos = (row * per_row + lane // half).astype(F32)
    freq = jnp.power(F32(ROPE_THETA), -(lane % half).astype(F32) / half)
    ang = pos * freq
    cos_ref[...] = jnp.cos(ang)
    sin_ref[...] = jnp.sin(ang)


def _rope_tables(n_pos):
    half = ROPE_DIM // 2
    per_row = LANES // half
    rows = -(-n_pos // per_row)
    rows = -(-rows // SUBLANES) * SUBLANES
    cos, sin = pl.pallas_call(
        _rope_table_kernel,
        out_shape=(jax.ShapeDtypeStruct((rows, LANES), F32),) * 2,
        name="rope_table",
    )()
    cos = cos.reshape(rows * per_row, half)[:n_pos]
    sin = sin.reshape(rows * per_row, half)[:n_pos]
    return cos, sin


def _front_kernel(x_ref, tc_ref, ts_ref, gn_ref, wcq_ref, wckv_ref, wkpe_ref, wgq_ref, wgk_ref, wgv_ref,
                  wgr_ref, wga_ref, gql_ref, wuq_ref, gkvl_ref, wuk_ref, wuv_ref, gqh_ref, gkh_ref,
                  wa2_ref, ba_ref,
                  q_ref, k_ref, v_ref, ckv_ref, kpe_ref, gq_ref, gk_ref, gv_ref, lg_ref, gr_ref):
    n = _rms(x_ref[...], gn_ref[...])
    nb = n.astype(BF16)
    tc = tc_ref[...]
    ts = ts_ref[...]
    lane = lax.broadcasted_iota(jnp.int32, tc.shape, 1)
    first_half = lane < NOPE_DIM + ROPE_DIM // 2

    def norm_rope(xh, g):
        ss = jnp.sum(xh * xh, axis=-1, keepdims=True) * (1.0 / QK_DIM)
        y = xh * lax.rsqrt(ss + EPS) * g
        partner = jnp.where(first_half, pltpu.roll(y, LANES - ROPE_DIM // 2, 1), pltpu.roll(y, ROPE_DIM // 2, 1))
        return y * tc + partner * ts

    cq = _rms(jnp.dot(nb, wcq_ref[...], preferred_element_type=F32), gql_ref[...])
    q = _bdot(cq, wuq_ref[...])
    ckv = _rms(jnp.dot(nb, wckv_ref[...], preferred_element_type=F32), gkvl_ref[...])
    ckv_ref[...] = ckv
    ckvb = ckv.astype(BF16)
    kn = jnp.dot(ckvb, wuk_ref[...], preferred_element_type=F32)
    v_ref[...] = jnp.dot(ckvb, wuv_ref[...], preferred_element_type=F32).astype(v_ref.dtype)
    kpe = jnp.dot(nb, wkpe_ref[...], preferred_element_type=F32)
    kpe_ref[...] = kpe
    gqh = gqh_ref[...]
    gkh = gkh_ref[...]
    scale = QK_DIM ** -0.5 * LOG2_E
    for h in range(MLA_HEADS):
        hs = slice(h * HEAD_PAD, (h + 1) * HEAD_PAD)
        q_ref[:, hs] = (norm_rope(q[:, hs], gqh) * scale).astype(q_ref.dtype)
        k_ref[:, hs] = norm_rope(kn[:, hs] + kpe, gkh).astype(k_ref.dtype)

    gq_ref[...] = jnp.dot(nb, wgq_ref[...], preferred_element_type=F32) * (GLA_DK ** -0.5)
    gk_ref[...] = jnp.dot(nb, wgk_ref[...], preferred_element_type=F32)
    gv_ref[...] = jnp.dot(nb, wgv_ref[...], preferred_element_type=F32)
    gr_ref[...] = jnp.dot(nb, wgr_ref[...], preferred_element_type=F32)
    ga = jnp.dot(nb, wga_ref[...], preferred_element_type=F32)
    logit = _bdot(ga, wa2_ref[...]) + ba_ref[...]
    log_sig = jnp.minimum(logit, 0.0) - jnp.log1p(jnp.exp(-jnp.abs(logit)))
    lg_ref[...] = log_sig * (1.0 / GLA_TAU)


def _front_weights(w_in, g_norm, g_qlat, w_uq, g_kvlat, w_ukv, g_qh, g_kh, w_a2, b_a):
    offs = [0]
    for s in IN_SPLITS:
        offs.append(offs[-1] + s)
    cols = [w_in[:, offs[i]:offs[i + 1]] for i in range(len(IN_SPLITS))]
    w_cq, w_ckv, w_kpe, w_gq, w_gk, w_gv, w_ga, w_gr = cols
    pad_head = HEAD_PAD - QK_DIM
    w_kpe_placed = jnp.pad(w_kpe, ((0, 0), (NOPE_DIM, HEAD_PAD - QK_DIM)))
    w_ga_pad = jnp.pad(w_ga, ((0, 0), (0, LANES - GLA_GATE_RANK)))
    w_a2_pad = jnp.pad(w_a2, ((0, LANES - GLA_GATE_RANK), (0, 0)))
    w_uq_pad = jnp.pad(w_uq.reshape(Q_LORA, MLA_HEADS, QK_DIM), ((0, 0), (0, 0), (0, pad_head)))
    w_uq_pad = w_uq_pad.reshape(Q_LORA, MLA_HEADS * HEAD_PAD)
    w_ukv3 = w_ukv.reshape(KV_LORA, MLA_HEADS, NOPE_DIM + V_DIM)
    w_uk = w_ukv3[:, :, :NOPE_DIM]
    w_uv = w_ukv3[:, :, NOPE_DIM:].reshape(KV_LORA, MLA_HEADS * V_DIM)
    w_uk_pad = jnp.pad(w_uk, ((0, 0), (0, 0), (0, HEAD_PAD - NOPE_DIM))).reshape(KV_LORA, MLA_HEADS * HEAD_PAD)
    g_qh_pad = jnp.pad(g_qh, (0, pad_head))[None, :]
    g_kh_pad = jnp.pad(g_kh, (0, pad_head))[None, :]
    bf = lambda a: a.astype(BF16)
    weights = (g_norm[None, :], bf(w_cq), bf(w_ckv), bf(w_kpe_placed), bf(w_gq), bf(w_gk), bf(w_gv), bf(w_gr),
               bf(w_ga_pad), g_qlat[None, :], bf(w_uq_pad), g_kvlat[None, :], bf(w_uk_pad), bf(w_uv),
               g_qh_pad, g_kh_pad, bf(w_a2_pad), b_a[None, :])
    return weights, w_uk, w_uv


def _front(x, tc, ts, weights, *, tm, table_tiles):
    n = x.shape[0]
    hk = GLA_HEADS * GLA_DK
    row = lambda w: pl.BlockSpec((tm, w), lambda i: (i, 0))
    tab = pl.BlockSpec((tm, LANES), lambda i: (i % table_tiles, 0))
    sds = lambda w, dt: jax.ShapeDtypeStruct((n, w), dt)
    out_shape = (sds(MLA_HEADS * HEAD_PAD, BF16), sds(MLA_HEADS * HEAD_PAD, BF16), sds(MLA_HEADS * V_DIM, BF16),
                 sds(KV_LORA, F32), sds(LANES, F32), sds(hk, F32), sds(hk, F32), sds(hk, F32), sds(hk, F32),
                 sds(hk, F32))
    out_specs = (row(MLA_HEADS * HEAD_PAD), row(MLA_HEADS * HEAD_PAD), row(MLA_HEADS * V_DIM), row(KV_LORA),
                 row(LANES), row(hk), row(hk), row(hk), row(hk), row(hk))
    return pl.pallas_call(
        _front_kernel,
        grid=(n // tm,),
        in_specs=[row(D_MODEL), tab, tab] + [_vmem_spec()] * len(weights),
        out_specs=out_specs,
        out_shape=out_shape,
        compiler_params=_params("parallel"),
        name="ab_front",
    )(x, tc, ts, *weights)


def _prompt_attn_kernel(q_ref, k_ref, v_ref, o_ref, *, tq, n_q):
    i = pl.program_id(1)
    row = lax.broadcasted_iota(jnp.int32, (tq, tq), 0)
    col = lax.broadcasted_iota(jnp.int32, (tq, tq), 1)
    causal = col <= row
    lane = lax.broadcasted_iota(jnp.int32, (tq, LANES), 1)

    def case(c):
        kv = (c + 1) * tq
        outs = []
        for h in range(MLA_HEADS):
            hs = slice(h * HEAD_PAD, (h + 1) * HEAD_PAD)
            vs = slice((h // 2) * LANES, (h // 2 + 1) * LANES)
            s = lax.dot_general(q_ref[:, hs], k_ref[:kv, hs], (((1,), (1,)), ((), ())),
                                preferred_element_type=F32)
            diag = jnp.where(causal, s[:, c * tq:], NEG_INF)
            s = diag if c == 0 else jnp.concatenate([s[:, :c * tq], diag], axis=1)
            p = jnp.exp2(s - jnp.max(s, axis=-1, keepdims=True))
            l = jnp.sum(p, axis=-1, keepdims=True)
            outs.append(jnp.dot(p.astype(BF16), v_ref[:kv, vs], preferred_element_type=F32) / l)
        for pair in range(MLA_HEADS // 2):
            o_ref[:, pair * LANES:(pair + 1) * LANES] = jnp.where(
                lane < V_DIM, outs[2 * pair], outs[2 * pair + 1]).astype(o_ref.dtype)

    for c in range(n_q):
        pl.when(i == c)(functools.partial(case, c))


def _prompt_attn(q, k, v, *, batch, seq, tq):
    nq = seq // tq
    return pl.pallas_call(
        functools.partial(_prompt_attn_kernel, tq=tq, n_q=nq),
        grid=(batch, nq),
        in_specs=[pl.BlockSpec((tq, MLA_HEADS * HEAD_PAD), lambda b, i: (b * nq + i, 0)),
                  pl.BlockSpec((seq, MLA_HEADS * HEAD_PAD), lambda b, i: (b, 0)),
                  pl.BlockSpec((seq, MLA_HEADS * V_DIM), lambda b, i: (b, 0))],
        out_specs=pl.BlockSpec((tq, MLA_HEADS * V_DIM), lambda b, i: (b * nq + i, 0)),
        out_shape=jax.ShapeDtypeStruct((batch * seq, MLA_HEADS * V_DIM), BF16),
        compiler_params=_params("parallel", "arbitrary"),
        name="mla_prompt_attn",
    )(q, k, v)


def _gla_intra(q, k, b):
    c = q.shape[0]
    row = lax.broadcasted_iota(jnp.int32, q.shape, 0)
    r2 = lax.broadcasted_iota(jnp.int32, (c, c), 0)
    c2 = lax.broadcasted_iota(jnp.int32, (c, c), 1)
    att = jnp.where(r2 == c2, _bdot_nt(q, k), 0.0)
    before = pltpu.roll(b, 1, 0)
    last = b
    m = 1
    while m < c:
        upper = (row // m) % 2 == 1
        qm = jnp.where(upper, q * jnp.exp(b - before), 0.0)
        km = jnp.where(upper, 0.0, k * jnp.exp(last - b))
        same = (r2 // (2 * m)) == (c2 // (2 * m))
        att = att + jnp.where(same, _bdot_nt(qm, km), 0.0)
        if 2 * m < c:
            before = jnp.where(upper, pltpu.roll(before, m, 0), before)
            last = jnp.where(upper, last, pltpu.roll(last, c - m, 0))
        m *= 2
    return att


def _cumsum_rows(g):
    c = g.shape[0]
    row = lax.broadcasted_iota(jnp.int32, g.shape, 0)
    b = g
    sh = 1
    while sh < c:
        b = b + jnp.where(row >= sh, pltpu.roll(b, sh, 0), 0.0)
        sh *= 2
    return b


def _gla_prompt_kernel(gq_ref, gk_ref, gv_ref, lg_ref, gr_ref, go_ref, o_ref, st_ref, state_ref, *, n_chunks):
    t = pl.program_id(1)

    @pl.when(t == 0)
    def _():
        state_ref[...] = jnp.zeros_like(state_ref)

    go = go_ref[...]

    for h in range(GLA_HEADS):
        hs = slice(h * GLA_DK, (h + 1) * GLA_DK)
        st = state_ref[h]
        for ci in range(n_chunks):
            rows = slice(ci * GLA_CHUNK, (ci + 1) * GLA_CHUNK)
            q = gq_ref[rows, hs]
            k = gk_ref[rows, hs]
            v = gv_ref[rows, hs]
            b = _cumsum_rows(lg_ref[rows, hs])
            o = _bdot_nt(q * jnp.exp(b), st) + _bdot(_gla_intra(q, k, b), v)
            bl = b[GLA_CHUNK - 1:GLA_CHUNK, :]
            st = st * jnp.exp(bl) + _bdot_tn(v, k * jnp.exp(bl - b))
            gr = gr_ref[rows, hs]
            o_ref[rows, hs] = (_rms(o, go) * (gr * _sigmoid(gr))).astype(o_ref.dtype)
        state_ref[h] = st

    @pl.when(t == pl.num_programs(1) - 1)
    def _():
        st_ref[0] = state_ref[...]


def _gla_prompt(gq, gk, gv, lg, gr, g_o, *, batch, seq, tg):
    nt = seq // tg
    hk = GLA_HEADS * GLA_DK
    row = pl.BlockSpec((tg, hk), lambda b, t: (b * nt + t, 0))
    return pl.pallas_call(
        functools.partial(_gla_prompt_kernel, n_chunks=tg // GLA_CHUNK),
        grid=(batch, nt),
        in_specs=[row, row, row, row, row, _vmem_spec()],
        out_specs=(row, pl.BlockSpec((1, GLA_HEADS, GLA_DV, GLA_DK), lambda b, t: (b, 0, 0, 0))),
        out_shape=(jax.ShapeDtypeStruct((batch * seq, hk), BF16),
                   jax.ShapeDtypeStruct((batch, GLA_HEADS, GLA_DV, GLA_DK), F32)),
        scratch_shapes=[pltpu.VMEM((GLA_HEADS, GLA_DV, GLA_DK), F32)],
        compiler_params=_params("parallel", "arbitrary"),
        name="gla_prompt",
    )(gq, gk, gv, lg, gr, g_o[None, :])


def _gla_sample_kernel(gq_ref, gk_ref, gv_ref, lg_ref, gr_ref, go_ref, s0_ref, o_ref, s1_ref, *, nb, t_len):
    rows = nb * t_len
    per_tile = SUBLANES // t_len
    tok = lax.broadcasted_iota(jnp.int32, (rows, GLA_DK), 0) % t_len
    tile_batch = lax.broadcasted_iota(jnp.int32, (SUBLANES, GLA_DK), 0) // t_len
    pad = jnp.zeros((LANES - rows, LANES), F32)
    go = go_ref[...]
    for h in range(GLA_HEADS):
        hs = slice(h * GLA_DK, (h + 1) * GLA_DK)
        q = gq_ref[:, hs]
        k = gk_ref[:, hs]
        v = gv_ref[:, hs]
        b = lg_ref[:, hs]
        sh = 1
        while sh < t_len:
            b = b + jnp.where(tok >= sh, pltpu.roll(b, sh, 0), 0.0)
            sh *= 2
        bl = jnp.where(tok == t_len - 1, b, 0.0)
        for d in range(1, t_len):
            bl = bl + jnp.where(tok == t_len - 1 - d, pltpu.roll(b, rows - d, 0), 0.0)
        qe = q * jnp.exp(b)
        kd = k * jnp.exp(bl - b)
        o = jnp.sum(q * k, axis=-1, keepdims=True) * v
        for d in range(1, t_len):
            w = jnp.sum(q * pltpu.roll(k, d, 0) * jnp.exp(b - pltpu.roll(b, d, 0)), axis=-1, keepdims=True)
            o = o + jnp.where(tok[:, :1] >= d, w, 0.0) * pltpu.roll(v, d, 0)
        decay_cols = jnp.transpose(jnp.concatenate([jnp.exp(bl), pad], axis=0))
        inter = []
        for tile in range(rows // SUBLANES):
            ts = slice(tile * SUBLANES, (tile + 1) * SUBLANES)
            acc = jnp.zeros((SUBLANES, GLA_DV), F32)
            for e in range(per_tile):
                bi = tile * per_tile + e
                s0 = s0_ref[bi, h]
                mine = tile_batch == e
                acc = jnp.where(mine, _bdot(qe[ts], s0), acc)
                r_last = bi * t_len + t_len - 1
                s1_ref[bi, h] = s0 * decay_cols[:, r_last:r_last + 1] + _bdot_tn(jnp.where(mine, kd[ts], 0.0), v[ts])
            inter.append(acc)
        o = o + jnp.concatenate(inter, axis=0)
        gr = gr_ref[:, hs]
        o_ref[:, hs] = (_rms(o, go) * (gr * _sigmoid(gr))).astype(o_ref.dtype)


def _gla_sample(gq, gk, gv, lg, gr, g_o, s0, *, t_len, nb):
    n = gq.shape[0]
    batch = n // t_len
    hk = GLA_HEADS * GLA_DK
    rows = nb * t_len
    row = pl.BlockSpec((rows, hk), lambda i: (i, 0))
    st = pl.BlockSpec((nb, GLA_HEADS, GLA_DK, GLA_DV), lambda i: (i, 0, 0, 0))
    return pl.pallas_call(
        functools.partial(_gla_sample_kernel, nb=nb, t_len=t_len),
        grid=(batch // nb,),
        in_specs=[row, row, row, row, row, _vmem_spec(), st],
        out_specs=(row, st),
        out_shape=(jax.ShapeDtypeStruct((n, hk), BF16), jax.ShapeDtypeStruct(s0.shape, s0.dtype)),
        compiler_params=_params("parallel"),
        name="gla_sample",
    )(gq, gk, gv, lg, gr, g_o[None, :], s0)


def _sample_attn_kernel(pt_ref, q_ref, cnew_ref, pnew_ref, kc_ref, ks_ref, kcn_ref, ksn_ref, gk_ref, gpe_ref,
                        wukt_ref, wukp_ref, wuv_ref, ckv_hbm, kpe_hbm, o_ref, ckv_buf, kpe_buf, sem,
                        *, n_pages, chunk_pages, t_len):
    n_cols = t_len * MLA_HEADS
    chunk = chunk_pages * PAGE_SIZE
    n_chunks = n_pages // chunk_pages
    past = n_pages * PAGE_SIZE
    b = pl.program_id(0)
    slot = b % 2

    def page_copies(batch, sl, i):
        page = pt_ref[batch, i]
        ci = i // chunk_pages
        return (pltpu.make_async_copy(ckv_hbm.at[page], ckv_buf.at[sl, i], sem.at[sl, ci, 0]),
                pltpu.make_async_copy(kpe_hbm.at[page], kpe_buf.at[sl, i], sem.at[sl, ci, 1]))

    def start_pages(batch, sl, pages):
        for i in pages:
            for cp in page_copies(batch, sl, i):
                cp.start()

    def wait_pages(batch, sl, pages):
        for i in pages:
            for cp in page_copies(batch, sl, i):
                cp.wait()

    @pl.when(b == 0)
    def _():
        start_pages(0, 0, range(n_pages))

    qg = q_ref[...].astype(F32) * gk_ref[...]
    gpe = gpe_ref[...]
    sub = lax.broadcasted_iota(jnp.int32, (SUBLANES, MLA_HEADS * HEAD_PAD), 0)
    lane_head = lax.broadcasted_iota(jnp.int32, (SUBLANES, MLA_HEADS * HEAD_PAD), 1) // HEAD_PAD
    qexp = jnp.concatenate(
        [jnp.where(lane_head == sub, jnp.broadcast_to(qg[t:t + 1, :], sub.shape), 0.0) for t in range(t_len)],
        axis=0)
    qt = _bdot(qexp, wukt_ref[...])
    lhs = jnp.concatenate([wukp_ref[...], qt[:, :KV_LORA].astype(BF16)], axis=0)
    qr = qt[:, KV_LORA:KV_LORA + ROPE_DIM].astype(BF16)
    n_norm = NOPE_DIM * MLA_HEADS
    half = ROPE_DIM // 2

    def scores(c, kt, kcos, ksin):
        n = c.shape[0]
        cb = c.astype(BF16)
        r = lax.dot_general(lhs, cb, (((1,), (1,)), ((), ())), preferred_element_type=F32)
        kn = r[:n_norm].reshape(NOPE_DIM, MLA_HEADS, n)
        ss = jnp.sum(kn * kn, axis=0) + jnp.sum(kt * kt, axis=0, keepdims=True)
        rs = lax.rsqrt(ss * (1.0 / QK_DIM) + EPS)
        kg = kt * gpe
        kr = kg * kcos + jnp.concatenate([kg[half:], kg[:half]], axis=0) * ksin
        s = r[n_norm:] + jnp.dot(qr, kr.astype(BF16), preferred_element_type=F32)
        s = (s.reshape(t_len, MLA_HEADS, n) * rs[None]).reshape(n_cols, n)
        return cb, s

    def fold(state, cb, s):
        m, l, acc = state
        m_new = jnp.maximum(m, jnp.max(s, axis=1, keepdims=True))
        alpha = jnp.exp2(m - m_new)
        p = jnp.exp2(s - m_new)
        l = alpha * l + jnp.sum(p, axis=1, keepdims=True)
        acc = alpha * acc + jnp.dot(p.astype(BF16), cb, preferred_element_type=F32)
        return m_new, l, acc

    @pl.when(b + 1 < pl.num_programs(0))
    def _():
        start_pages(b + 1, 1 - slot, range(n_pages))

    wait_pages(b, slot, range(n_pages))
    state = (jnp.full((n_cols, 1), NEG_INF, F32), jnp.zeros((n_cols, 1), F32), jnp.zeros((n_cols, KV_LORA), F32))
    for ci in range(n_chunks):
        pages = range(ci * chunk_pages, (ci + 1) * chunk_pages)
        c = ckv_buf[slot, pl.ds(ci * chunk_pages, chunk_pages)].reshape(chunk, KV_LORA)
        kt = jnp.concatenate([kpe_buf[slot, p] for p in pages], axis=1)
        keys = slice(ci * chunk, (ci + 1) * chunk)
        state = fold(state, *scores(c, kt, kc_ref[:, keys], ks_ref[:, keys]))

    zpad = lambda a: jnp.concatenate([a, jnp.zeros((PAGE_SIZE - t_len, a.shape[1]), a.dtype)], axis=0)
    pnew = jnp.concatenate([zpad(pnew_ref[...]), jnp.zeros((PAGE_SIZE, LANES - ROPE_DIM), F32)], axis=1)
    cb, s = scores(zpad(cnew_ref[...]), jnp.transpose(pnew)[:ROPE_DIM], kcn_ref[...], ksn_ref[...])
    r2 = lax.broadcasted_iota(jnp.int32, (n_cols, PAGE_SIZE), 0)
    c2 = lax.broadcasted_iota(jnp.int32, (n_cols, PAGE_SIZE), 1)
    s = jnp.where((c2 < t_len) & (c2 <= r2 // MLA_HEADS), s, NEG_INF)
    _, l, acc = fold(state, cb, s)

    lat = acc / l
    res = _bdot(lat, wuv_ref[...])
    r3 = lax.broadcasted_iota(jnp.int32, res.shape, 0)
    c3 = lax.broadcasted_iota(jnp.int32, res.shape, 1)
    res = jnp.where(c3 // V_DIM == r3 % MLA_HEADS, res, 0.0)
    o_ref[...] = jnp.sum(res.reshape(t_len, MLA_HEADS, MLA_HEADS * V_DIM), axis=1).astype(o_ref.dtype)


def _sample_attn(q, ckv_new, kpe_new, cache_ckv, cache_kpe, page_table, kc, ks, g_kh, w_uk, w_uv,
                 *, t_len, chunk_pages):
    batch, n_pages = page_table.shape
    past = n_pages * PAGE_SIZE
    w_ukt = jnp.pad(jnp.transpose(w_uk, (1, 2, 0)), ((0, 0), (0, HEAD_PAD - NOPE_DIM), (0, 0)))
    w_ukt = w_ukt.reshape(MLA_HEADS * HEAD_PAD, KV_LORA)
    sel = jnp.pad(jnp.eye(ROPE_DIM, dtype=F32), ((NOPE_DIM, HEAD_PAD - QK_DIM), (0, LANES - ROPE_DIM)))
    w_ukt_ext = jnp.concatenate([w_ukt, jnp.tile(sel, (MLA_HEADS, 1))], axis=1).astype(BF16)
    w_uk_perm = jnp.transpose(w_uk, (2, 1, 0)).reshape(NOPE_DIM * MLA_HEADS, KV_LORA).astype(BF16)
    g_k_pad = jnp.concatenate([g_kh[:NOPE_DIM], jnp.ones((ROPE_DIM,), F32), jnp.zeros((HEAD_PAD - QK_DIM,), F32)])
    g_k_pad = jnp.tile(g_k_pad, MLA_HEADS)[None, :]
    g_pe = g_kh[NOPE_DIM:][:, None]
    kc, ks = jnp.transpose(kc), jnp.transpose(ks)
    kcn = jnp.pad(kc[:, past:past + t_len], ((0, 0), (0, PAGE_SIZE - t_len)))
    ksn = jnp.pad(ks[:, past:past + t_len], ((0, 0), (0, PAGE_SIZE - t_len)))
    per_b = lambda w: pl.BlockSpec((None, t_len, w), lambda b, pt: (b, 0, 0))
    hbm = pl.BlockSpec(memory_space=pl.ANY)
    in_specs = ([per_b(MLA_HEADS * HEAD_PAD), per_b(KV_LORA), per_b(ROPE_DIM)] + [_vmem_spec()] * 9 + [hbm, hbm])
    grid_spec = pltpu.PrefetchScalarGridSpec(
        num_scalar_prefetch=1,
        grid=(batch,),
        in_specs=in_specs,
        out_specs=pl.BlockSpec((None, t_len, MLA_HEADS * V_DIM), lambda b, pt: (b, 0, 0)),
        scratch_shapes=[pltpu.VMEM((2, n_pages, PAGE_SIZE, KV_LORA), F32),
                        pltpu.VMEM((2, n_pages, ROPE_DIM, PAGE_SIZE), F32),
                        pltpu.SemaphoreType.DMA((2, n_pages // chunk_pages, 2))],
    )
    out = pl.pallas_call(
        functools.partial(_sample_attn_kernel, n_pages=n_pages, chunk_pages=chunk_pages, t_len=t_len),
        grid_spec=grid_spec,
        out_shape=jax.ShapeDtypeStruct((batch, t_len, MLA_HEADS * V_DIM), BF16),
        compiler_params=_params("arbitrary"),
        name="mla_sample_attn",
    )(page_table, q.reshape(batch, t_len, -1), ckv_new.reshape(batch, t_len, -1),
      kpe_new.reshape(batch, t_len, -1), kc[:, :past], ks[:, :past], kcn, ksn, g_k_pad, g_pe,
      w_ukt_ext, w_uk_perm, w_uv.astype(BF16), cache_ckv, jnp.swapaxes(cache_kpe, 1, 2))
    return out.reshape(batch * t_len, MLA_HEADS * V_DIM)


def _mlp_ple(h, p_ref, gm_ref, wup_ref, wdown_ref, gp_ref, wgate_ref, wproj_ref, o_ref, ff_chunk):
    nb = _rms(h, gm_ref[...]).astype(BF16)
    y = jnp.zeros_like(h)
    for c in range(D_FF // ff_chunk):
        cs = slice(c * ff_chunk, (c + 1) * ff_chunk)
        a = jnp.maximum(jnp.dot(nb, wup_ref[:, cs], preferred_element_type=F32), 0.0)
        y = y + jnp.dot((a * a).astype(BF16), wdown_ref[cs, :], preferred_element_type=F32)
    h = h + y
    gate = _sigmoid(_bdot(_rms(h, gp_ref[...]), wgate_ref[...]))
    o_ref[...] = h + gate * _bdot(p_ref[...], wproj_ref[...])


def _post_mix_kernel(h_ref, a_ref, o2_ref, p_ref, wout_ref, gm_ref, wup_ref, wdown_ref, gp_ref, wgate_ref,
                     wproj_ref, o_ref, *, ff_chunk):
    n_attn = MLA_HEADS * V_DIM
    h = (h_ref[...] + jnp.dot(a_ref[...], wout_ref[:n_attn, :], preferred_element_type=F32)
         + jnp.dot(o2_ref[...], wout_ref[n_attn:, :], preferred_element_type=F32))
    _mlp_ple(h, p_ref, gm_ref, wup_ref, wdown_ref, gp_ref, wgate_ref, wproj_ref, o_ref, ff_chunk)


def _post_kernel(h_ref, p_ref, gm_ref, wup_ref, wdown_ref, gp_ref, wgate_ref, wproj_ref, o_ref, *, ff_chunk):
    _mlp_ple(h_ref[...], p_ref, gm_ref, wup_ref, wdown_ref, gp_ref, wgate_ref, wproj_ref, o_ref, ff_chunk)


def _post(h, p, weights, mix=None, *, tm, ff_chunk=1024):
    n = h.shape[0]
    row = lambda w: pl.BlockSpec((tm, w), lambda i: (i, 0))
    if mix is None:
        body, acts, specs = _post_kernel, (h, p), [row(D_MODEL), row(PLE_DIM)]
    else:
        attn, o_gla, w_out = mix
        body, acts = _post_mix_kernel, (h, attn, o_gla, p)
        specs = [row(D_MODEL), row(attn.shape[1]), row(o_gla.shape[1]), row(PLE_DIM)]
        weights = (w_out,) + tuple(weights)
    return pl.pallas_call(
        functools.partial(body, ff_chunk=ff_chunk),
        grid=(n // tm,),
        in_specs=specs + [_vmem_spec()] * len(weights),
        out_specs=row(D_MODEL),
        out_shape=jax.ShapeDtypeStruct((n, D_MODEL), F32),
        compiler_params=_params("parallel"),
        name="post_mix" if mix is not None else "post",
    )(*acts, *weights)


def _s5_prep_kernel(are_ref, aim_ref, ldt_ref, bre_ref, bim_ref, lr_ref, li_ref, bbr_ref, bbi_ref):
    a_re = are_ref[...]
    a_im = aim_ref[...]
    dt = jnp.exp(ldt_ref[...])
    mag = jnp.exp(a_re * dt)
    lr = mag * jnp.cos(a_im * dt)
    li = mag * jnp.sin(a_im * dt)
    den = a_re * a_re + a_im * a_im
    cr = ((lr - 1.0) * a_re + li * a_im) / den
    ci = (li * a_re - (lr - 1.0) * a_im) / den
    lr_ref[...] = lr
    li_ref[...] = li
    b_re = bre_ref[...]
    b_im = bim_ref[...]
    bbr_ref[...] = cr[:, None, :] * b_re - ci[:, None, :] * b_im
    bbi_ref[...] = cr[:, None, :] * b_im + ci[:, None, :] * b_re


def _s5_weights(a_re, a_im, log_dt, b_re, b_im, c_re, c_im):
    g, p_, ch = S5_GROUPS, S5_STATE, S5_GROUP
    gp = jax.ShapeDtypeStruct((g, p_), F32)
    gcp = jax.ShapeDtypeStruct((g, ch, p_), F32)
    lr, li, bbr, bbi = pl.pallas_call(
        _s5_prep_kernel, out_shape=(gp, gp, gcp, gcp), name="s5_prep",
    )(a_re, a_im, log_dt[:, None], jnp.transpose(b_re, (0, 2, 1)), jnp.transpose(b_im, (0, 2, 1)))
    per_slab = LANES // ch
    eye = jnp.eye(per_slab, dtype=F32)

    def b_blocks(bb):
        bb = bb.reshape(S5_SLABS, per_slab, ch, p_)
        return jnp.einsum('jgcp,gh->jgchp', bb, eye).reshape(S5_SLABS, LANES, S5_SLAB_STATE)

    def c_blocks(cc):
        cc = cc.reshape(S5_SLABS, per_slab, ch, p_)
        return jnp.einsum('jgcp,gh->jgphc', cc, eye).reshape(S5_SLABS, S5_SLAB_STATE, LANES)

    w_b = jnp.concatenate([b_blocks(bbr), b_blocks(bbi)], axis=2).astype(BF16)
    w_c = jnp.concatenate([c_blocks(c_re), -c_blocks(c_im)], axis=1).astype(BF16)
    lam = jnp.stack([lr.reshape(S5_SLABS, S5_SLAB_STATE), li.reshape(S5_SLABS, S5_SLAB_STATE)], axis=1)
    return w_b, w_c, lam


def _s5_kernel(x_ref, x0r_ref, x0i_ref, gn_ref, wb_ref, wc_ref, lam_ref, d_ref, wglu_ref,
               o_ref, xr_ref, xi_ref, bur_ref, bui_ref, sr_ref, si_ref, shuf_ref=None, *, nb, t_len):
    step = pl.program_id(0)
    ns = S5_SLAB_STATE

    @pl.when(step == 0)
    def _():
        for j in range(S5_SLABS):
            sr_ref[j] = x0r_ref[:, j * ns:(j + 1) * ns]
            si_ref[j] = x0i_ref[:, j * ns:(j + 1) * ns]

    if shuf_ref is None:
        x = x_ref[...]
    else:
        for bi in range(nb):
            for s in range(D_MODEL // LANES):
                shuf_ref[s, pl.ds(bi, t_len, stride=nb), :] = x_ref[bi, :, s * LANES:(s + 1) * LANES]
        x = jnp.concatenate([shuf_ref[s] for s in range(D_MODEL // LANES)], axis=1)
    u = _rms(x, gn_ref[...])
    ub = u.astype(BF16)
    ys = []
    for j in range(S5_SLABS):
        bu = jnp.dot(ub[:, j * LANES:(j + 1) * LANES], wb_ref[j], preferred_element_type=F32)
        bur_ref[j] = bu[:, :ns]
        bui_ref[j] = bu[:, ns:]
        lr = jnp.broadcast_to(lam_ref[j, 0:1, :], (SUBLANES, ns))
        li = jnp.broadcast_to(lam_ref[j, 1:2, :], (SUBLANES, ns))
        for gi in range(nb // SUBLANES):
            srow = slice(gi * SUBLANES, (gi + 1) * SUBLANES)
            xr = sr_ref[j, srow, :]
            xi = si_ref[j, srow, :]
            for t in range(t_len):
                rows = slice(t * nb + gi * SUBLANES, t * nb + (gi + 1) * SUBLANES)
                xr, xi = (lr * xr - li * xi + bur_ref[j, rows, :], lr * xi + li * xr + bui_ref[j, rows, :])
                bur_ref[j, rows, :] = xr
                bui_ref[j, rows, :] = xi
            sr_ref[j, srow, :] = xr
            si_ref[j, srow, :] = xi
        xs = jnp.concatenate([bur_ref[j].astype(BF16), bui_ref[j].astype(BF16)], axis=1)
        ys.append(jnp.dot(xs, wc_ref[j], preferred_element_type=F32))
    y = jnp.concatenate(ys, axis=1) + d_ref[...] * u
    gl = jax.nn.gelu(y)
    zg = _bdot(gl, wglu_ref[...])
    h = x + zg[:, :D_MODEL] * _sigmoid(zg[:, D_MODEL:])
    if shuf_ref is None:
        o_ref[...] = h
    else:
        for s in range(D_MODEL // LANES):
            shuf_ref[s] = h[:, s * LANES:(s + 1) * LANES]
        for bi in range(nb):
            for s in range(D_MODEL // LANES):
                o_ref[bi, :, s * LANES:(s + 1) * LANES] = shuf_ref[s, pl.ds(bi, t_len, stride=nb), :]

    @pl.when(step == pl.num_programs(0) - 1)
    def _():
        for j in range(S5_SLABS):
            xr_ref[:, j * ns:(j + 1) * ns] = sr_ref[j]
            xi_ref[:, j * ns:(j + 1) * ns] = si_ref[j]


def _s5(x, x0r, x0i, g_norm, w_b, w_c, lam, d, w_glu, *, nb, t_tile, batch_major):
    n = x.shape[0] * x.shape[1] if batch_major else x.shape[0]
    rows = nb * t_tile
    n_state = S5_GROUPS * S5_STATE
    full = lambda shape: pl.BlockSpec(shape, lambda i: (0,) * len(shape))
    st = jax.ShapeDtypeStruct((nb, n_state), F32)
    scratch = [pltpu.VMEM((S5_SLABS, rows, S5_SLAB_STATE), F32),
               pltpu.VMEM((S5_SLABS, rows, S5_SLAB_STATE), F32),
               pltpu.VMEM((S5_SLABS, nb, S5_SLAB_STATE), F32),
               pltpu.VMEM((S5_SLABS, nb, S5_SLAB_STATE), F32)]
    if batch_major:
        x_spec = out_spec = pl.BlockSpec((nb, t_tile, D_MODEL), lambda i: (0, i, 0))
        out_sds = jax.ShapeDtypeStruct((nb, n // nb, D_MODEL), F32)
        scratch.append(pltpu.VMEM((D_MODEL // LANES, rows, LANES), F32))
    else:
        x_spec = out_spec = pl.BlockSpec((rows, D_MODEL), lambda i: (i, 0))
        out_sds = jax.ShapeDtypeStruct((n, D_MODEL), F32)
    return pl.pallas_call(
        functools.partial(_s5_kernel, nb=nb, t_len=t_tile),
        grid=(n // rows,),
        in_specs=[x_spec, full((nb, n_state)), full((nb, n_state))] + [_vmem_spec()] * 6,
        out_specs=(out_spec, full((nb, n_state)), full((nb, n_state))),
        out_shape=(out_sds, st, st),
        scratch_shapes=scratch,
        compiler_params=_params("arbitrary"),
        name="s5_layer",
    )(x, x0r, x0i, g_norm[None, :], w_b, w_c, lam, d[None, :], w_glu.astype(BF16))


def _to_time_major(h, batch, seq):
    return h.reshape(batch, seq, -1).transpose(1, 0, 2).reshape(batch * seq, -1)


def _to_batch_major(h, batch, seq):
    return h.reshape(seq, batch, -1).transpose(1, 0, 2).reshape(batch * seq, -1)


def kernel(x_prompt, x_sample, cache_ckv, cache_kpe, state_gla, state_s5_re, state_s5_im, page_table,
           p_prompt, p_sample, g_norm_ab, w_in_ab, g_qlat, w_uq, g_kvlat, w_ukv, g_qh, g_kh,
           w_gla_a2, b_gla_a, g_gla_o, w_out_ab, g_norm_c, s5_a_re, s5_a_im, s5_log_dt,
           s5_b_re, s5_b_im, s5_c_re, s5_c_im, s5_d, w_glu, g_norm_mlp, w_up, w_down,
           g_norm_ple, w_ple_gate, w_ple_proj):
    bp, sp, _ = x_prompt.shape
    bs, ts_len, _ = x_sample.shape
    n_pages = page_table.shape[1]
    past = n_pages * PAGE_SIZE
    n_p, n_s = bp * sp, bs * ts_len
    bf = lambda a: a.astype(BF16)

    cos16, sin16 = _rope_tables(past + ts_len)
    ones = jnp.ones((cos16.shape[0], 1), F32)
    tc = jnp.concatenate([ones * jnp.ones((1, NOPE_DIM), F32), cos16, cos16,
                          ones * jnp.ones((1, HEAD_PAD - QK_DIM), F32)], axis=1)
    ts = jnp.concatenate([ones * jnp.zeros((1, NOPE_DIM), F32), -sin16, sin16,
                          ones * jnp.zeros((1, HEAD_PAD - QK_DIM), F32)], axis=1)
    kc = jnp.concatenate([cos16, cos16], axis=1)
    ks = jnp.concatenate([-sin16, sin16], axis=1)
    tc_s = jnp.tile(tc[past:past + ts_len], (bs, 1))
    ts_s = jnp.tile(ts[past:past + ts_len], (bs, 1))

    post_w = lambda i: (g_norm_mlp[i][None, :], bf(w_up[i]), bf(w_down[i]), g_norm_ple[i][None, :],
                        bf(w_ple_gate[i]), bf(w_ple_proj[i]))

    fw, w_uk, w_uv = _front_weights(w_in_ab[0], g_norm_ab[0], g_qlat[0], w_uq[0], g_kvlat[0], w_ukv[0],
                                    g_qh[0], g_kh[0], w_gla_a2[0], b_gla_a[0])
    tm_p = 512
    (q_p, k_p, v_p, ckv_p, kpe_p, gq_p, gk_p, gv_p, lg_p, gr_p) = _front(
        x_prompt.reshape(n_p, D_MODEL), tc, ts, fw, tm=tm_p, table_tiles=sp // tm_p)
    (q_s, _, _, ckv_s, kpe_s, gq_s, gk_s, gv_s, lg_s, gr_s) = _front(
        x_sample.reshape(n_s, D_MODEL), tc_s, ts_s, fw, tm=n_s // 2, table_tiles=2)
    kpe_p = kpe_p[:, NOPE_DIM:QK_DIM]
    kpe_s = kpe_s[:, NOPE_DIM:QK_DIM]

    attn_p = _prompt_attn(q_p, k_p, v_p, batch=bp, seq=sp, tq=512)
    o_p, gla_p = _gla_prompt(gq_p, gk_p, gv_p, lg_p, gr_p, g_gla_o[0], batch=bp, seq=sp, tg=1024)
    gla_p = jnp.swapaxes(gla_p, -1, -2)

    attn_s = _sample_attn(q_s, ckv_s, kpe_s, cache_ckv[0], cache_kpe[0], page_table, kc, ks, g_kh[0],
                          w_uk, w_uv, t_len=ts_len, chunk_pages=32)
    o_s, gla_s = _gla_sample(gq_s, gk_s, gv_s, lg_s, gr_s, g_gla_o[0], state_gla[0], t_len=ts_len, nb=8)

    w_out = bf(w_out_ab[0])
    hp = _post(x_prompt.reshape(n_p, D_MODEL), p_prompt[0].reshape(n_p, PLE_DIM), post_w(0),
               mix=(attn_p, o_p, w_out), tm=1024)
    hs = _post(x_sample.reshape(n_s, D_MODEL), p_sample[0].reshape(n_s, PLE_DIM), post_w(0),
               mix=(attn_s, o_s, w_out), tm=n_s // 2)

    w_b, w_c, lam = _s5_weights(s5_a_re[0], s5_a_im[0], s5_log_dt[0], s5_b_re[0], s5_b_im[0],
                                s5_c_re[0], s5_c_im[0])
    n_state = S5_GROUPS * S5_STATE
    zero_state = jnp.zeros((bp, n_state), F32)
    hp, s5r_p, s5i_p = _s5(hp.reshape(bp, sp, D_MODEL), zero_state, zero_state, g_norm_c[0], w_b, w_c, lam,
                           s5_d[0], w_glu[0], nb=bp, t_tile=64, batch_major=True)
    hs_tm, s5r_s, s5i_s = _s5(_to_time_major(hs, bs, ts_len), state_s5_re[0].reshape(bs, n_state),
                              state_s5_im[0].reshape(bs, n_state), g_norm_c[0], w_b, w_c, lam,
                              s5_d[0], w_glu[0], nb=bs, t_tile=ts_len, batch_major=False)
    hp = _post(hp.reshape(n_p, D_MODEL), p_prompt[1].reshape(n_p, PLE_DIM), post_w(1), tm=1024)
    hs = _post(_to_batch_major(hs_tm, bs, ts_len), p_sample[1].reshape(n_s, PLE_DIM), post_w(1), tm=n_s // 2)

    st = lambda a, b: a.reshape(1, b, S5_GROUPS, S5_STATE)
    return (hp.reshape(bp, sp, D_MODEL), hs.reshape(bs, ts_len, D_MODEL),
            ckv_p.reshape(1, bp, sp, KV_LORA), kpe_p.reshape(1, bp, sp, ROPE_DIM), gla_p[None],
            st(s5r_p, bp), st(s5i_p, bp),
            ckv_s.reshape(1, bs, ts_len, KV_LORA), kpe_s.reshape(1, bs, ts_len, ROPE_DIM), gla_s[None],
            st(s5r_s, bs), st(s5i_s, bs))
```

```python
import functools
from typing import NamedTuple

import jax
import jax.numpy as jnp
from jax import lax
from jax.experimental import pallas as pl
from jax.experimental.pallas import tpu as pltpu

F32 = jnp.float32
BF16 = jnp.bfloat16

D_MODEL = 1024
PAGE_SIZE = 128
MLA_HEADS = 8
Q_LORA = 384
KV_LORA = 256
NOPE_DIM = 64
ROPE_DIM = 32
QK_DIM = NOPE_DIM + ROPE_DIM
V_DIM = 64
ROPE_THETA = 10000.0
GLA_HEADS = 4
GLA_DK = 128
GLA_DV = 128
GLA_GATE_RANK = 16
GLA_TAU = 16.0
GLA_CHUNK = 64
S5_GROUP = 16
S5_GROUPS = D_MODEL // S5_GROUP
S5_STATE = 64
D_FF = 4 * D_MODEL
PLE_DIM = 256
EPS = 1e-6
IN_SPLITS = (Q_LORA, KV_LORA, ROPE_DIM, GLA_HEADS * GLA_DK, GLA_HEADS * GLA_DK,
             GLA_HEADS * GLA_DV, GLA_GATE_RANK, GLA_HEADS * GLA_DV)

LANES = 128
SUBLANES = 8
HEAD_PAD = LANES
VMEM_LIMIT = 56 * 1024 * 1024
S5_SLABS = D_MODEL // LANES
S5_SLAB_STATE = (LANES // S5_GROUP) * S5_STATE
NEG_INF = float("-inf")
LOG2_E = 1.4426950408889634


class _Tiles(NamedTuple):
    front_rows: int
    post_rows: int
    sample_rows: int
    attn_queries: int
    gla_rows: int
    s5_positions: int
    attn_chunk_pages: int
    gla_sample_batch: int


def _tiles(seq, sample_rows, n_pages):
    pick = lambda n, *sizes: next(s for s in sizes if n % s == 0)
    return _Tiles(front_rows=pick(seq, 512, 256, 128), post_rows=pick(seq, 1024, 512, 256, 128),
                  sample_rows=sample_rows // 2, attn_queries=pick(seq, 512, 256, 128),
                  gla_rows=pick(seq, 1024, 512, 256, 128, GLA_CHUNK), s5_positions=pick(seq, 64, 32, 16, 8),
                  attn_chunk_pages=pick(n_pages, 32, 16, 8, 4, 2, 1), gla_sample_batch=SUBLANES)


def _vmem_spec():
    return pl.BlockSpec(memory_space=pltpu.VMEM)


def _params(*sem):
    return pltpu.CompilerParams(dimension_semantics=sem, vmem_limit_bytes=VMEM_LIMIT)


def _bdot(a, b):
    return jnp.dot(a.astype(BF16), b.astype(BF16), preferred_element_type=F32)


def _bdot_nt(a, b):
    return lax.dot_general(a.astype(BF16), b.astype(BF16), (((1,), (1,)), ((), ())),
                           preferred_element_type=F32)


def _bdot_tn(a, b):
    return lax.dot_general(a.astype(BF16), b.astype(BF16), (((0,), (0,)), ((), ())),
                           preferred_element_type=F32)


def _rms(x, g):
    return x * lax.rsqrt(jnp.mean(x * x, axis=-1, keepdims=True) + EPS) * g


def _sigmoid(x):
    return 1.0 / (1.0 + jnp.exp(-x))


def _rope_table_kernel(cos_ref, sin_ref):
    shape = cos_ref.shape
    half = ROPE_DIM // 2
    per_row = LANES // half
    row = lax.broadcasted_iota(jnp.int32, shape, 0)
    lane = lax.broadcasted_iota(jnp.int32, shape, 1)
    pos = (row * per_row + lane // half).astype(F32)
    freq = jnp.power(F32(ROPE_THETA), -(lane % half).astype(F32) / half)
    ang = pos * freq
    cos_ref[...] = jnp.cos(ang)
    sin_ref[...] = jnp.sin(ang)


def _rope_tables(n_pos):
    half = ROPE_DIM // 2
    per_row = LANES // half
    rows = -(-n_pos // per_row)
    rows = -(-rows // SUBLANES) * SUBLANES
    cos, sin = pl.pallas_call(
        _rope_table_kernel,
        out_shape=(jax.ShapeDtypeStruct((rows, LANES), F32),) * 2,
        name="rope_table",
    )()
    cos = cos.reshape(rows * per_row, half)[:n_pos]
    sin = sin.reshape(rows * per_row, half)[:n_pos]
    return cos, sin


def _front_kernel(x_ref, tc_ref, ts_ref, gn_ref, wcq_ref, wckv_ref, wkpe_ref, wgq_ref, wgk_ref, wgv_ref,
                  wgr_ref, wga_ref, gql_ref, wuq_ref, gkvl_ref, wuk_ref, wuv_ref, gqh_ref, gkh_ref,
                  wa2_ref, ba_ref,
                  q_ref, k_ref, v_ref, ckv_ref, kpe_ref, gq_ref, gk_ref, gv_ref, lg_ref, gr_ref):
    n = _rms(x_ref[...], gn_ref[...])
    nb = n.astype(BF16)
    tc = tc_ref[...]
    ts = ts_ref[...]
    lane = lax.broadcasted_iota(jnp.int32, tc.shape, 1)
    first_half = lane < NOPE_DIM + ROPE_DIM // 2

    def norm_rope(xh, g):
        ss = jnp.sum(xh * xh, axis=-1, keepdims=True) * (1.0 / QK_DIM)
        y = xh * lax.rsqrt(ss + EPS) * g
        partner = jnp.where(first_half, pltpu.roll(y, LANES - ROPE_DIM // 2, 1), pltpu.roll(y, ROPE_DIM // 2, 1))
        return y * tc + partner * ts

    cq = _rms(jnp.dot(nb, wcq_ref[...], preferred_element_type=F32), gql_ref[...])
    q = _bdot(cq, wuq_ref[...])
    ckv = _rms(jnp.dot(nb, wckv_ref[...], preferred_element_type=F32), gkvl_ref[...])
    ckv_ref[...] = ckv
    ckvb = ckv.astype(BF16)
    kn = jnp.dot(ckvb, wuk_ref[...], preferred_element_type=F32)
    v_ref[...] = jnp.dot(ckvb, wuv_ref[...], preferred_element_type=F32).astype(v_ref.dtype)
    kpe = jnp.dot(nb, wkpe_ref[...], preferred_element_type=F32)
    kpe_ref[...] = kpe
    gqh = gqh_ref[...]
    gkh = gkh_ref[...]
    scale = QK_DIM ** -0.5 * LOG2_E
    for h in range(MLA_HEADS):
        hs = slice(h * HEAD_PAD, (h + 1) * HEAD_PAD)
        q_ref[:, hs] = (norm_rope(q[:, hs], gqh) * scale).astype(q_ref.dtype)
        k_ref[:, hs] = norm_rope(kn[:, hs] + kpe, gkh).astype(k_ref.dtype)

    gq_ref[...] = jnp.dot(nb, wgq_ref[...], preferred_element_type=F32) * (GLA_DK ** -0.5)
    gk_ref[...] = jnp.dot(nb, wgk_ref[...], preferred_element_type=F32)
    gv_ref[...] = jnp.dot(nb, wgv_ref[...], preferred_element_type=F32)
    gr_ref[...] = jnp.dot(nb, wgr_ref[...], preferred_element_type=F32)
    ga = jnp.dot(nb, wga_ref[...], preferred_element_type=F32)
    logit = _bdot(ga, wa2_ref[...]) + ba_ref[...]
    log_sig = jnp.minimum(logit, 0.0) - jnp.log1p(jnp.exp(-jnp.abs(logit)))
    lg_ref[...] = log_sig * (1.0 / GLA_TAU)


def _front_weights(w_in, g_norm, g_qlat, w_uq, g_kvlat, w_ukv, g_qh, g_kh, w_a2, b_a):
    offs = [0]
    for s in IN_SPLITS:
        offs.append(offs[-1] + s)
    cols = [w_in[:, offs[i]:offs[i + 1]] for i in range(len(IN_SPLITS))]
    w_cq, w_ckv, w_kpe, w_gq, w_gk, w_gv, w_ga, w_gr = cols
    pad_head = HEAD_PAD - QK_DIM
    w_kpe_placed = jnp.pad(w_kpe, ((0, 0), (NOPE_DIM, HEAD_PAD - QK_DIM)))
    w_ga_pad = jnp.pad(w_ga, ((0, 0), (0, LANES - GLA_GATE_RANK)))
    w_a2_pad = jnp.pad(w_a2, ((0, LANES - GLA_GATE_RANK), (0, 0)))
    w_uq_pad = jnp.pad(w_uq.reshape(Q_LORA, MLA_HEADS, QK_DIM), ((0, 0), (0, 0), (0, pad_head)))
    w_uq_pad = w_uq_pad.reshape(Q_LORA, MLA_HEADS * HEAD_PAD)
    w_ukv3 = w_ukv.reshape(KV_LORA, MLA_HEADS, NOPE_DIM + V_DIM)
    w_uk = w_ukv3[:, :, :NOPE_DIM]
    w_uv = w_ukv3[:, :, NOPE_DIM:].reshape(KV_LORA, MLA_HEADS * V_DIM)
    w_uk_pad = jnp.pad(w_uk, ((0, 0), (0, 0), (0, HEAD_PAD - NOPE_DIM))).reshape(KV_LORA, MLA_HEADS * HEAD_PAD)
    g_qh_pad = jnp.pad(g_qh, (0, pad_head))[None, :]
    g_kh_pad = jnp.pad(g_kh, (0, pad_head))[None, :]
    bf = lambda a: a.astype(BF16)
    weights = (g_norm[None, :], bf(w_cq), bf(w_ckv), bf(w_kpe_placed), bf(w_gq), bf(w_gk), bf(w_gv), bf(w_gr),
               bf(w_ga_pad), g_qlat[None, :], bf(w_uq_pad), g_kvlat[None, :], bf(w_uk_pad), bf(w_uv),
               g_qh_pad, g_kh_pad, bf(w_a2_pad), b_a[None, :])
    return weights, w_uk, w_uv


def _front(x, tc, ts, weights, *, tm, table_tiles):
    n = x.shape[0]
    hk = GLA_HEADS * GLA_DK
    row = lambda w: pl.BlockSpec((tm, w), lambda i: (i, 0))
    tab = pl.BlockSpec((tm, LANES), lambda i: (i % table_tiles, 0))
    sds = lambda w, dt: jax.ShapeDtypeStruct((n, w), dt)
    out_shape = (sds(MLA_HEADS * HEAD_PAD, BF16), sds(MLA_HEADS * HEAD_PAD, BF16), sds(MLA_HEADS * V_DIM, BF16),
                 sds(KV_LORA, F32), sds(LANES, F32), sds(hk, F32), sds(hk, F32), sds(hk, F32), sds(hk, F32),
                 sds(hk, F32))
    out_specs = (row(MLA_HEADS * HEAD_PAD), row(MLA_HEADS * HEAD_PAD), row(MLA_HEADS * V_DIM), row(KV_LORA),
                 row(LANES), row(hk), row(hk), row(hk), row(hk), row(hk))
    return pl.pallas_call(
        _front_kernel,
        grid=(n // tm,),
        in_specs=[row(D_MODEL), tab, tab] + [_vmem_spec()] * len(weights),
        out_specs=out_specs,
        out_shape=out_shape,
        compiler_params=_params("parallel"),
        name="ab_front",
    )(x, tc, ts, *weights)


def _prompt_attn_kernel(q_ref, k_ref, v_ref, o_ref, *, tq, n_q):
    i = pl.program_id(1)
    row = lax.broadcasted_iota(jnp.int32, (tq, tq), 0)
    col = lax.broadcasted_iota(jnp.int32, (tq, tq), 1)
    causal = col <= row
    lane = lax.broadcasted_iota(jnp.int32, (tq, LANES), 1)

    def case(c):
        kv = (c + 1) * tq
        outs = []
        for h in range(MLA_HEADS):
            hs = slice(h * HEAD_PAD, (h + 1) * HEAD_PAD)
            vs = slice((h // 2) * LANES, (h // 2 + 1) * LANES)
            s = lax.dot_general(q_ref[:, hs], k_ref[:kv, hs], (((1,), (1,)), ((), ())),
                                preferred_element_type=F32)
            diag = jnp.where(causal, s[:, c * tq:], NEG_INF)
            s = diag if c == 0 else jnp.concatenate([s[:, :c * tq], diag], axis=1)
            p = jnp.exp2(s - jnp.max(s, axis=-1, keepdims=True))
            l = jnp.sum(p, axis=-1, keepdims=True)
            outs.append(jnp.dot(p.astype(BF16), v_ref[:kv, vs], preferred_element_type=F32) / l)
        for pair in range(MLA_HEADS // 2):
            o_ref[:, pair * LANES:(pair + 1) * LANES] = jnp.where(
                lane < V_DIM, outs[2 * pair], outs[2 * pair + 1]).astype(o_ref.dtype)

    for c in range(n_q):
        pl.when(i == c)(functools.partial(case, c))


def _prompt_attn(q, k, v, *, batch, seq, tq):
    nq = seq // tq
    return pl.pallas_call(
        functools.partial(_prompt_attn_kernel, tq=tq, n_q=nq),
        grid=(batch, nq),
        in_specs=[pl.BlockSpec((tq, MLA_HEADS * HEAD_PAD), lambda b, i: (b * nq + i, 0)),
                  pl.BlockSpec((seq, MLA_HEADS * HEAD_PAD), lambda b, i: (b, 0)),
                  pl.BlockSpec((seq, MLA_HEADS * V_DIM), lambda b, i: (b, 0))],
        out_specs=pl.BlockSpec((tq, MLA_HEADS * V_DIM), lambda b, i: (b * nq + i, 0)),
        out_shape=jax.ShapeDtypeStruct((batch * seq, MLA_HEADS * V_DIM), BF16),
        compiler_params=_params("parallel", "arbitrary"),
        name="mla_prompt_attn",
    )(q, k, v)


def _gla_intra(q, k, b):
    c = q.shape[0]
    row = lax.broadcasted_iota(jnp.int32, q.shape, 0)
    r2 = lax.broadcasted_iota(jnp.int32, (c, c), 0)
    c2 = lax.broadcasted_iota(jnp.int32, (c, c), 1)
    att = jnp.where(r2 == c2, _bdot_nt(q, k), 0.0)
    before = pltpu.roll(b, 1, 0)
    last = b
    m = 1
    while m < c:
        upper = (row // m) % 2 == 1
        qm = jnp.where(upper, q * jnp.exp(b - before), 0.0)
        km = jnp.where(upper, 0.0, k * jnp.exp(last - b))
        same = (r2 // (2 * m)) == (c2 // (2 * m))
        att = att + jnp.where(same, _bdot_nt(qm, km), 0.0)
        if 2 * m < c:
            before = jnp.where(upper, pltpu.roll(before, m, 0), before)
            last = jnp.where(upper, last, pltpu.roll(last, c - m, 0))
        m *= 2
    return att


def _cumsum_rows(g):
    c = g.shape[0]
    row = lax.broadcasted_iota(jnp.int32, g.shape, 0)
    b = g
    sh = 1
    while sh < c:
        b = b + jnp.where(row >= sh, pltpu.roll(b, sh, 0), 0.0)
        sh *= 2
    return b


def _gla_prompt_kernel(gq_ref, gk_ref, gv_ref, lg_ref, gr_ref, go_ref, o_ref, st_ref, state_ref, *, n_chunks):
    t = pl.program_id(1)

    @pl.when(t == 0)
    def _():
        state_ref[...] = jnp.zeros_like(state_ref)

    go = go_ref[...]

    for h in range(GLA_HEADS):
        hs = slice(h * GLA_DK, (h + 1) * GLA_DK)
        st = state_ref[h]
        for ci in range(n_chunks):
            rows = slice(ci * GLA_CHUNK, (ci + 1) * GLA_CHUNK)
            q = gq_ref[rows, hs]
            k = gk_ref[rows, hs]
            v = gv_ref[rows, hs]
            b = _cumsum_rows(lg_ref[rows, hs])
            o = _bdot_nt(q * jnp.exp(b), st) + _bdot(_gla_intra(q, k, b), v)
            bl = b[GLA_CHUNK - 1:GLA_CHUNK, :]
            st = st * jnp.exp(bl) + _bdot_tn(v, k * jnp.exp(bl - b))
            gr = gr_ref[rows, hs]
            o_ref[rows, hs] = (_rms(o, go) * (gr * _sigmoid(gr))).astype(o_ref.dtype)
        state_ref[h] = st

    @pl.when(t == pl.num_programs(1) - 1)
    def _():
        st_ref[0] = state_ref[...]


def _gla_prompt(gq, gk, gv, lg, gr, g_o, *, batch, seq, tg):
    nt = seq // tg
    hk = GLA_HEADS * GLA_DK
    row = pl.BlockSpec((tg, hk), lambda b, t: (b * nt + t, 0))
    return pl.pallas_call(
        functools.partial(_gla_prompt_kernel, n_chunks=tg // GLA_CHUNK),
        grid=(batch, nt),
        in_specs=[row, row, row, row, row, _vmem_spec()],
        out_specs=(row, pl.BlockSpec((1, GLA_HEADS, GLA_DV, GLA_DK), lambda b, t: (b, 0, 0, 0))),
        out_shape=(jax.ShapeDtypeStruct((batch * seq, hk), BF16),
                   jax.ShapeDtypeStruct((batch, GLA_HEADS, GLA_DV, GLA_DK), F32)),
        scratch_shapes=[pltpu.VMEM((GLA_HEADS, GLA_DV, GLA_DK), F32)],
        compiler_params=_params("parallel", "arbitrary"),
        name="gla_prompt",
    )(gq, gk, gv, lg, gr, g_o[None, :])


def _gla_sample_kernel(gq_ref, gk_ref, gv_ref, lg_ref, gr_ref, go_ref, s0_ref, o_ref, s1_ref, *, nb, t_len):
    rows = nb * t_len
    per_tile = SUBLANES // t_len
    tok = lax.broadcasted_iota(jnp.int32, (rows, GLA_DK), 0) % t_len
    tile_batch = lax.broadcasted_iota(jnp.int32, (SUBLANES, GLA_DK), 0) // t_len
    pad = jnp.zeros((LANES - rows, LANES), F32)
    go = go_ref[...]
    for h in range(GLA_HEADS):
        hs = slice(h * GLA_DK, (h + 1) * GLA_DK)
        q = gq_ref[:, hs]
        k = gk_ref[:, hs]
        v = gv_ref[:, hs]
        b = lg_ref[:, hs]
        sh = 1
        while sh < t_len:
            b = b + jnp.where(tok >= sh, pltpu.roll(b, sh, 0), 0.0)
            sh *= 2
        bl = jnp.where(tok == t_len - 1, b, 0.0)
        for d in range(1, t_len):
            bl = bl + jnp.where(tok == t_len - 1 - d, pltpu.roll(b, rows - d, 0), 0.0)
        qe = q * jnp.exp(b)
        kd = k * jnp.exp(bl - b)
        o = jnp.sum(q * k, axis=-1, keepdims=True) * v
        for d in range(1, t_len):
            w = jnp.sum(q * pltpu.roll(k, d, 0) * jnp.exp(b - pltpu.roll(b, d, 0)), axis=-1, keepdims=True)
            o = o + jnp.where(tok[:, :1] >= d, w, 0.0) * pltpu.roll(v, d, 0)
        decay_cols = jnp.transpose(jnp.concatenate([jnp.exp(bl), pad], axis=0))
        inter = []
        for tile in range(rows // SUBLANES):
            ts = slice(tile * SUBLANES, (tile + 1) * SUBLANES)
            acc = jnp.zeros((SUBLANES, GLA_DV), F32)
            for e in range(per_tile):
                bi = tile * per_tile + e
                s0 = s0_ref[bi, h]
                mine = tile_batch == e
                acc = jnp.where(mine, _bdot(qe[ts], s0), acc)
                r_last = bi * t_len + t_len - 1
                s1_ref[bi, h] = s0 * decay_cols[:, r_last:r_last + 1] + _bdot_tn(jnp.where(mine, kd[ts], 0.0), v[ts])
            inter.append(acc)
        o = o + jnp.concatenate(inter, axis=0)
        gr = gr_ref[:, hs]
        o_ref[:, hs] = (_rms(o, go) * (gr * _sigmoid(gr))).astype(o_ref.dtype)


def _gla_sample(gq, gk, gv, lg, gr, g_o, s0, *, t_len, nb):
    n = gq.shape[0]
    batch = n // t_len
    hk = GLA_HEADS * GLA_DK
    rows = nb * t_len
    row = pl.BlockSpec((rows, hk), lambda i: (i, 0))
    st = pl.BlockSpec((nb, GLA_HEADS, GLA_DK, GLA_DV), lambda i: (i, 0, 0, 0))
    return pl.pallas_call(
        functools.partial(_gla_sample_kernel, nb=nb, t_len=t_len),
        grid=(batch // nb,),
        in_specs=[row, row, row, row, row, _vmem_spec(), st],
        out_specs=(row, st),
        out_shape=(jax.ShapeDtypeStruct((n, hk), BF16), jax.ShapeDtypeStruct(s0.shape, s0.dtype)),
        compiler_params=_params("parallel"),
        name="gla_sample",
    )(gq, gk, gv, lg, gr, g_o[None, :], s0)


def _sample_attn_kernel(pt_ref, q_ref, cnew_ref, pnew_ref, kc_ref, ks_ref, kcn_ref, ksn_ref, gk_ref, gpe_ref,
                        wukt_ref, wukp_ref, wuv_ref, ckv_hbm, kpe_hbm, o_ref, ckv_buf, kpe_buf, sem,
                        *, n_pages, chunk_pages, t_len):
    n_cols = t_len * MLA_HEADS
    chunk = chunk_pages * PAGE_SIZE
    n_chunks = n_pages // chunk_pages
    past = n_pages * PAGE_SIZE
    b = pl.program_id(0)
    slot = b % 2

    def page_copies(batch, sl, i):
        page = pt_ref[batch, i]
        ci = i // chunk_pages
        return (pltpu.make_async_copy(ckv_hbm.at[page], ckv_buf.at[sl, i], sem.at[sl, ci, 0]),
                pltpu.make_async_copy(kpe_hbm.at[page], kpe_buf.at[sl, i], sem.at[sl, ci, 1]))

    def start_pages(batch, sl, pages):
        for i in pages:
            for cp in page_copies(batch, sl, i):
                cp.start()

    def wait_pages(batch, sl, pages):
        for i in pages:
            for cp in page_copies(batch, sl, i):
                cp.wait()

    @pl.when(b == 0)
    def _():
        start_pages(0, 0, range(n_pages))

    qg = q_ref[...].astype(F32) * gk_ref[...]
    gpe = gpe_ref[...]
    sub = lax.broadcasted_iota(jnp.int32, (SUBLANES, MLA_HEADS * HEAD_PAD), 0)
    lane_head = lax.broadcasted_iota(jnp.int32, (SUBLANES, MLA_HEADS * HEAD_PAD), 1) // HEAD_PAD
    qexp = jnp.concatenate(
        [jnp.where(lane_head == sub, jnp.broadcast_to(qg[t:t + 1, :], sub.shape), 0.0) for t in range(t_len)],
        axis=0)
    qt = _bdot(qexp, wukt_ref[...])
    lhs = jnp.concatenate([wukp_ref[...], qt[:, :KV_LORA].astype(BF16)], axis=0)
    qr = qt[:, KV_LORA:KV_LORA + ROPE_DIM].astype(BF16)
    n_norm = NOPE_DIM * MLA_HEADS
    half = ROPE_DIM // 2

    def scores(c, kt, kcos, ksin):
        n = c.shape[0]
        cb = c.astype(BF16)
        r = lax.dot_general(lhs, cb, (((1,), (1,)), ((), ())), preferred_element_type=F32)
        kn = r[:n_norm].reshape(NOPE_DIM, MLA_HEADS, n)
        ss = jnp.sum(kn * kn, axis=0) + jnp.sum(kt * kt, axis=0, keepdims=True)
        rs = lax.rsqrt(ss * (1.0 / QK_DIM) + EPS)
        kg = kt * gpe
        kr = kg * kcos + jnp.concatenate([kg[half:], kg[:half]], axis=0) * ksin
        s = r[n_norm:] + jnp.dot(qr, kr.astype(BF16), preferred_element_type=F32)
        s = (s.reshape(t_len, MLA_HEADS, n) * rs[None]).reshape(n_cols, n)
        return cb, s

    def fold(state, cb, s):
        m, l, acc = state
        m_new = jnp.maximum(m, jnp.max(s, axis=1, keepdims=True))
        alpha = jnp.exp2(m - m_new)
        p = jnp.exp2(s - m_new)
        l = alpha * l + jnp.sum(p, axis=1, keepdims=True)
        acc = alpha * acc + jnp.dot(p.astype(BF16), cb, preferred_element_type=F32)
        return m_new, l, acc

    @pl.when(b + 1 < pl.num_programs(0))
    def _():
        start_pages(b + 1, 1 - slot, range(n_pages))

    wait_pages(b, slot, range(n_pages))
    state = (jnp.full((n_cols, 1), NEG_INF, F32), jnp.zeros((n_cols, 1), F32), jnp.zeros((n_cols, KV_LORA), F32))
    for ci in range(n_chunks):
        pages = range(ci * chunk_pages, (ci + 1) * chunk_pages)
        c = ckv_buf[slot, pl.ds(ci * chunk_pages, chunk_pages)].reshape(chunk, KV_LORA)
        kt = jnp.concatenate([kpe_buf[slot, p] for p in pages], axis=1)
        keys = slice(ci * chunk, (ci + 1) * chunk)
        state = fold(state, *scores(c, kt, kc_ref[:, keys], ks_ref[:, keys]))

    zpad = lambda a: jnp.concatenate([a, jnp.zeros((PAGE_SIZE - t_len, a.shape[1]), a.dtype)], axis=0)
    pnew = jnp.concatenate([zpad(pnew_ref[...]), jnp.zeros((PAGE_SIZE, LANES - ROPE_DIM), F32)], axis=1)
    cb, s = scores(zpad(cnew_ref[...]), jnp.transpose(pnew)[:ROPE_DIM], kcn_ref[...], ksn_ref[...])
    r2 = lax.broadcasted_iota(jnp.int32, (n_cols, PAGE_SIZE), 0)
    c2 = lax.broadcasted_iota(jnp.int32, (n_cols, PAGE_SIZE), 1)
    s = jnp.where((c2 < t_len) & (c2 <= r2 // MLA_HEADS), s, NEG_INF)
    _, l, acc = fold(state, cb, s)

    lat = acc / l
    res = _bdot(lat, wuv_ref[...])
    r3 = lax.broadcasted_iota(jnp.int32, res.shape, 0)
    c3 = lax.broadcasted_iota(jnp.int32, res.shape, 1)
    res = jnp.where(c3 // V_DIM == r3 % MLA_HEADS, res, 0.0)
    o_ref[...] = jnp.sum(res.reshape(t_len, MLA_HEADS, MLA_HEADS * V_DIM), axis=1).astype(o_ref.dtype)


def _sample_attn(q, ckv_new, kpe_new, cache_ckv, cache_kpe, page_table, kc, ks, g_kh, w_uk, w_uv,
                 *, t_len, chunk_pages):
    batch, n_pages = page_table.shape
    past = n_pages * PAGE_SIZE
    w_ukt = jnp.pad(jnp.transpose(w_uk, (1, 2, 0)), ((0, 0), (0, HEAD_PAD - NOPE_DIM), (0, 0)))
    w_ukt = w_ukt.reshape(MLA_HEADS * HEAD_PAD, KV_LORA)
    sel = jnp.pad(jnp.eye(ROPE_DIM, dtype=F32), ((NOPE_DIM, HEAD_PAD - QK_DIM), (0, LANES - ROPE_DIM)))
    w_ukt_ext = jnp.concatenate([w_ukt, jnp.tile(sel, (MLA_HEADS, 1))], axis=1).astype(BF16)
    w_uk_perm = jnp.transpose(w_uk, (2, 1, 0)).reshape(NOPE_DIM * MLA_HEADS, KV_LORA).astype(BF16)
    g_k_pad = jnp.concatenate([g_kh[:NOPE_DIM], jnp.ones((ROPE_DIM,), F32), jnp.zeros((HEAD_PAD - QK_DIM,), F32)])
    g_k_pad = jnp.tile(g_k_pad, MLA_HEADS)[None, :]
    g_pe = g_kh[NOPE_DIM:][:, None]
    kc, ks = jnp.transpose(kc), jnp.transpose(ks)
    kcn = jnp.pad(kc[:, past:past + t_len], ((0, 0), (0, PAGE_SIZE - t_len)))
    ksn = jnp.pad(ks[:, past:past + t_len], ((0, 0), (0, PAGE_SIZE - t_len)))
    per_b = lambda w: pl.BlockSpec((None, t_len, w), lambda b, pt: (b, 0, 0))
    hbm = pl.BlockSpec(memory_space=pl.ANY)
    in_specs = ([per_b(MLA_HEADS * HEAD_PAD), per_b(KV_LORA), per_b(ROPE_DIM)] + [_vmem_spec()] * 9 + [hbm, hbm])
    grid_spec = pltpu.PrefetchScalarGridSpec(
        num_scalar_prefetch=1,
        grid=(batch,),
        in_specs=in_specs,
        out_specs=pl.BlockSpec((None, t_len, MLA_HEADS * V_DIM), lambda b, pt: (b, 0, 0)),
        scratch_shapes=[pltpu.VMEM((2, n_pages, PAGE_SIZE, KV_LORA), F32),
                        pltpu.VMEM((2, n_pages, ROPE_DIM, PAGE_SIZE), F32),
                        pltpu.SemaphoreType.DMA((2, n_pages // chunk_pages, 2))],
    )
    out = pl.pallas_call(
        functools.partial(_sample_attn_kernel, n_pages=n_pages, chunk_pages=chunk_pages, t_len=t_len),
        grid_spec=grid_spec,
        out_shape=jax.ShapeDtypeStruct((batch, t_len, MLA_HEADS * V_DIM), BF16),
        compiler_params=_params("arbitrary"),
        name="mla_sample_attn",
    )(page_table, q.reshape(batch, t_len, -1), ckv_new.reshape(batch, t_len, -1),
      kpe_new.reshape(batch, t_len, -1), kc[:, :past], ks[:, :past], kcn, ksn, g_k_pad, g_pe,
      w_ukt_ext, w_uk_perm, w_uv.astype(BF16), cache_ckv, jnp.swapaxes(cache_kpe, 1, 2))
    return out.reshape(batch * t_len, MLA_HEADS * V_DIM)


def _mlp_ple(h, p_ref, gm_ref, wup_ref, wdown_ref, gp_ref, wgate_ref, wproj_ref, o_ref, ff_chunk):
    nb = _rms(h, gm_ref[...]).astype(BF16)
    y = jnp.zeros_like(h)
    for c in range(D_FF // ff_chunk):
        cs = slice(c * ff_chunk, (c + 1) * ff_chunk)
        a = jnp.maximum(jnp.dot(nb, wup_ref[:, cs], preferred_element_type=F32), 0.0)
        y = y + jnp.dot((a * a).astype(BF16), wdown_ref[cs, :], preferred_element_type=F32)
    h = h + y
    gate = _sigmoid(_bdot(_rms(h, gp_ref[...]), wgate_ref[...]))
    o_ref[...] = h + gate * _bdot(p_ref[...], wproj_ref[...])


def _post_mix_kernel(h_ref, a_ref, o2_ref, p_ref, wout_ref, gm_ref, wup_ref, wdown_ref, gp_ref, wgate_ref,
                     wproj_ref, o_ref, *, ff_chunk):
    n_attn = MLA_HEADS * V_DIM
    h = (h_ref[...] + jnp.dot(a_ref[...], wout_ref[:n_attn, :], preferred_element_type=F32)
         + jnp.dot(o2_ref[...], wout_ref[n_attn:, :], preferred_element_type=F32))
    _mlp_ple(h, p_ref, gm_ref, wup_ref, wdown_ref, gp_ref, wgate_ref, wproj_ref, o_ref, ff_chunk)


def _post_kernel(h_ref, p_ref, gm_ref, wup_ref, wdown_ref, gp_ref, wgate_ref, wproj_ref, o_ref, *, ff_chunk):
    _mlp_ple(h_ref[...], p_ref, gm_ref, wup_ref, wdown_ref, gp_ref, wgate_ref, wproj_ref, o_ref, ff_chunk)


def _post(h, p_layers, layer, weights, mix=None, *, tm, ff_chunk=1024):
    n = h.shape[0]
    row = lambda w: pl.BlockSpec((tm, w), lambda i: (i, 0))
    p_spec = pl.BlockSpec((None, tm, PLE_DIM), lambda i: (layer, i, 0))
    if mix is None:
        body, acts, specs = _post_kernel, (h, p_layers), [row(D_MODEL), p_spec]
    else:
        attn, o_gla, w_out = mix
        body, acts = _post_mix_kernel, (h, attn, o_gla, p_layers)
        specs = [row(D_MODEL), row(attn.shape[1]), row(o_gla.shape[1]), p_spec]
        weights = (w_out,) + tuple(weights)
    return pl.pallas_call(
        functools.partial(body, ff_chunk=ff_chunk),
        grid=(n // tm,),
        in_specs=specs + [_vmem_spec()] * len(weights),
        out_specs=row(D_MODEL),
        out_shape=jax.ShapeDtypeStruct((n, D_MODEL), F32),
        compiler_params=_params("parallel"),
        name="post_mix" if mix is not None else "post",
    )(*acts, *weights)


def _s5_prep_kernel(are_ref, aim_ref, ldt_ref, bre_ref, bim_ref, lr_ref, li_ref, bbr_ref, bbi_ref):
    a_re = are_ref[...]
    a_im = aim_ref[...]
    dt = jnp.exp(ldt_ref[...])
    mag = jnp.exp(a_re * dt)
    lr = mag * jnp.cos(a_im * dt)
    li = mag * jnp.sin(a_im * dt)
    den = a_re * a_re + a_im * a_im
    cr = ((lr - 1.0) * a_re + li * a_im) / den
    ci = (li * a_re - (lr - 1.0) * a_im) / den
    lr_ref[...] = lr
    li_ref[...] = li
    b_re = bre_ref[...]
    b_im = bim_ref[...]
    bbr_ref[...] = cr[:, None, :] * b_re - ci[:, None, :] * b_im
    bbi_ref[...] = cr[:, None, :] * b_im + ci[:, None, :] * b_re


def _s5_weights(a_re, a_im, log_dt, b_re, b_im, c_re, c_im):
    g, p_, ch = S5_GROUPS, S5_STATE, S5_GROUP
    gp = jax.ShapeDtypeStruct((g, p_), F32)
    gcp = jax.ShapeDtypeStruct((g, ch, p_), F32)
    lr, li, bbr, bbi = pl.pallas_call(
        _s5_prep_kernel, out_shape=(gp, gp, gcp, gcp), name="s5_prep",
    )(a_re, a_im, log_dt[:, None], jnp.transpose(b_re, (0, 2, 1)), jnp.transpose(b_im, (0, 2, 1)))
    per_slab = LANES // ch
    eye = jnp.eye(per_slab, dtype=F32)

    def b_blocks(bb):
        bb = bb.reshape(S5_SLABS, per_slab, ch, p_)
        return jnp.einsum('jgcp,gh->jgchp', bb, eye).reshape(S5_SLABS, LANES, S5_SLAB_STATE)

    def c_blocks(cc):
        cc = cc.reshape(S5_SLABS, per_slab, ch, p_)
        return jnp.einsum('jgcp,gh->jgphc', cc, eye).reshape(S5_SLABS, S5_SLAB_STATE, LANES)

    w_b = jnp.concatenate([b_blocks(bbr), b_blocks(bbi)], axis=2).astype(BF16)
    w_c = jnp.concatenate([c_blocks(c_re), -c_blocks(c_im)], axis=1).astype(BF16)
    lam = jnp.stack([lr.reshape(S5_SLABS, S5_SLAB_STATE), li.reshape(S5_SLABS, S5_SLAB_STATE)], axis=1)
    return w_b, w_c, lam


def _s5_kernel(x_ref, x0r_ref, x0i_ref, gn_ref, wb_ref, wc_ref, lam_ref, d_ref, wglu_ref,
               o_ref, xr_ref, xi_ref, bur_ref, bui_ref, sr_ref, si_ref, shuf_ref=None, *, nb, t_len):
    step = pl.program_id(0)
    ns = S5_SLAB_STATE

    @pl.when(step == 0)
    def _():
        for j in range(S5_SLABS):
            sr_ref[j] = x0r_ref[:, j * ns:(j + 1) * ns]
            si_ref[j] = x0i_ref[:, j * ns:(j + 1) * ns]

    if shuf_ref is None:
        x = x_ref[...]
    else:
        for bi in range(nb):
            for s in range(D_MODEL // LANES):
                shuf_ref[s, pl.ds(bi, t_len, stride=nb), :] = x_ref[bi, :, s * LANES:(s + 1) * LANES]
        x = jnp.concatenate([shuf_ref[s] for s in range(D_MODEL // LANES)], axis=1)
    u = _rms(x, gn_ref[...])
    ub = u.astype(BF16)
    ys = []
    for j in range(S5_SLABS):
        bu = jnp.dot(ub[:, j * LANES:(j + 1) * LANES], wb_ref[j], preferred_element_type=F32)
        bur_ref[j] = bu[:, :ns]
        bui_ref[j] = bu[:, ns:]
        lr = jnp.broadcast_to(lam_ref[j, 0:1, :], (SUBLANES, ns))
        li = jnp.broadcast_to(lam_ref[j, 1:2, :], (SUBLANES, ns))
        for gi in range(nb // SUBLANES):
            srow = slice(gi * SUBLANES, (gi + 1) * SUBLANES)
            xr = sr_ref[j, srow, :]
            xi = si_ref[j, srow, :]
            for t in range(t_len):
                rows = slice(t * nb + gi * SUBLANES, t * nb + (gi + 1) * SUBLANES)
                xr, xi = (lr * xr - li * xi + bur_ref[j, rows, :], lr * xi + li * xr + bui_ref[j, rows, :])
                bur_ref[j, rows, :] = xr
                bui_ref[j, rows, :] = xi
            sr_ref[j, srow, :] = xr
            si_ref[j, srow, :] = xi
        xs = jnp.concatenate([bur_ref[j].astype(BF16), bui_ref[j].astype(BF16)], axis=1)
        ys.append(jnp.dot(xs, wc_ref[j], preferred_element_type=F32))
    y = jnp.concatenate(ys, axis=1) + d_ref[...] * u
    gl = jax.nn.gelu(y)
    zg = _bdot(gl, wglu_ref[...])
    h = x + zg[:, :D_MODEL] * _sigmoid(zg[:, D_MODEL:])
    if shuf_ref is None:
        o_ref[...] = h
    else:
        for s in range(D_MODEL // LANES):
            shuf_ref[s] = h[:, s * LANES:(s + 1) * LANES]
        for bi in range(nb):
            for s in range(D_MODEL // LANES):
                o_ref[bi, :, s * LANES:(s + 1) * LANES] = shuf_ref[s, pl.ds(bi, t_len, stride=nb), :]

    @pl.when(step == pl.num_programs(0) - 1)
    def _():
        for j in range(S5_SLABS):
            xr_ref[:, j * ns:(j + 1) * ns] = sr_ref[j]
            xi_ref[:, j * ns:(j + 1) * ns] = si_ref[j]


def _s5(x, x0r, x0i, g_norm, w_b, w_c, lam, d, w_glu, *, nb, t_tile, batch_major):
    n = x.shape[0] * x.shape[1] if batch_major else x.shape[0]
    rows = nb * t_tile
    n_state = S5_GROUPS * S5_STATE
    full = lambda shape: pl.BlockSpec(shape, lambda i: (0,) * len(shape))
    st = jax.ShapeDtypeStruct((nb, n_state), F32)
    scratch = [pltpu.VMEM((S5_SLABS, rows, S5_SLAB_STATE), F32),
               pltpu.VMEM((S5_SLABS, rows, S5_SLAB_STATE), F32),
               pltpu.VMEM((S5_SLABS, nb, S5_SLAB_STATE), F32),
               pltpu.VMEM((S5_SLABS, nb, S5_SLAB_STATE), F32)]
    if batch_major:
        x_spec = out_spec = pl.BlockSpec((nb, t_tile, D_MODEL), lambda i: (0, i, 0))
        out_sds = jax.ShapeDtypeStruct((nb, n // nb, D_MODEL), F32)
        scratch.append(pltpu.VMEM((D_MODEL // LANES, rows, LANES), F32))
    else:
        x_spec = out_spec = pl.BlockSpec((rows, D_MODEL), lambda i: (i, 0))
        out_sds = jax.ShapeDtypeStruct((n, D_MODEL), F32)
    return pl.pallas_call(
        functools.partial(_s5_kernel, nb=nb, t_len=t_tile),
        grid=(n // rows,),
        in_specs=[x_spec, full((nb, n_state)), full((nb, n_state))] + [_vmem_spec()] * 6,
        out_specs=(out_spec, full((nb, n_state)), full((nb, n_state))),
        out_shape=(out_sds, st, st),
        scratch_shapes=scratch,
        compiler_params=_params("arbitrary"),
        name="s5_layer",
    )(x, x0r, x0i, g_norm[None, :], w_b, w_c, lam, d[None, :], w_glu.astype(BF16))


def _to_time_major(h, batch, seq):
    return h.reshape(batch, seq, -1).transpose(1, 0, 2).reshape(batch * seq, -1)


def _to_batch_major(h, batch, seq):
    return h.reshape(seq, batch, -1).transpose(1, 0, 2).reshape(batch * seq, -1)


def kernel(x_prompt, x_sample, cache_ckv, cache_kpe, state_gla, state_s5_re, state_s5_im, page_table,
           p_prompt, p_sample, g_norm_ab, w_in_ab, g_qlat, w_uq, g_kvlat, w_ukv, g_qh, g_kh,
           w_gla_a2, b_gla_a, g_gla_o, w_out_ab, g_norm_c, s5_a_re, s5_a_im, s5_log_dt,
           s5_b_re, s5_b_im, s5_c_re, s5_c_im, s5_d, w_glu, g_norm_mlp, w_up, w_down,
           g_norm_ple, w_ple_gate, w_ple_proj):
    bp, sp, _ = x_prompt.shape
    bs, ts_len, _ = x_sample.shape
    n_pages = page_table.shape[1]
    past = n_pages * PAGE_SIZE
    n_p, n_s = bp * sp, bs * ts_len
    bf = lambda a: a.astype(BF16)

    tiles = _tiles(sp, n_s, n_pages)
    n_layers = p_prompt.shape[0]
    pp = p_prompt.reshape(n_layers, n_p, PLE_DIM)
    ps = p_sample.reshape(n_layers, n_s, PLE_DIM)

    cos16, sin16 = _rope_tables(past + ts_len)
    kc = jnp.concatenate([cos16, cos16], axis=1)
    ks = jnp.concatenate([-sin16, sin16], axis=1)

    def head_tables(rows):
        c, s = cos16[rows], sin16[rows]
        ones = jnp.ones((c.shape[0], NOPE_DIM), F32)
        tail = jnp.zeros((c.shape[0], HEAD_PAD - QK_DIM), F32)
        return (jnp.concatenate([ones, c, c, tail + 1.0], axis=1),
                jnp.concatenate([ones * 0.0, -s, s, tail], axis=1))

    tc, ts = head_tables(slice(0, sp))
    tc_s, ts_s = (jnp.tile(a, (bs, 1)) for a in head_tables(slice(past, past + ts_len)))

    post_w = lambda i: (g_norm_mlp[i][None, :], bf(w_up[i]), bf(w_down[i]), g_norm_ple[i][None, :],
                        bf(w_ple_gate[i]), bf(w_ple_proj[i]))

    fw, w_uk, w_uv = _front_weights(w_in_ab[0], g_norm_ab[0], g_qlat[0], w_uq[0], g_kvlat[0], w_ukv[0],
                                    g_qh[0], g_kh[0], w_gla_a2[0], b_gla_a[0])
    (q_p, k_p, v_p, ckv_p, kpe_p, gq_p, gk_p, gv_p, lg_p, gr_p) = _front(
        x_prompt.reshape(n_p, D_MODEL), tc, ts, fw, tm=tiles.front_rows, table_tiles=sp // tiles.front_rows)
    (q_s, _, _, ckv_s, kpe_s, gq_s, gk_s, gv_s, lg_s, gr_s) = _front(
        x_sample.reshape(n_s, D_MODEL), tc_s, ts_s, fw, tm=tiles.sample_rows, table_tiles=n_s // tiles.sample_rows)
    kpe_p = kpe_p[:, NOPE_DIM:QK_DIM]
    kpe_s = kpe_s[:, NOPE_DIM:QK_DIM]

    attn_p = _prompt_attn(q_p, k_p, v_p, batch=bp, seq=sp, tq=tiles.attn_queries)
    o_p, gla_p = _gla_prompt(gq_p, gk_p, gv_p, lg_p, gr_p, g_gla_o[0], batch=bp, seq=sp, tg=tiles.gla_rows)
    gla_p = jnp.swapaxes(gla_p, -1, -2)

    attn_s = _sample_attn(q_s, ckv_s, kpe_s, cache_ckv[0], cache_kpe[0], page_table, kc, ks, g_kh[0],
                          w_uk, w_uv, t_len=ts_len, chunk_pages=tiles.attn_chunk_pages)
    o_s, gla_s = _gla_sample(gq_s, gk_s, gv_s, lg_s, gr_s, g_gla_o[0], state_gla[0], t_len=ts_len,
                             nb=tiles.gla_sample_batch)

    w_out = bf(w_out_ab[0])
    hp = _post(x_prompt.reshape(n_p, D_MODEL), pp, 0, post_w(0), mix=(attn_p, o_p, w_out), tm=tiles.post_rows)
    hs = _post(x_sample.reshape(n_s, D_MODEL), ps, 0, post_w(0), mix=(attn_s, o_s, w_out), tm=tiles.sample_rows)

    w_b, w_c, lam = _s5_weights(s5_a_re[0], s5_a_im[0], s5_log_dt[0], s5_b_re[0], s5_b_im[0],
                                s5_c_re[0], s5_c_im[0])
    n_state = S5_GROUPS * S5_STATE
    zero_state = jnp.zeros((bp, n_state), F32)
    hp, s5r_p, s5i_p = _s5(hp.reshape(bp, sp, D_MODEL), zero_state, zero_state, g_norm_c[0], w_b, w_c, lam,
                           s5_d[0], w_glu[0], nb=bp, t_tile=tiles.s5_positions, batch_major=True)
    hs_tm, s5r_s, s5i_s = _s5(_to_time_major(hs, bs, ts_len), state_s5_re[0].reshape(bs, n_state),
                              state_s5_im[0].reshape(bs, n_state), g_norm_c[0], w_b, w_c, lam,
                              s5_d[0], w_glu[0], nb=bs, t_tile=ts_len, batch_major=False)
    hp = _post(hp.reshape(n_p, D_MODEL), pp, 1, post_w(1), tm=tiles.post_rows)
    hs = _post(_to_batch_major(hs_tm, bs, ts_len), ps, 1, post_w(1), tm=tiles.sample_rows)

    st = lambda a, b: a.reshape(1, b, S5_GROUPS, S5_STATE)
    return (hp.reshape(bp, sp, D_MODEL), hs.reshape(bs, ts_len, D_MODEL),
            ckv_p.reshape(1, bp, sp, KV_LORA), kpe_p.reshape(1, bp, sp, ROPE_DIM), gla_p[None],
            st(s5r_p, bp), st(s5i_p, bp),
            ckv_s.reshape(1, bs, ts_len, KV_LORA), kpe_s.reshape(1, bs, ts_len, ROPE_DIM), gla_s[None],
            st(s5r_s, bs), st(s5i_s, bs))
```

```python
import functools
from typing import NamedTuple

import jax
import jax.numpy as jnp
from jax import lax
from jax.experimental import pallas as pl
from jax.experimental.pallas import tpu as pltpu

F32 = jnp.float32
BF16 = jnp.bfloat16

D_MODEL = 1024
PAGE_SIZE = 128
MLA_HEADS = 8
Q_LORA = 384
KV_LORA = 256
NOPE_DIM = 64
ROPE_DIM = 32
QK_DIM = NOPE_DIM + ROPE_DIM
V_DIM = 64
ROPE_THETA = 10000.0
GLA_HEADS = 4
GLA_DK = 128
GLA_DV = 128
GLA_GATE_RANK = 16
GLA_TAU = 16.0
GLA_CHUNK = 64
S5_GROUP = 16
S5_GROUPS = D_MODEL // S5_GROUP
S5_STATE = 64
D_FF = 4 * D_MODEL
PLE_DIM = 256
EPS = 1e-6
IN_SPLITS = (Q_LORA, KV_LORA, ROPE_DIM, GLA_HEADS * GLA_DK, GLA_HEADS * GLA_DK,
             GLA_HEADS * GLA_DV, GLA_GATE_RANK, GLA_HEADS * GLA_DV)

LANES = 128
SUBLANES = 8
HEAD_PAD = LANES
VMEM_LIMIT = 56 * 1024 * 1024
S5_SLABS = D_MODEL // LANES
S5_SLAB_STATE = (LANES // S5_GROUP) * S5_STATE
NEG_INF = float("-inf")
LOG2_E = 1.4426950408889634


class _Tiles(NamedTuple):
    front_rows: int
    post_rows: int
    sample_rows: int
    attn_queries: int
    gla_rows: int
    s5_positions: int
    attn_chunk_pages: int
    gla_sample_batch: int


def _tiles(seq, sample_rows, n_pages):
    pick = lambda n, *sizes: next(s for s in sizes if n % s == 0)
    return _Tiles(front_rows=pick(seq, 512, 256, 128), post_rows=pick(seq, 1024, 512, 256, 128),
                  sample_rows=sample_rows // 2, attn_queries=pick(seq, 512, 256, 128),
                  gla_rows=pick(seq, 1024, 512, 256, 128, GLA_CHUNK), s5_positions=pick(seq, 64, 32, 16, 8),
                  attn_chunk_pages=pick(n_pages, 32, 16, 8, 4, 2, 1), gla_sample_batch=SUBLANES)


def _vmem_spec():
    return pl.BlockSpec(memory_space=pltpu.VMEM)


def _params(*sem):
    return pltpu.CompilerParams(dimension_semantics=sem, vmem_limit_bytes=VMEM_LIMIT)


def _bdot(a, b):
    return jnp.dot(a.astype(BF16), b.astype(BF16), preferred_element_type=F32)


def _bdot_nt(a, b):
    return lax.dot_general(a.astype(BF16), b.astype(BF16), (((1,), (1,)), ((), ())),
                           preferred_element_type=F32)


def _bdot_tn(a, b):
    return lax.dot_general(a.astype(BF16), b.astype(BF16), (((0,), (0,)), ((), ())),
                           preferred_element_type=F32)


def _rms(x, g):
    return x * lax.rsqrt(jnp.mean(x * x, axis=-1, keepdims=True) + EPS) * g


def _sigmoid(x):
    return 1.0 / (1.0 + jnp.exp(-x))


def _rope_table_kernel(cos_ref, sin_ref):
    shape = cos_ref.shape
    half = ROPE_DIM // 2
    per_row = LANES // half
    row = lax.broadcasted_iota(jnp.int32, shape, 0)
    lane = lax.broadcasted_iota(jnp.int32, shape, 1)
    pos = (row * per_row + lane // half).astype(F32)
    freq = jnp.power(F32(ROPE_THETA), -(lane % half).astype(F32) / half)
    ang = pos * freq
    cos_ref[...] = jnp.cos(ang)
    sin_ref[...] = jnp.sin(ang)


def _rope_tables(n_pos):
    half = ROPE_DIM // 2
    per_row = LANES // half
    rows = -(-n_pos // per_row)
    rows = -(-rows // SUBLANES) * SUBLANES
    cos, sin = pl.pallas_call(
        _rope_table_kernel,
        out_shape=(jax.ShapeDtypeStruct((rows, LANES), F32),) * 2,
        name="rope_table",
    )()
    cos = cos.reshape(rows * per_row, half)[:n_pos]
    sin = sin.reshape(rows * per_row, half)[:n_pos]
    return cos, sin


def _front_kernel(x_ref, tc_ref, ts_ref, gn_ref, wcq_ref, wckv_ref, wkpe_ref, wgq_ref, wgk_ref, wgv_ref,
                  wgr_ref, wga_ref, gql_ref, wuq_ref, gkvl_ref, wuk_ref, wuv_ref, gqh_ref, gkh_ref,
                  wa2_ref, ba_ref,
                  q_ref, k_ref, v_ref, ckv_ref, kpe_ref, gq_ref, gk_ref, gv_ref, lg_ref, gr_ref):
    n = _rms(x_ref[...], gn_ref[...])
    nb = n.astype(BF16)
    tc = tc_ref[...]
    ts = ts_ref[...]
    lane = lax.broadcasted_iota(jnp.int32, tc.shape, 1)
    first_half = lane < NOPE_DIM + ROPE_DIM // 2

    def norm_rope(xh, g):
        ss = jnp.sum(xh * xh, axis=-1, keepdims=True) * (1.0 / QK_DIM)
        y = xh * lax.rsqrt(ss + EPS) * g
        partner = jnp.where(first_half, pltpu.roll(y, LANES - ROPE_DIM // 2, 1), pltpu.roll(y, ROPE_DIM // 2, 1))
        return y * tc + partner * ts

    cq = _rms(jnp.dot(nb, wcq_ref[...], preferred_element_type=F32), gql_ref[...])
    q = _bdot(cq, wuq_ref[...])
    ckv = _rms(jnp.dot(nb, wckv_ref[...], preferred_element_type=F32), gkvl_ref[...])
    ckv_ref[...] = ckv
    ckvb = ckv.astype(BF16)
    kn = jnp.dot(ckvb, wuk_ref[...], preferred_element_type=F32)
    v_ref[...] = jnp.dot(ckvb, wuv_ref[...], preferred_element_type=F32).astype(v_ref.dtype)
    kpe = jnp.dot(nb, wkpe_ref[...], preferred_element_type=F32)
    kpe_ref[...] = kpe
    gqh = gqh_ref[...]
    gkh = gkh_ref[...]
    scale = QK_DIM ** -0.5 * LOG2_E
    for h in range(MLA_HEADS):
        hs = slice(h * HEAD_PAD, (h + 1) * HEAD_PAD)
        q_ref[:, hs] = (norm_rope(q[:, hs], gqh) * scale).astype(q_ref.dtype)
        k_ref[:, hs] = norm_rope(kn[:, hs] + kpe, gkh).astype(k_ref.dtype)

    gq_ref[...] = jnp.dot(nb, wgq_ref[...], preferred_element_type=F32) * (GLA_DK ** -0.5)
    gk_ref[...] = jnp.dot(nb, wgk_ref[...], preferred_element_type=F32)
    gv_ref[...] = jnp.dot(nb, wgv_ref[...], preferred_element_type=F32)
    gr_ref[...] = jnp.dot(nb, wgr_ref[...], preferred_element_type=F32)
    ga = jnp.dot(nb, wga_ref[...], preferred_element_type=F32)
    logit = _bdot(ga, wa2_ref[...]) + ba_ref[...]
    log_sig = jnp.minimum(logit, 0.0) - jnp.log1p(jnp.exp(-jnp.abs(logit)))
    lg_ref[...] = log_sig * (1.0 / GLA_TAU)


def _front_weights(w_in, g_norm, g_qlat, w_uq, g_kvlat, w_ukv, g_qh, g_kh, w_a2, b_a):
    offs = [0]
    for s in IN_SPLITS:
        offs.append(offs[-1] + s)
    cols = [w_in[:, offs[i]:offs[i + 1]] for i in range(len(IN_SPLITS))]
    w_cq, w_ckv, w_kpe, w_gq, w_gk, w_gv, w_ga, w_gr = cols
    pad_head = HEAD_PAD - QK_DIM
    w_kpe_placed = jnp.pad(w_kpe, ((0, 0), (NOPE_DIM, HEAD_PAD - QK_DIM)))
    w_ga_pad = jnp.pad(w_ga, ((0, 0), (0, LANES - GLA_GATE_RANK)))
    w_a2_pad = jnp.pad(w_a2, ((0, LANES - GLA_GATE_RANK), (0, 0)))
    w_uq_pad = jnp.pad(w_uq.reshape(Q_LORA, MLA_HEADS, QK_DIM), ((0, 0), (0, 0), (0, pad_head)))
    w_uq_pad = w_uq_pad.reshape(Q_LORA, MLA_HEADS * HEAD_PAD)
    w_ukv3 = w_ukv.reshape(KV_LORA, MLA_HEADS, NOPE_DIM + V_DIM)
    w_uk = w_ukv3[:, :, :NOPE_DIM]
    w_uv = w_ukv3[:, :, NOPE_DIM:].reshape(KV_LORA, MLA_HEADS * V_DIM)
    w_uk_pad = jnp.pad(w_uk, ((0, 0), (0, 0), (0, HEAD_PAD - NOPE_DIM))).reshape(KV_LORA, MLA_HEADS * HEAD_PAD)
    g_qh_pad = jnp.pad(g_qh, (0, pad_head))[None, :]
    g_kh_pad = jnp.pad(g_kh, (0, pad_head))[None, :]
    bf = lambda a: a.astype(BF16)
    weights = (g_norm[None, :], bf(w_cq), bf(w_ckv), bf(w_kpe_placed), bf(w_gq), bf(w_gk), bf(w_gv), bf(w_gr),
               bf(w_ga_pad), g_qlat[None, :], bf(w_uq_pad), g_kvlat[None, :], bf(w_uk_pad), bf(w_uv),
               g_qh_pad, g_kh_pad, bf(w_a2_pad), b_a[None, :])
    return weights, w_uk, w_uv


def _front(x, tc, ts, weights, *, tm, table_tiles):
    n = x.shape[0]
    hk = GLA_HEADS * GLA_DK
    row = lambda w: pl.BlockSpec((tm, w), lambda i: (i, 0))
    tab = pl.BlockSpec((tm, LANES), lambda i: (i % table_tiles, 0))
    sds = lambda w, dt: jax.ShapeDtypeStruct((n, w), dt)
    out_shape = (sds(MLA_HEADS * HEAD_PAD, BF16), sds(MLA_HEADS * HEAD_PAD, BF16), sds(MLA_HEADS * V_DIM, BF16),
                 sds(KV_LORA, F32), sds(LANES, F32), sds(hk, F32), sds(hk, F32), sds(hk, F32), sds(hk, F32),
                 sds(hk, F32))
    out_specs = (row(MLA_HEADS * HEAD_PAD), row(MLA_HEADS * HEAD_PAD), row(MLA_HEADS * V_DIM), row(KV_LORA),
                 row(LANES), row(hk), row(hk), row(hk), row(hk), row(hk))
    return pl.pallas_call(
        _front_kernel,
        grid=(n // tm,),
        in_specs=[row(D_MODEL), tab, tab] + [_vmem_spec()] * len(weights),
        out_specs=out_specs,
        out_shape=out_shape,
        compiler_params=_params("parallel"),
        name="ab_front",
    )(x, tc, ts, *weights)


def _prompt_attn_kernel(q_ref, k_ref, v_ref, o_ref, *, tq, n_q):
    i = pl.program_id(1)
    row = lax.broadcasted_iota(jnp.int32, (tq, tq), 0)
    col = lax.broadcasted_iota(jnp.int32, (tq, tq), 1)
    causal = col <= row
    lane = lax.broadcasted_iota(jnp.int32, (tq, LANES), 1)

    def case(c):
        kv = (c + 1) * tq
        outs = []
        for h in range(MLA_HEADS):
            hs = slice(h * HEAD_PAD, (h + 1) * HEAD_PAD)
            vs = slice((h // 2) * LANES, (h // 2 + 1) * LANES)
            s = lax.dot_general(q_ref[:, hs], k_ref[:kv, hs], (((1,), (1,)), ((), ())),
                                preferred_element_type=F32)
            diag = jnp.where(causal, s[:, c * tq:], NEG_INF)
            s = diag if c == 0 else jnp.concatenate([s[:, :c * tq], diag], axis=1)
            p = jnp.exp2(s - jnp.max(s, axis=-1, keepdims=True))
            l = jnp.sum(p, axis=-1, keepdims=True)
            outs.append(jnp.dot(p.astype(BF16), v_ref[:kv, vs], preferred_element_type=F32) / l)
        for pair in range(MLA_HEADS // 2):
            o_ref[:, pair * LANES:(pair + 1) * LANES] = jnp.where(
                lane < V_DIM, outs[2 * pair], outs[2 * pair + 1]).astype(o_ref.dtype)

    for c in range(n_q):
        pl.when(i == c)(functools.partial(case, c))


def _prompt_attn(q, k, v, *, batch, seq, tq):
    nq = seq // tq
    return pl.pallas_call(
        functools.partial(_prompt_attn_kernel, tq=tq, n_q=nq),
        grid=(batch, nq),
        in_specs=[pl.BlockSpec((tq, MLA_HEADS * HEAD_PAD), lambda b, i: (b * nq + i, 0)),
                  pl.BlockSpec((seq, MLA_HEADS * HEAD_PAD), lambda b, i: (b, 0)),
                  pl.BlockSpec((seq, MLA_HEADS * V_DIM), lambda b, i: (b, 0))],
        out_specs=pl.BlockSpec((tq, MLA_HEADS * V_DIM), lambda b, i: (b * nq + i, 0)),
        out_shape=jax.ShapeDtypeStruct((batch * seq, MLA_HEADS * V_DIM), BF16),
        compiler_params=_params("parallel", "arbitrary"),
        name="mla_prompt_attn",
    )(q, k, v)


def _gla_intra(q, k, b):
    c = q.shape[0]
    row = lax.broadcasted_iota(jnp.int32, q.shape, 0)
    r2 = lax.broadcasted_iota(jnp.int32, (c, c), 0)
    c2 = lax.broadcasted_iota(jnp.int32, (c, c), 1)
    att = jnp.where(r2 == c2, _bdot_nt(q, k), 0.0)
    before = pltpu.roll(b, 1, 0)
    last = b
    m = 1
    while m < c:
        upper = (row // m) % 2 == 1
        qm = jnp.where(upper, q * jnp.exp(b - before), 0.0)
        km = jnp.where(upper, 0.0, k * jnp.exp(last - b))
        same = (r2 // (2 * m)) == (c2 // (2 * m))
        att = att + jnp.where(same, _bdot_nt(qm, km), 0.0)
        if 2 * m < c:
            before = jnp.where(upper, pltpu.roll(before, m, 0), before)
            last = jnp.where(upper, last, pltpu.roll(last, c - m, 0))
        m *= 2
    return att


def _cumsum_rows(g):
    c = g.shape[0]
    row = lax.broadcasted_iota(jnp.int32, g.shape, 0)
    b = g
    sh = 1
    while sh < c:
        b = b + jnp.where(row >= sh, pltpu.roll(b, sh, 0), 0.0)
        sh *= 2
    return b


def _gla_prompt_kernel(gq_ref, gk_ref, gv_ref, lg_ref, gr_ref, go_ref, o_ref, st_ref, state_ref, *, n_chunks):
    t = pl.program_id(1)

    @pl.when(t == 0)
    def _():
        state_ref[...] = jnp.zeros_like(state_ref)

    go = go_ref[...]

    for h in range(GLA_HEADS):
        hs = slice(h * GLA_DK, (h + 1) * GLA_DK)
        st = state_ref[h]
        for ci in range(n_chunks):
            rows = slice(ci * GLA_CHUNK, (ci + 1) * GLA_CHUNK)
            q = gq_ref[rows, hs]
            k = gk_ref[rows, hs]
            v = gv_ref[rows, hs]
            b = _cumsum_rows(lg_ref[rows, hs])
            o = _bdot_nt(q * jnp.exp(b), st) + _bdot(_gla_intra(q, k, b), v)
            bl = b[GLA_CHUNK - 1:GLA_CHUNK, :]
            st = st * jnp.exp(bl) + _bdot_tn(v, k * jnp.exp(bl - b))
            gr = gr_ref[rows, hs]
            o_ref[rows, hs] = (_rms(o, go) * (gr * _sigmoid(gr))).astype(o_ref.dtype)
        state_ref[h] = st

    @pl.when(t == pl.num_programs(1) - 1)
    def _():
        st_ref[0] = state_ref[...]


def _gla_prompt(gq, gk, gv, lg, gr, g_o, *, batch, seq, tg):
    nt = seq // tg
    hk = GLA_HEADS * GLA_DK
    row = pl.BlockSpec((tg, hk), lambda b, t: (b * nt + t, 0))
    return pl.pallas_call(
        functools.partial(_gla_prompt_kernel, n_chunks=tg // GLA_CHUNK),
        grid=(batch, nt),
        in_specs=[row, row, row, row, row, _vmem_spec()],
        out_specs=(row, pl.BlockSpec((1, GLA_HEADS, GLA_DV, GLA_DK), lambda b, t: (b, 0, 0, 0))),
        out_shape=(jax.ShapeDtypeStruct((batch * seq, hk), BF16),
                   jax.ShapeDtypeStruct((batch, GLA_HEADS, GLA_DV, GLA_DK), F32)),
        scratch_shapes=[pltpu.VMEM((GLA_HEADS, GLA_DV, GLA_DK), F32)],
        compiler_params=_params("parallel", "arbitrary"),
        name="gla_prompt",
    )(gq, gk, gv, lg, gr, g_o[None, :])


def _gla_sample_kernel(gq_ref, gk_ref, gv_ref, lg_ref, gr_ref, go_ref, s0_ref, o_ref, s1_ref, *, nb, t_len):
    rows = nb * t_len
    per_tile = SUBLANES // t_len
    tok = lax.broadcasted_iota(jnp.int32, (rows, GLA_DK), 0) % t_len
    tile_batch = lax.broadcasted_iota(jnp.int32, (SUBLANES, GLA_DK), 0) // t_len
    pad = jnp.zeros((LANES - rows, LANES), F32)
    go = go_ref[...]
    for h in range(GLA_HEADS):
        hs = slice(h * GLA_DK, (h + 1) * GLA_DK)
        q = gq_ref[:, hs]
        k = gk_ref[:, hs]
        v = gv_ref[:, hs]
        b = lg_ref[:, hs]
        sh = 1
        while sh < t_len:
            b = b + jnp.where(tok >= sh, pltpu.roll(b, sh, 0), 0.0)
            sh *= 2
        bl = jnp.where(tok == t_len - 1, b, 0.0)
        for d in range(1, t_len):
            bl = bl + jnp.where(tok == t_len - 1 - d, pltpu.roll(b, rows - d, 0), 0.0)
        qe = q * jnp.exp(b)
        kd = k * jnp.exp(bl - b)
        o = jnp.sum(q * k, axis=-1, keepdims=True) * v
        for d in range(1, t_len):
            w = jnp.sum(q * pltpu.roll(k, d, 0) * jnp.exp(b - pltpu.roll(b, d, 0)), axis=-1, keepdims=True)
            o = o + jnp.where(tok[:, :1] >= d, w, 0.0) * pltpu.roll(v, d, 0)
        decay_cols = jnp.transpose(jnp.concatenate([jnp.exp(bl), pad], axis=0))
        inter = []
        for tile in range(rows // SUBLANES):
            ts = slice(tile * SUBLANES, (tile + 1) * SUBLANES)
            acc = jnp.zeros((SUBLANES, GLA_DV), F32)
            for e in range(per_tile):
                bi = tile * per_tile + e
                s0 = s0_ref[bi, h]
                mine = tile_batch == e
                acc = jnp.where(mine, _bdot(qe[ts], s0), acc)
                r_last = bi * t_len + t_len - 1
                s1_ref[bi, h] = s0 * decay_cols[:, r_last:r_last + 1] + _bdot_tn(jnp.where(mine, kd[ts], 0.0), v[ts])
            inter.append(acc)
        o = o + jnp.concatenate(inter, axis=0)
        gr = gr_ref[:, hs]
        o_ref[:, hs] = (_rms(o, go) * (gr * _sigmoid(gr))).astype(o_ref.dtype)


def _gla_sample(gq, gk, gv, lg, gr, g_o, s0, *, t_len, nb):
    n = gq.shape[0]
    batch = n // t_len
    hk = GLA_HEADS * GLA_DK
    rows = nb * t_len
    row = pl.BlockSpec((rows, hk), lambda i: (i, 0))
    st = pl.BlockSpec((nb, GLA_HEADS, GLA_DK, GLA_DV), lambda i: (i, 0, 0, 0))
    return pl.pallas_call(
        functools.partial(_gla_sample_kernel, nb=nb, t_len=t_len),
        grid=(batch // nb,),
        in_specs=[row, row, row, row, row, _vmem_spec(), st],
        out_specs=(row, st),
        out_shape=(jax.ShapeDtypeStruct((n, hk), BF16), jax.ShapeDtypeStruct(s0.shape, s0.dtype)),
        compiler_params=_params("parallel"),
        name="gla_sample",
    )(gq, gk, gv, lg, gr, g_o[None, :], s0)


def _sample_attn_kernel(pt_ref, q_ref, cnew_ref, pnew_ref, kc_ref, ks_ref, kcn_ref, ksn_ref, gk_ref, gpe_ref,
                        wukt_ref, wukp_ref, wuv_ref, ckv_hbm, kpe_hbm, o_ref, ckv_buf, kpe_buf, sem,
                        *, n_pages, chunk_pages, t_len):
    n_cols = t_len * MLA_HEADS
    chunk = chunk_pages * PAGE_SIZE
    n_chunks = n_pages // chunk_pages
    past = n_pages * PAGE_SIZE
    b = pl.program_id(0)
    slot = b % 2

    def page_copies(batch, sl, i):
        page = pt_ref[batch, i]
        ci = i // chunk_pages
        return (pltpu.make_async_copy(ckv_hbm.at[page], ckv_buf.at[sl, i], sem.at[sl, ci, 0]),
                pltpu.make_async_copy(kpe_hbm.at[page], kpe_buf.at[sl, i], sem.at[sl, ci, 1]))

    def start_pages(batch, sl, pages):
        for i in pages:
            for cp in page_copies(batch, sl, i):
                cp.start()

    def wait_pages(batch, sl, pages):
        for i in pages:
            for cp in page_copies(batch, sl, i):
                cp.wait()

    @pl.when(b == 0)
    def _():
        start_pages(0, 0, range(n_pages))

    qg = q_ref[...].astype(F32) * gk_ref[...]
    gpe = gpe_ref[...]
    sub = lax.broadcasted_iota(jnp.int32, (SUBLANES, MLA_HEADS * HEAD_PAD), 0)
    lane_head = lax.broadcasted_iota(jnp.int32, (SUBLANES, MLA_HEADS * HEAD_PAD), 1) // HEAD_PAD
    qexp = jnp.concatenate(
        [jnp.where(lane_head == sub, jnp.broadcast_to(qg[t:t + 1, :], sub.shape), 0.0) for t in range(t_len)],
        axis=0)
    qt = _bdot(qexp, wukt_ref[...])
    lhs = jnp.concatenate([wukp_ref[...], qt[:, :KV_LORA].astype(BF16)], axis=0)
    qr = qt[:, KV_LORA:KV_LORA + ROPE_DIM].astype(BF16)
    n_norm = NOPE_DIM * MLA_HEADS
    half = ROPE_DIM // 2

    def scores(c, kt, kcos, ksin):
        n = c.shape[0]
        cb = c.astype(BF16)
        r = lax.dot_general(lhs, cb, (((1,), (1,)), ((), ())), preferred_element_type=F32)
        kn = r[:n_norm].reshape(NOPE_DIM, MLA_HEADS, n)
        ss = jnp.sum(kn * kn, axis=0) + jnp.sum(kt * kt, axis=0, keepdims=True)
        rs = lax.rsqrt(ss * (1.0 / QK_DIM) + EPS)
        kg = kt * gpe
        kr = kg * kcos + jnp.concatenate([kg[half:], kg[:half]], axis=0) * ksin
        s = r[n_norm:] + jnp.dot(qr, kr.astype(BF16), preferred_element_type=F32)
        s = (s.reshape(t_len, MLA_HEADS, n) * rs[None]).reshape(n_cols, n)
        return cb, s

    def fold(state, cb, s):
        m, l, acc = state
        m_new = jnp.maximum(m, jnp.max(s, axis=1, keepdims=True))
        alpha = jnp.exp2(m - m_new)
        p = jnp.exp2(s - m_new)
        l = alpha * l + jnp.sum(p, axis=1, keepdims=True)
        acc = alpha * acc + jnp.dot(p.astype(BF16), cb, preferred_element_type=F32)
        return m_new, l, acc

    @pl.when(b + 1 < pl.num_programs(0))
    def _():
        start_pages(b + 1, 1 - slot, range(n_pages))

    wait_pages(b, slot, range(n_pages))
    state = (jnp.full((n_cols, 1), NEG_INF, F32), jnp.zeros((n_cols, 1), F32), jnp.zeros((n_cols, KV_LORA), F32))
    for ci in range(n_chunks):
        pages = range(ci * chunk_pages, (ci + 1) * chunk_pages)
        c = ckv_buf[slot, pl.ds(ci * chunk_pages, chunk_pages)].reshape(chunk, KV_LORA)
        kt = jnp.concatenate([kpe_buf[slot, p] for p in pages], axis=1)
        keys = slice(ci * chunk, (ci + 1) * chunk)
        state = fold(state, *scores(c, kt, kc_ref[:, keys], ks_ref[:, keys]))

    zpad = lambda a: jnp.concatenate([a, jnp.zeros((PAGE_SIZE - t_len, a.shape[1]), a.dtype)], axis=0)
    pnew = jnp.concatenate([zpad(pnew_ref[...]), jnp.zeros((PAGE_SIZE, LANES - ROPE_DIM), F32)], axis=1)
    cb, s = scores(zpad(cnew_ref[...]), jnp.transpose(pnew)[:ROPE_DIM], kcn_ref[...], ksn_ref[...])
    r2 = lax.broadcasted_iota(jnp.int32, (n_cols, PAGE_SIZE), 0)
    c2 = lax.broadcasted_iota(jnp.int32, (n_cols, PAGE_SIZE), 1)
    s = jnp.where((c2 < t_len) & (c2 <= r2 // MLA_HEADS), s, NEG_INF)
    _, l, acc = fold(state, cb, s)

    lat = acc / l
    res = _bdot(lat, wuv_ref[...])
    r3 = lax.broadcasted_iota(jnp.int32, res.shape, 0)
    c3 = lax.broadcasted_iota(jnp.int32, res.shape, 1)
    res = jnp.where(c3 // V_DIM == r3 % MLA_HEADS, res, 0.0)
    o_ref[...] = jnp.sum(res.reshape(t_len, MLA_HEADS, MLA_HEADS * V_DIM), axis=1).astype(o_ref.dtype)


def _sample_attn(q, ckv_new, kpe_new, cache_ckv, cache_kpe, page_table, kc, ks, g_kh, w_uk, w_uv,
                 *, t_len, chunk_pages):
    batch, n_pages = page_table.shape
    past = n_pages * PAGE_SIZE
    w_ukt = jnp.pad(jnp.transpose(w_uk, (1, 2, 0)), ((0, 0), (0, HEAD_PAD - NOPE_DIM), (0, 0)))
    w_ukt = w_ukt.reshape(MLA_HEADS * HEAD_PAD, KV_LORA)
    sel = jnp.pad(jnp.eye(ROPE_DIM, dtype=F32), ((NOPE_DIM, HEAD_PAD - QK_DIM), (0, LANES - ROPE_DIM)))
    w_ukt_ext = jnp.concatenate([w_ukt, jnp.tile(sel, (MLA_HEADS, 1))], axis=1).astype(BF16)
    w_uk_perm = jnp.transpose(w_uk, (2, 1, 0)).reshape(NOPE_DIM * MLA_HEADS, KV_LORA).astype(BF16)
    g_k_pad = jnp.concatenate([g_kh[:NOPE_DIM], jnp.ones((ROPE_DIM,), F32), jnp.zeros((HEAD_PAD - QK_DIM,), F32)])
    g_k_pad = jnp.tile(g_k_pad, MLA_HEADS)[None, :]
    g_pe = g_kh[NOPE_DIM:][:, None]
    kc, ks = jnp.transpose(kc), jnp.transpose(ks)
    kcn = jnp.pad(kc[:, past:past + t_len], ((0, 0), (0, PAGE_SIZE - t_len)))
    ksn = jnp.pad(ks[:, past:past + t_len], ((0, 0), (0, PAGE_SIZE - t_len)))
    per_b = lambda w: pl.BlockSpec((None, t_len, w), lambda b, pt: (b, 0, 0))
    hbm = pl.BlockSpec(memory_space=pl.ANY)
    in_specs = ([per_b(MLA_HEADS * HEAD_PAD), per_b(KV_LORA), per_b(ROPE_DIM)] + [_vmem_spec()] * 9 + [hbm, hbm])
    grid_spec = pltpu.PrefetchScalarGridSpec(
        num_scalar_prefetch=1,
        grid=(batch,),
        in_specs=in_specs,
        out_specs=pl.BlockSpec((None, t_len, MLA_HEADS * V_DIM), lambda b, pt: (b, 0, 0)),
        scratch_shapes=[pltpu.VMEM((2, n_pages, PAGE_SIZE, KV_LORA), F32),
                        pltpu.VMEM((2, n_pages, ROPE_DIM, PAGE_SIZE), F32),
                        pltpu.SemaphoreType.DMA((2, n_pages // chunk_pages, 2))],
    )
    out = pl.pallas_call(
        functools.partial(_sample_attn_kernel, n_pages=n_pages, chunk_pages=chunk_pages, t_len=t_len),
        grid_spec=grid_spec,
        out_shape=jax.ShapeDtypeStruct((batch, t_len, MLA_HEADS * V_DIM), BF16),
        compiler_params=_params("arbitrary"),
        name="mla_sample_attn",
    )(page_table, q.reshape(batch, t_len, -1), ckv_new.reshape(batch, t_len, -1),
      kpe_new.reshape(batch, t_len, -1), kc[:, :past], ks[:, :past], kcn, ksn, g_k_pad, g_pe,
      w_ukt_ext, w_uk_perm, w_uv.astype(BF16), cache_ckv, jnp.swapaxes(cache_kpe, 1, 2))
    return out.reshape(batch * t_len, MLA_HEADS * V_DIM)


def _mlp_ple(h, p_ref, gm_ref, wup_ref, wdown_ref, gp_ref, wgate_ref, wproj_ref, o_ref, ff_chunk):
    nb = _rms(h, gm_ref[...]).astype(BF16)
    y = jnp.zeros_like(h)
    for c in range(D_FF // ff_chunk):
        cs = slice(c * ff_chunk, (c + 1) * ff_chunk)
        a = jnp.maximum(jnp.dot(nb, wup_ref[:, cs], preferred_element_type=F32), 0.0)
        y = y + jnp.dot((a * a).astype(BF16), wdown_ref[cs, :], preferred_element_type=F32)
    h = h + y
    gate = _sigmoid(_bdot(_rms(h, gp_ref[...]), wgate_ref[...]))
    o_ref[...] = h + gate * _bdot(p_ref[...], wproj_ref[...])


def _post_mix_kernel(h_ref, a_ref, o2_ref, p_ref, wout_ref, gm_ref, wup_ref, wdown_ref, gp_ref, wgate_ref,
                     wproj_ref, o_ref, *, ff_chunk):
    n_attn = MLA_HEADS * V_DIM
    h = (h_ref[...] + jnp.dot(a_ref[...], wout_ref[:n_attn, :], preferred_element_type=F32)
         + jnp.dot(o2_ref[...], wout_ref[n_attn:, :], preferred_element_type=F32))
    _mlp_ple(h, p_ref, gm_ref, wup_ref, wdown_ref, gp_ref, wgate_ref, wproj_ref, o_ref, ff_chunk)


def _post_kernel(h_ref, p_ref, gm_ref, wup_ref, wdown_ref, gp_ref, wgate_ref, wproj_ref, o_ref, *, ff_chunk):
    _mlp_ple(h_ref[...], p_ref, gm_ref, wup_ref, wdown_ref, gp_ref, wgate_ref, wproj_ref, o_ref, ff_chunk)


def _post(h, p_layers, layer, weights, mix=None, *, tm, ff_chunk=1024):
    n = h.shape[0]
    row = lambda w: pl.BlockSpec((tm, w), lambda i: (i, 0))
    p_spec = pl.BlockSpec((None, tm, PLE_DIM), lambda i: (layer, i, 0))
    if mix is None:
        body, acts, specs = _post_kernel, (h, p_layers), [row(D_MODEL), p_spec]
    else:
        attn, o_gla, w_out = mix
        body, acts = _post_mix_kernel, (h, attn, o_gla, p_layers)
        specs = [row(D_MODEL), row(attn.shape[1]), row(o_gla.shape[1]), p_spec]
        weights = (w_out,) + tuple(weights)
    return pl.pallas_call(
        functools.partial(body, ff_chunk=ff_chunk),
        grid=(n // tm,),
        in_specs=specs + [_vmem_spec()] * len(weights),
        out_specs=row(D_MODEL),
        out_shape=jax.ShapeDtypeStruct((n, D_MODEL), F32),
        compiler_params=_params("parallel"),
        name="post_mix" if mix is not None else "post",
    )(*acts, *weights)


def _s5_prep_kernel(are_ref, aim_ref, ldt_ref, bre_ref, bim_ref, lr_ref, li_ref, bbr_ref, bbi_ref):
    a_re = are_ref[...]
    a_im = aim_ref[...]
    dt = jnp.exp(ldt_ref[...])
    mag = jnp.exp(a_re * dt)
    lr = mag * jnp.cos(a_im * dt)
    li = mag * jnp.sin(a_im * dt)
    den = a_re * a_re + a_im * a_im
    cr = ((lr - 1.0) * a_re + li * a_im) / den
    ci = (li * a_re - (lr - 1.0) * a_im) / den
    lr_ref[...] = lr
    li_ref[...] = li
    b_re = bre_ref[...]
    b_im = bim_ref[...]
    bbr_ref[...] = cr[:, None, :] * b_re - ci[:, None, :] * b_im
    bbi_ref[...] = cr[:, None, :] * b_im + ci[:, None, :] * b_re


def _s5_weights(a_re, a_im, log_dt, b_re, b_im, c_re, c_im):
    g, p_, ch = S5_GROUPS, S5_STATE, S5_GROUP
    gp = jax.ShapeDtypeStruct((g, p_), F32)
    gcp = jax.ShapeDtypeStruct((g, ch, p_), F32)
    lr, li, bbr, bbi = pl.pallas_call(
        _s5_prep_kernel, out_shape=(gp, gp, gcp, gcp), name="s5_prep",
    )(a_re, a_im, log_dt[:, None], jnp.transpose(b_re, (0, 2, 1)), jnp.transpose(b_im, (0, 2, 1)))
    per_slab = LANES // ch
    eye = jnp.eye(per_slab, dtype=F32)

    def b_blocks(bb):
        bb = bb.reshape(S5_SLABS, per_slab, ch, p_)
        return jnp.einsum('jgcp,gh->jgchp', bb, eye).reshape(S5_SLABS, LANES, S5_SLAB_STATE)

    def c_blocks(cc):
        cc = cc.reshape(S5_SLABS, per_slab, ch, p_)
        return jnp.einsum('jgcp,gh->jgphc', cc, eye).reshape(S5_SLABS, S5_SLAB_STATE, LANES)

    w_b = jnp.concatenate([b_blocks(bbr), b_blocks(bbi)], axis=2).astype(BF16)
    w_c = jnp.concatenate([c_blocks(c_re), -c_blocks(c_im)], axis=1).astype(BF16)
    lam = jnp.stack([lr.reshape(S5_SLABS, S5_SLAB_STATE), li.reshape(S5_SLABS, S5_SLAB_STATE)], axis=1)
    return w_b, w_c, lam


def _s5_kernel(x_ref, x0r_ref, x0i_ref, gn_ref, wb_ref, wc_ref, lam_ref, d_ref, wglu_ref,
               o_ref, xr_ref, xi_ref, bur_ref, bui_ref, sr_ref, si_ref, shuf_ref=None, *, nb, t_len):
    step = pl.program_id(0)
    ns = S5_SLAB_STATE

    @pl.when(step == 0)
    def _():
        for j in range(S5_SLABS):
            sr_ref[j] = x0r_ref[:, j * ns:(j + 1) * ns]
            si_ref[j] = x0i_ref[:, j * ns:(j + 1) * ns]

    if shuf_ref is None:
        x = x_ref[...]
    else:
        for bi in range(nb):
            for s in range(D_MODEL // LANES):
                shuf_ref[s, pl.ds(bi, t_len, stride=nb), :] = x_ref[bi, :, s * LANES:(s + 1) * LANES]
        x = jnp.concatenate([shuf_ref[s] for s in range(D_MODEL // LANES)], axis=1)
    u = _rms(x, gn_ref[...])
    ub = u.astype(BF16)
    def b_proj(j):
        bu = jnp.dot(ub[:, j * LANES:(j + 1) * LANES], wb_ref[j], preferred_element_type=F32)
        bur_ref[j] = bu[:, :ns]
        bui_ref[j] = bu[:, ns:]

    def c_proj(j):
        xs = jnp.concatenate([bur_ref[j].astype(BF16), bui_ref[j].astype(BF16)], axis=1)
        return jnp.dot(xs, wc_ref[j], preferred_element_type=F32)

    ys = []
    b_proj(0)
    for j in range(S5_SLABS):
        if j + 1 < S5_SLABS:
            b_proj(j + 1)
        if j >= 1:
            ys.append(c_proj(j - 1))
        lr = jnp.broadcast_to(lam_ref[j, 0:1, :], (SUBLANES, ns))
        li = jnp.broadcast_to(lam_ref[j, 1:2, :], (SUBLANES, ns))
        for gi in range(nb // SUBLANES):
            srow = slice(gi * SUBLANES, (gi + 1) * SUBLANES)
            xr = sr_ref[j, srow, :]
            xi = si_ref[j, srow, :]
            for t in range(t_len):
                rows = slice(t * nb + gi * SUBLANES, t * nb + (gi + 1) * SUBLANES)
                xr, xi = (lr * xr - li * xi + bur_ref[j, rows, :], lr * xi + li * xr + bui_ref[j, rows, :])
                bur_ref[j, rows, :] = xr
                bui_ref[j, rows, :] = xi
            sr_ref[j, srow, :] = xr
            si_ref[j, srow, :] = xi
    ys.append(c_proj(S5_SLABS - 1))
    y = jnp.concatenate(ys, axis=1) + d_ref[...] * u
    gl = jax.nn.gelu(y)
    zg = _bdot(gl, wglu_ref[...])
    h = x + zg[:, :D_MODEL] * _sigmoid(zg[:, D_MODEL:])
    if shuf_ref is None:
        o_ref[...] = h
    else:
        for s in range(D_MODEL // LANES):
            shuf_ref[s] = h[:, s * LANES:(s + 1) * LANES]
        for bi in range(nb):
            for s in range(D_MODEL // LANES):
                o_ref[bi, :, s * LANES:(s + 1) * LANES] = shuf_ref[s, pl.ds(bi, t_len, stride=nb), :]

    @pl.when(step == pl.num_programs(0) - 1)
    def _():
        for j in range(S5_SLABS):
            xr_ref[:, j * ns:(j + 1) * ns] = sr_ref[j]
            xi_ref[:, j * ns:(j + 1) * ns] = si_ref[j]


def _s5(x, x0r, x0i, g_norm, w_b, w_c, lam, d, w_glu, *, nb, t_tile, batch_major):
    n = x.shape[0] * x.shape[1] if batch_major else x.shape[0]
    rows = nb * t_tile
    n_state = S5_GROUPS * S5_STATE
    full = lambda shape: pl.BlockSpec(shape, lambda i: (0,) * len(shape))
    st = jax.ShapeDtypeStruct((nb, n_state), F32)
    scratch = [pltpu.VMEM((S5_SLABS, rows, S5_SLAB_STATE), F32),
               pltpu.VMEM((S5_SLABS, rows, S5_SLAB_STATE), F32),
               pltpu.VMEM((S5_SLABS, nb, S5_SLAB_STATE), F32),
               pltpu.VMEM((S5_SLABS, nb, S5_SLAB_STATE), F32)]
    if batch_major:
        x_spec = out_spec = pl.BlockSpec((nb, t_tile, D_MODEL), lambda i: (0, i, 0))
        out_sds = jax.ShapeDtypeStruct((nb, n // nb, D_MODEL), F32)
        scratch.append(pltpu.VMEM((D_MODEL // LANES, rows, LANES), F32))
    else:
        x_spec = out_spec = pl.BlockSpec((rows, D_MODEL), lambda i: (i, 0))
        out_sds = jax.ShapeDtypeStruct((n, D_MODEL), F32)
    return pl.pallas_call(
        functools.partial(_s5_kernel, nb=nb, t_len=t_tile),
        grid=(n // rows,),
        in_specs=[x_spec, full((nb, n_state)), full((nb, n_state))] + [_vmem_spec()] * 6,
        out_specs=(out_spec, full((nb, n_state)), full((nb, n_state))),
        out_shape=(out_sds, st, st),
        scratch_shapes=scratch,
        compiler_params=_params("arbitrary"),
        name="s5_layer",
    )(x, x0r, x0i, g_norm[None, :], w_b, w_c, lam, d[None, :], w_glu.astype(BF16))


def _to_time_major(h, batch, seq):
    return h.reshape(batch, seq, -1).transpose(1, 0, 2).reshape(batch * seq, -1)


def _to_batch_major(h, batch, seq):
    return h.reshape(seq, batch, -1).transpose(1, 0, 2).reshape(batch * seq, -1)


def kernel(x_prompt, x_sample, cache_ckv, cache_kpe, state_gla, state_s5_re, state_s5_im, page_table,
           p_prompt, p_sample, g_norm_ab, w_in_ab, g_qlat, w_uq, g_kvlat, w_ukv, g_qh, g_kh,
           w_gla_a2, b_gla_a, g_gla_o, w_out_ab, g_norm_c, s5_a_re, s5_a_im, s5_log_dt,
           s5_b_re, s5_b_im, s5_c_re, s5_c_im, s5_d, w_glu, g_norm_mlp, w_up, w_down,
           g_norm_ple, w_ple_gate, w_ple_proj):
    bp, sp, _ = x_prompt.shape
    bs, ts_len, _ = x_sample.shape
    n_pages = page_table.shape[1]
    past = n_pages * PAGE_SIZE
    n_p, n_s = bp * sp, bs * ts_len
    bf = lambda a: a.astype(BF16)

    tiles = _tiles(sp, n_s, n_pages)
    n_layers = p_prompt.shape[0]
    pp = p_prompt.reshape(n_layers, n_p, PLE_DIM)
    ps = p_sample.reshape(n_layers, n_s, PLE_DIM)

    cos16, sin16 = _rope_tables(past + ts_len)
    kc = jnp.concatenate([cos16, cos16], axis=1)
    ks = jnp.concatenate([-sin16, sin16], axis=1)

    def head_tables(rows):
        c, s = cos16[rows], sin16[rows]
        ones = jnp.ones((c.shape[0], NOPE_DIM), F32)
        tail = jnp.zeros((c.shape[0], HEAD_PAD - QK_DIM), F32)
        return (jnp.concatenate([ones, c, c, tail + 1.0], axis=1),
                jnp.concatenate([ones * 0.0, -s, s, tail], axis=1))

    tc, ts = head_tables(slice(0, sp))
    tc_s, ts_s = (jnp.tile(a, (bs, 1)) for a in head_tables(slice(past, past + ts_len)))

    post_w = lambda i: (g_norm_mlp[i][None, :], bf(w_up[i]), bf(w_down[i]), g_norm_ple[i][None, :],
                        bf(w_ple_gate[i]), bf(w_ple_proj[i]))

    fw, w_uk, w_uv = _front_weights(w_in_ab[0], g_norm_ab[0], g_qlat[0], w_uq[0], g_kvlat[0], w_ukv[0],
                                    g_qh[0], g_kh[0], w_gla_a2[0], b_gla_a[0])
    (q_p, k_p, v_p, ckv_p, kpe_p, gq_p, gk_p, gv_p, lg_p, gr_p) = _front(
        x_prompt.reshape(n_p, D_MODEL), tc, ts, fw, tm=tiles.front_rows, table_tiles=sp // tiles.front_rows)
    (q_s, _, _, ckv_s, kpe_s, gq_s, gk_s, gv_s, lg_s, gr_s) = _front(
        x_sample.reshape(n_s, D_MODEL), tc_s, ts_s, fw, tm=tiles.sample_rows, table_tiles=n_s // tiles.sample_rows)
    kpe_p = kpe_p[:, NOPE_DIM:QK_DIM]
    kpe_s = kpe_s[:, NOPE_DIM:QK_DIM]

    attn_p = _prompt_attn(q_p, k_p, v_p, batch=bp, seq=sp, tq=tiles.attn_queries)
    o_p, gla_p = _gla_prompt(gq_p, gk_p, gv_p, lg_p, gr_p, g_gla_o[0], batch=bp, seq=sp, tg=tiles.gla_rows)
    gla_p = jnp.swapaxes(gla_p, -1, -2)

    attn_s = _sample_attn(q_s, ckv_s, kpe_s, cache_ckv[0], cache_kpe[0], page_table, kc, ks, g_kh[0],
                          w_uk, w_uv, t_len=ts_len, chunk_pages=tiles.attn_chunk_pages)
    o_s, gla_s = _gla_sample(gq_s, gk_s, gv_s, lg_s, gr_s, g_gla_o[0], state_gla[0], t_len=ts_len,
                             nb=tiles.gla_sample_batch)

    w_out = bf(w_out_ab[0])
    hp = _post(x_prompt.reshape(n_p, D_MODEL), pp, 0, post_w(0), mix=(attn_p, o_p, w_out), tm=tiles.post_rows)
    hs = _post(x_sample.reshape(n_s, D_MODEL), ps, 0, post_w(0), mix=(attn_s, o_s, w_out), tm=tiles.sample_rows)

    w_b, w_c, lam = _s5_weights(s5_a_re[0], s5_a_im[0], s5_log_dt[0], s5_b_re[0], s5_b_im[0],
                                s5_c_re[0], s5_c_im[0])
    n_state = S5_GROUPS * S5_STATE
    zero_state = jnp.zeros((bp, n_state), F32)
    hp, s5r_p, s5i_p = _s5(hp.reshape(bp, sp, D_MODEL), zero_state, zero_state, g_norm_c[0], w_b, w_c, lam,
                           s5_d[0], w_glu[0], nb=bp, t_tile=tiles.s5_positions, batch_major=True)
    hs_tm, s5r_s, s5i_s = _s5(_to_time_major(hs, bs, ts_len), state_s5_re[0].reshape(bs, n_state),
                              state_s5_im[0].reshape(bs, n_state), g_norm_c[0], w_b, w_c, lam,
                              s5_d[0], w_glu[0], nb=bs, t_tile=ts_len, batch_major=False)
    hp = _post(hp.reshape(n_p, D_MODEL), pp, 1, post_w(1), tm=tiles.post_rows)
    hs = _post(_to_batch_major(hs_tm, bs, ts_len), ps, 1, post_w(1), tm=tiles.sample_rows)

    st = lambda a, b: a.reshape(1, b, S5_GROUPS, S5_STATE)
    return (hp.reshape(bp, sp, D_MODEL), hs.reshape(bs, ts_len, D_MODEL),
            ckv_p.reshape(1, bp, sp, KV_LORA), kpe_p.reshape(1, bp, sp, ROPE_DIM), gla_p[None],
            st(s5r_p, bp), st(s5i_p, bp),
            ckv_s.reshape(1, bs, ts_len, KV_LORA), kpe_s.reshape(1, bs, ts_len, ROPE_DIM), gla_s[None],
            st(s5r_s, bs), st(s5i_s, bs))
```

```python
import functools
from typing import NamedTuple

import jax
import jax.numpy as jnp
from jax import lax
from jax.experimental import pallas as pl
from jax.experimental.pallas import tpu as pltpu

F32 = jnp.float32
BF16 = jnp.bfloat16

D_MODEL = 1024
PAGE_SIZE = 128
MLA_HEADS = 8
Q_LORA = 384
KV_LORA = 256
NOPE_DIM = 64
ROPE_DIM = 32
QK_DIM = NOPE_DIM + ROPE_DIM
V_DIM = 64
ROPE_THETA = 10000.0
GLA_HEADS = 4
GLA_DK = 128
GLA_DV = 128
GLA_GATE_RANK = 16
GLA_TAU = 16.0
GLA_CHUNK = 64
S5_GROUP = 16
S5_GROUPS = D_MODEL // S5_GROUP
S5_STATE = 64
D_FF = 4 * D_MODEL
PLE_DIM = 256
EPS = 1e-6
IN_SPLITS = (Q_LORA, KV_LORA, ROPE_DIM, GLA_HEADS * GLA_DK, GLA_HEADS * GLA_DK,
             GLA_HEADS * GLA_DV, GLA_GATE_RANK, GLA_HEADS * GLA_DV)

LANES = 128
SUBLANES = 8
HEAD_PAD = LANES
VMEM_LIMIT = 56 * 1024 * 1024
S5_SLABS = D_MODEL // LANES
S5_SLAB_STATE = (LANES // S5_GROUP) * S5_STATE
NEG_INF = float("-inf")
LOG2_E = 1.4426950408889634


class _Tiles(NamedTuple):
    front_rows: int
    post_rows: int
    sample_rows: int
    attn_queries: int
    gla_rows: int
    s5_positions: int
    attn_chunk_pages: int
    gla_sample_batch: int


def _tiles(seq, sample_rows, n_pages):
    pick = lambda n, *sizes: next(s for s in sizes if n % s == 0)
    return _Tiles(front_rows=pick(seq, 512, 256, 128), post_rows=pick(seq, 1024, 512, 256, 128),
                  sample_rows=sample_rows // 2, attn_queries=pick(seq, 512, 256, 128),
                  gla_rows=pick(seq, 1024, 512, 256, 128, GLA_CHUNK), s5_positions=pick(seq, 64, 32, 16, 8),
                  attn_chunk_pages=pick(n_pages, 32, 16, 8, 4, 2, 1), gla_sample_batch=SUBLANES)


def _vmem_spec():
    return pl.BlockSpec(memory_space=pltpu.VMEM)


def _params(*sem):
    return pltpu.CompilerParams(dimension_semantics=sem, vmem_limit_bytes=VMEM_LIMIT)


def _bdot(a, b):
    return jnp.dot(a.astype(BF16), b.astype(BF16), preferred_element_type=F32)


def _bdot_nt(a, b):
    return lax.dot_general(a.astype(BF16), b.astype(BF16), (((1,), (1,)), ((), ())),
                           preferred_element_type=F32)


def _bdot_tn(a, b):
    return lax.dot_general(a.astype(BF16), b.astype(BF16), (((0,), (0,)), ((), ())),
                           preferred_element_type=F32)


def _rms(x, g):
    return x * lax.rsqrt(jnp.mean(x * x, axis=-1, keepdims=True) + EPS) * g


def _sigmoid(x):
    return 1.0 / (1.0 + jnp.exp(-x))


def _rope_table_kernel(cos_ref, sin_ref):
    shape = cos_ref.shape
    half = ROPE_DIM // 2
    per_row = LANES // half
    row = lax.broadcasted_iota(jnp.int32, shape, 0)
    lane = lax.broadcasted_iota(jnp.int32, shape, 1)
    pos = (row * per_row + lane // half).astype(F32)
    freq = jnp.power(F32(ROPE_THETA), -(lane % half).astype(F32) / half)
    ang = pos * freq
    cos_ref[...] = jnp.cos(ang)
    sin_ref[...] = jnp.sin(ang)


def _rope_tables(n_pos):
    half = ROPE_DIM // 2
    per_row = LANES // half
    rows = -(-n_pos // per_row)
    rows = -(-rows // SUBLANES) * SUBLANES
    cos, sin = pl.pallas_call(
        _rope_table_kernel,
        out_shape=(jax.ShapeDtypeStruct((rows, LANES), F32),) * 2,
        name="rope_table",
    )()
    cos = cos.reshape(rows * per_row, half)[:n_pos]
    sin = sin.reshape(rows * per_row, half)[:n_pos]
    return cos, sin


def _front_kernel(x_ref, tc_ref, ts_ref, gn_ref, wcq_ref, wckv_ref, wkpe_ref, wgq_ref, wgk_ref, wgv_ref,
                  wgr_ref, wga_ref, gql_ref, wuq_ref, gkvl_ref, wuk_ref, wuv_ref, gqh_ref, gkh_ref,
                  wa2_ref, ba_ref,
                  q_ref, k_ref, v_ref, ckv_ref, kpe_ref, gq_ref, gk_ref, gv_ref, lg_ref, gr_ref):
    n = _rms(x_ref[...], gn_ref[...])
    nb = n.astype(BF16)
    tc = tc_ref[...]
    ts = ts_ref[...]
    lane = lax.broadcasted_iota(jnp.int32, tc.shape, 1)
    first_half = lane < NOPE_DIM + ROPE_DIM // 2

    def norm_rope(xh, g):
        ss = jnp.sum(xh * xh, axis=-1, keepdims=True) * (1.0 / QK_DIM)
        y = xh * lax.rsqrt(ss + EPS) * g
        partner = jnp.where(first_half, pltpu.roll(y, LANES - ROPE_DIM // 2, 1), pltpu.roll(y, ROPE_DIM // 2, 1))
        return y * tc + partner * ts

    cq = _rms(jnp.dot(nb, wcq_ref[...], preferred_element_type=F32), gql_ref[...])
    q = _bdot(cq, wuq_ref[...])
    ckv = _rms(jnp.dot(nb, wckv_ref[...], preferred_element_type=F32), gkvl_ref[...])
    ckv_ref[...] = ckv
    ckvb = ckv.astype(BF16)
    kn = jnp.dot(ckvb, wuk_ref[...], preferred_element_type=F32)
    v_ref[...] = jnp.dot(ckvb, wuv_ref[...], preferred_element_type=F32).astype(v_ref.dtype)
    kpe = jnp.dot(nb, wkpe_ref[...], preferred_element_type=F32)
    kpe_ref[...] = kpe
    gqh = gqh_ref[...]
    gkh = gkh_ref[...]
    scale = QK_DIM ** -0.5 * LOG2_E
    for h in range(MLA_HEADS):
        hs = slice(h * HEAD_PAD, (h + 1) * HEAD_PAD)
        q_ref[:, hs] = (norm_rope(q[:, hs], gqh) * scale).astype(q_ref.dtype)
        k_ref[:, hs] = norm_rope(kn[:, hs] + kpe, gkh).astype(k_ref.dtype)

    gq_ref[...] = jnp.dot(nb, wgq_ref[...], preferred_element_type=F32) * (GLA_DK ** -0.5)
    gk_ref[...] = jnp.dot(nb, wgk_ref[...], preferred_element_type=F32)
    gv_ref[...] = jnp.dot(nb, wgv_ref[...], preferred_element_type=F32)
    gr_ref[...] = jnp.dot(nb, wgr_ref[...], preferred_element_type=F32)
    ga = jnp.dot(nb, wga_ref[...], preferred_element_type=F32)
    logit = _bdot(ga, wa2_ref[...]) + ba_ref[...]
    log_sig = jnp.minimum(logit, 0.0) - jnp.log1p(jnp.exp(-jnp.abs(logit)))
    lg_ref[...] = log_sig * (1.0 / GLA_TAU)


def _front_weights(w_in, g_norm, g_qlat, w_uq, g_kvlat, w_ukv, g_qh, g_kh, w_a2, b_a):
    offs = [0]
    for s in IN_SPLITS:
        offs.append(offs[-1] + s)
    cols = [w_in[:, offs[i]:offs[i + 1]] for i in range(len(IN_SPLITS))]
    w_cq, w_ckv, w_kpe, w_gq, w_gk, w_gv, w_ga, w_gr = cols
    pad_head = HEAD_PAD - QK_DIM
    w_kpe_placed = jnp.pad(w_kpe, ((0, 0), (NOPE_DIM, HEAD_PAD - QK_DIM)))
    w_ga_pad = jnp.pad(w_ga, ((0, 0), (0, LANES - GLA_GATE_RANK)))
    w_a2_pad = jnp.pad(w_a2, ((0, LANES - GLA_GATE_RANK), (0, 0)))
    w_uq_pad = jnp.pad(w_uq.reshape(Q_LORA, MLA_HEADS, QK_DIM), ((0, 0), (0, 0), (0, pad_head)))
    w_uq_pad = w_uq_pad.reshape(Q_LORA, MLA_HEADS * HEAD_PAD)
    w_ukv3 = w_ukv.reshape(KV_LORA, MLA_HEADS, NOPE_DIM + V_DIM)
    w_uk = w_ukv3[:, :, :NOPE_DIM]
    w_uv = w_ukv3[:, :, NOPE_DIM:].reshape(KV_LORA, MLA_HEADS * V_DIM)
    w_uk_pad = jnp.pad(w_uk, ((0, 0), (0, 0), (0, HEAD_PAD - NOPE_DIM))).reshape(KV_LORA, MLA_HEADS * HEAD_PAD)
    g_qh_pad = jnp.pad(g_qh, (0, pad_head))[None, :]
    g_kh_pad = jnp.pad(g_kh, (0, pad_head))[None, :]
    bf = lambda a: a.astype(BF16)
    weights = (g_norm[None, :], bf(w_cq), bf(w_ckv), bf(w_kpe_placed), bf(w_gq), bf(w_gk), bf(w_gv), bf(w_gr),
               bf(w_ga_pad), g_qlat[None, :], bf(w_uq_pad), g_kvlat[None, :], bf(w_uk_pad), bf(w_uv),
               g_qh_pad, g_kh_pad, bf(w_a2_pad), b_a[None, :])
    return weights, w_uk, w_uv


def _front(x, tc, ts, weights, *, tm, table_tiles):
    n = x.shape[0]
    hk = GLA_HEADS * GLA_DK
    row = lambda w: pl.BlockSpec((tm, w), lambda i: (i, 0))
    tab = pl.BlockSpec((tm, LANES), lambda i: (i % table_tiles, 0))
    sds = lambda w, dt: jax.ShapeDtypeStruct((n, w), dt)
    out_shape = (sds(MLA_HEADS * HEAD_PAD, BF16), sds(MLA_HEADS * HEAD_PAD, BF16), sds(MLA_HEADS * V_DIM, BF16),
                 sds(KV_LORA, F32), sds(LANES, F32), sds(hk, F32), sds(hk, F32), sds(hk, F32), sds(hk, F32),
                 sds(hk, F32))
    out_specs = (row(MLA_HEADS * HEAD_PAD), row(MLA_HEADS * HEAD_PAD), row(MLA_HEADS * V_DIM), row(KV_LORA),
                 row(LANES), row(hk), row(hk), row(hk), row(hk), row(hk))
    return pl.pallas_call(
        _front_kernel,
        grid=(n // tm,),
        in_specs=[row(D_MODEL), tab, tab] + [_vmem_spec()] * len(weights),
        out_specs=out_specs,
        out_shape=out_shape,
        compiler_params=_params("parallel"),
        name="ab_front",
    )(x, tc, ts, *weights)


def _prompt_attn_kernel(q_ref, k_ref, v_ref, o_ref, *, tq, n_q):
    i = pl.program_id(1)
    row = lax.broadcasted_iota(jnp.int32, (tq, tq), 0)
    col = lax.broadcasted_iota(jnp.int32, (tq, tq), 1)
    causal = col <= row
    lane = lax.broadcasted_iota(jnp.int32, (tq, LANES), 1)

    def case(c):
        kv = (c + 1) * tq
        outs = []
        for h in range(MLA_HEADS):
            hs = slice(h * HEAD_PAD, (h + 1) * HEAD_PAD)
            vs = slice((h // 2) * LANES, (h // 2 + 1) * LANES)
            s = lax.dot_general(q_ref[:, hs], k_ref[:kv, hs], (((1,), (1,)), ((), ())),
                                preferred_element_type=F32)
            diag = jnp.where(causal, s[:, c * tq:], NEG_INF)
            s = diag if c == 0 else jnp.concatenate([s[:, :c * tq], diag], axis=1)
            p = jnp.exp2(s - jnp.max(s, axis=-1, keepdims=True))
            l = jnp.sum(p, axis=-1, keepdims=True)
            outs.append(jnp.dot(p.astype(BF16), v_ref[:kv, vs], preferred_element_type=F32) / l)
        for pair in range(MLA_HEADS // 2):
            o_ref[:, pair * LANES:(pair + 1) * LANES] = jnp.where(
                lane < V_DIM, outs[2 * pair], outs[2 * pair + 1]).astype(o_ref.dtype)

    for c in range(n_q):
        pl.when(i == c)(functools.partial(case, c))


def _prompt_attn(q, k, v, *, batch, seq, tq):
    nq = seq // tq
    return pl.pallas_call(
        functools.partial(_prompt_attn_kernel, tq=tq, n_q=nq),
        grid=(batch, nq),
        in_specs=[pl.BlockSpec((tq, MLA_HEADS * HEAD_PAD), lambda b, i: (b * nq + i, 0)),
                  pl.BlockSpec((seq, MLA_HEADS * HEAD_PAD), lambda b, i: (b, 0)),
                  pl.BlockSpec((seq, MLA_HEADS * V_DIM), lambda b, i: (b, 0))],
        out_specs=pl.BlockSpec((tq, MLA_HEADS * V_DIM), lambda b, i: (b * nq + i, 0)),
        out_shape=jax.ShapeDtypeStruct((batch * seq, MLA_HEADS * V_DIM), BF16),
        compiler_params=_params("parallel", "arbitrary"),
        name="mla_prompt_attn",
    )(q, k, v)


def _gla_intra(q, k, b):
    c = q.shape[0]
    row = lax.broadcasted_iota(jnp.int32, q.shape, 0)
    r2 = lax.broadcasted_iota(jnp.int32, (c, c), 0)
    c2 = lax.broadcasted_iota(jnp.int32, (c, c), 1)
    att = jnp.where(r2 == c2, _bdot_nt(q, k), 0.0)
    before = pltpu.roll(b, 1, 0)
    last = b
    m = 1
    while m < c:
        upper = (row // m) % 2 == 1
        qm = jnp.where(upper, q * jnp.exp(b - before), 0.0)
        km = jnp.where(upper, 0.0, k * jnp.exp(last - b))
        same = (r2 // (2 * m)) == (c2 // (2 * m))
        att = att + jnp.where(same, _bdot_nt(qm, km), 0.0)
        if 2 * m < c:
            before = jnp.where(upper, pltpu.roll(before, m, 0), before)
            last = jnp.where(upper, last, pltpu.roll(last, c - m, 0))
        m *= 2
    return att


def _cumsum_rows(g):
    c = g.shape[0]
    row = lax.broadcasted_iota(jnp.int32, g.shape, 0)
    b = g
    sh = 1
    while sh < c:
        b = b + jnp.where(row >= sh, pltpu.roll(b, sh, 0), 0.0)
        sh *= 2
    return b


def _gla_prompt_kernel(gq_ref, gk_ref, gv_ref, lg_ref, gr_ref, go_ref, o_ref, st_ref, state_ref, *, n_chunks):
    t = pl.program_id(1)

    @pl.when(t == 0)
    def _():
        state_ref[...] = jnp.zeros_like(state_ref)

    go = go_ref[...]

    for h in range(GLA_HEADS):
        hs = slice(h * GLA_DK, (h + 1) * GLA_DK)
        st = state_ref[h]
        for ci in range(n_chunks):
            rows = slice(ci * GLA_CHUNK, (ci + 1) * GLA_CHUNK)
            q = gq_ref[rows, hs]
            k = gk_ref[rows, hs]
            v = gv_ref[rows, hs]
            b = _cumsum_rows(lg_ref[rows, hs])
            o = _bdot_nt(q * jnp.exp(b), st) + _bdot(_gla_intra(q, k, b), v)
            bl = b[GLA_CHUNK - 1:GLA_CHUNK, :]
            st = st * jnp.exp(bl) + _bdot_tn(v, k * jnp.exp(bl - b))
            gr = gr_ref[rows, hs]
            o_ref[rows, hs] = (_rms(o, go) * (gr * _sigmoid(gr))).astype(o_ref.dtype)
        state_ref[h] = st

    @pl.when(t == pl.num_programs(1) - 1)
    def _():
        st_ref[0] = state_ref[...]


def _gla_prompt(gq, gk, gv, lg, gr, g_o, *, batch, seq, tg):
    nt = seq // tg
    hk = GLA_HEADS * GLA_DK
    row = pl.BlockSpec((tg, hk), lambda b, t: (b * nt + t, 0))
    return pl.pallas_call(
        functools.partial(_gla_prompt_kernel, n_chunks=tg // GLA_CHUNK),
        grid=(batch, nt),
        in_specs=[row, row, row, row, row, _vmem_spec()],
        out_specs=(row, pl.BlockSpec((1, GLA_HEADS, GLA_DV, GLA_DK), lambda b, t: (b, 0, 0, 0))),
        out_shape=(jax.ShapeDtypeStruct((batch * seq, hk), BF16),
                   jax.ShapeDtypeStruct((batch, GLA_HEADS, GLA_DV, GLA_DK), F32)),
        scratch_shapes=[pltpu.VMEM((GLA_HEADS, GLA_DV, GLA_DK), F32)],
        compiler_params=_params("parallel", "arbitrary"),
        name="gla_prompt",
    )(gq, gk, gv, lg, gr, g_o[None, :])


def _gla_sample_kernel(gq_ref, gk_ref, gv_ref, lg_ref, gr_ref, go_ref, s0_ref, o_ref, s1_ref, *, nb, t_len):
    rows = nb * t_len
    per_tile = SUBLANES // t_len
    tok = lax.broadcasted_iota(jnp.int32, (rows, GLA_DK), 0) % t_len
    tile_batch = lax.broadcasted_iota(jnp.int32, (SUBLANES, GLA_DK), 0) // t_len
    pad = jnp.zeros((LANES - rows, LANES), F32)
    go = go_ref[...]
    for h in range(GLA_HEADS):
        hs = slice(h * GLA_DK, (h + 1) * GLA_DK)
        q = gq_ref[:, hs]
        k = gk_ref[:, hs]
        v = gv_ref[:, hs]
        b = lg_ref[:, hs]
        sh = 1
        while sh < t_len:
            b = b + jnp.where(tok >= sh, pltpu.roll(b, sh, 0), 0.0)
            sh *= 2
        bl = jnp.where(tok == t_len - 1, b, 0.0)
        for d in range(1, t_len):
            bl = bl + jnp.where(tok == t_len - 1 - d, pltpu.roll(b, rows - d, 0), 0.0)
        qe = q * jnp.exp(b)
        kd = k * jnp.exp(bl - b)
        o = jnp.sum(q * k, axis=-1, keepdims=True) * v
        for d in range(1, t_len):
            w = jnp.sum(q * pltpu.roll(k, d, 0) * jnp.exp(b - pltpu.roll(b, d, 0)), axis=-1, keepdims=True)
            o = o + jnp.where(tok[:, :1] >= d, w, 0.0) * pltpu.roll(v, d, 0)
        decay_cols = jnp.transpose(jnp.concatenate([jnp.exp(bl), pad], axis=0))
        inter = []
        for tile in range(rows // SUBLANES):
            ts = slice(tile * SUBLANES, (tile + 1) * SUBLANES)
            acc = jnp.zeros((SUBLANES, GLA_DV), F32)
            for e in range(per_tile):
                bi = tile * per_tile + e
                s0 = s0_ref[bi, h]
                mine = tile_batch == e
                acc = jnp.where(mine, _bdot(qe[ts], s0), acc)
                r_last = bi * t_len + t_len - 1
                s1_ref[bi, h] = s0 * decay_cols[:, r_last:r_last + 1] + _bdot_tn(jnp.where(mine, kd[ts], 0.0), v[ts])
            inter.append(acc)
        o = o + jnp.concatenate(inter, axis=0)
        gr = gr_ref[:, hs]
        o_ref[:, hs] = (_rms(o, go) * (gr * _sigmoid(gr))).astype(o_ref.dtype)


def _gla_sample(gq, gk, gv, lg, gr, g_o, s0, *, t_len, nb):
    n = gq.shape[0]
    batch = n // t_len
    hk = GLA_HEADS * GLA_DK
    rows = nb * t_len
    row = pl.BlockSpec((rows, hk), lambda i: (i, 0))
    st = pl.BlockSpec((nb, GLA_HEADS, GLA_DK, GLA_DV), lambda i: (i, 0, 0, 0))
    return pl.pallas_call(
        functools.partial(_gla_sample_kernel, nb=nb, t_len=t_len),
        grid=(batch // nb,),
        in_specs=[row, row, row, row, row, _vmem_spec(), st],
        out_specs=(row, st),
        out_shape=(jax.ShapeDtypeStruct((n, hk), BF16), jax.ShapeDtypeStruct(s0.shape, s0.dtype)),
        compiler_params=_params("parallel"),
        name="gla_sample",
    )(gq, gk, gv, lg, gr, g_o[None, :], s0)


def _sample_attn_kernel(pt_ref, q_ref, cnew_ref, pnew_ref, kc_ref, ks_ref, kcn_ref, ksn_ref, gk_ref, gpe_ref,
                        wukt_ref, wukp_ref, wuv_ref, ckv_hbm, kpe_hbm, o_ref, ckv_buf, kpe_buf, sem,
                        *, n_pages, chunk_pages, t_len):
    n_cols = t_len * MLA_HEADS
    chunk = chunk_pages * PAGE_SIZE
    n_chunks = n_pages // chunk_pages
    past = n_pages * PAGE_SIZE
    b = pl.program_id(0)
    slot = b % 2

    def page_copies(batch, sl, i):
        page = pt_ref[batch, i]
        ci = i // chunk_pages
        return (pltpu.make_async_copy(ckv_hbm.at[page], ckv_buf.at[sl, i], sem.at[sl, ci, 0]),
                pltpu.make_async_copy(kpe_hbm.at[page], kpe_buf.at[sl, i], sem.at[sl, ci, 1]))

    def start_pages(batch, sl, pages):
        for i in pages:
            for cp in page_copies(batch, sl, i):
                cp.start(priority=i % 2)

    def wait_pages(batch, sl, pages):
        for i in pages:
            for cp in page_copies(batch, sl, i):
                cp.wait()

    @pl.when(b == 0)
    def _():
        start_pages(0, 0, range(n_pages))

    qg = q_ref[...].astype(F32) * gk_ref[...]
    gpe = gpe_ref[...]
    sub = lax.broadcasted_iota(jnp.int32, (SUBLANES, MLA_HEADS * HEAD_PAD), 0)
    lane_head = lax.broadcasted_iota(jnp.int32, (SUBLANES, MLA_HEADS * HEAD_PAD), 1) // HEAD_PAD
    qexp = jnp.concatenate(
        [jnp.where(lane_head == sub, jnp.broadcast_to(qg[t:t + 1, :], sub.shape), 0.0) for t in range(t_len)],
        axis=0)
    qt = _bdot(qexp, wukt_ref[...])
    lhs = jnp.concatenate([wukp_ref[...], qt[:, :KV_LORA].astype(BF16)], axis=0)
    qr = qt[:, KV_LORA:KV_LORA + ROPE_DIM].astype(BF16)
    n_norm = NOPE_DIM * MLA_HEADS
    half = ROPE_DIM // 2

    def scores(c, kt, kcos, ksin):
        n = c.shape[0]
        cb = c.astype(BF16)
        r = lax.dot_general(lhs, cb, (((1,), (1,)), ((), ())), preferred_element_type=F32)
        kn = r[:n_norm].reshape(NOPE_DIM, MLA_HEADS, n)
        ss = jnp.sum(kn * kn, axis=0) + jnp.sum(kt * kt, axis=0, keepdims=True)
        rs = lax.rsqrt(ss * (1.0 / QK_DIM) + EPS)
        kg = kt * gpe
        kr = kg * kcos + jnp.concatenate([kg[half:], kg[:half]], axis=0) * ksin
        s = r[n_norm:] + jnp.dot(qr, kr.astype(BF16), preferred_element_type=F32)
        s = (s.reshape(t_len, MLA_HEADS, n) * rs[None]).reshape(n_cols, n)
        return cb, s

    def fold(state, cb, s):
        m, l, acc = state
        m_new = jnp.maximum(m, jnp.max(s, axis=1, keepdims=True))
        alpha = jnp.exp2(m - m_new)
        p = jnp.exp2(s - m_new)
        l = alpha * l + jnp.sum(p, axis=1, keepdims=True)
        acc = alpha * acc + jnp.dot(p.astype(BF16), cb, preferred_element_type=F32)
        return m_new, l, acc

    @pl.when(b + 1 < pl.num_programs(0))
    def _():
        start_pages(b + 1, 1 - slot, range(n_pages))

    wait_pages(b, slot, range(n_pages))
    state = (jnp.full((n_cols, 1), NEG_INF, F32), jnp.zeros((n_cols, 1), F32), jnp.zeros((n_cols, KV_LORA), F32))
    for ci in range(n_chunks):
        pages = range(ci * chunk_pages, (ci + 1) * chunk_pages)
        c = ckv_buf[slot, pl.ds(ci * chunk_pages, chunk_pages)].reshape(chunk, KV_LORA)
        kt = jnp.concatenate([kpe_buf[slot, p] for p in pages], axis=1)
        keys = slice(ci * chunk, (ci + 1) * chunk)
        state = fold(state, *scores(c, kt, kc_ref[:, keys], ks_ref[:, keys]))

    zpad = lambda a: jnp.concatenate([a, jnp.zeros((PAGE_SIZE - t_len, a.shape[1]), a.dtype)], axis=0)
    pnew = jnp.concatenate([zpad(pnew_ref[...]), jnp.zeros((PAGE_SIZE, LANES - ROPE_DIM), F32)], axis=1)
    cb, s = scores(zpad(cnew_ref[...]), jnp.transpose(pnew)[:ROPE_DIM], kcn_ref[...], ksn_ref[...])
    r2 = lax.broadcasted_iota(jnp.int32, (n_cols, PAGE_SIZE), 0)
    c2 = lax.broadcasted_iota(jnp.int32, (n_cols, PAGE_SIZE), 1)
    s = jnp.where((c2 < t_len) & (c2 <= r2 // MLA_HEADS), s, NEG_INF)
    _, l, acc = fold(state, cb, s)

    lat = acc / l
    res = _bdot(lat, wuv_ref[...])
    r3 = lax.broadcasted_iota(jnp.int32, res.shape, 0)
    c3 = lax.broadcasted_iota(jnp.int32, res.shape, 1)
    res = jnp.where(c3 // V_DIM == r3 % MLA_HEADS, res, 0.0)
    o_ref[...] = jnp.sum(res.reshape(t_len, MLA_HEADS, MLA_HEADS * V_DIM), axis=1).astype(o_ref.dtype)


def _sample_attn(q, ckv_new, kpe_new, cache_ckv, cache_kpe, page_table, kc, ks, g_kh, w_uk, w_uv,
                 *, t_len, chunk_pages):
    batch, n_pages = page_table.shape
    past = n_pages * PAGE_SIZE
    w_ukt = jnp.pad(jnp.transpose(w_uk, (1, 2, 0)), ((0, 0), (0, HEAD_PAD - NOPE_DIM), (0, 0)))
    w_ukt = w_ukt.reshape(MLA_HEADS * HEAD_PAD, KV_LORA)
    sel = jnp.pad(jnp.eye(ROPE_DIM, dtype=F32), ((NOPE_DIM, HEAD_PAD - QK_DIM), (0, LANES - ROPE_DIM)))
    w_ukt_ext = jnp.concatenate([w_ukt, jnp.tile(sel, (MLA_HEADS, 1))], axis=1).astype(BF16)
    w_uk_perm = jnp.transpose(w_uk, (2, 1, 0)).reshape(NOPE_DIM * MLA_HEADS, KV_LORA).astype(BF16)
    g_k_pad = jnp.concatenate([g_kh[:NOPE_DIM], jnp.ones((ROPE_DIM,), F32), jnp.zeros((HEAD_PAD - QK_DIM,), F32)])
    g_k_pad = jnp.tile(g_k_pad, MLA_HEADS)[None, :]
    g_pe = g_kh[NOPE_DIM:][:, None]
    kc, ks = jnp.transpose(kc), jnp.transpose(ks)
    kcn = jnp.pad(kc[:, past:past + t_len], ((0, 0), (0, PAGE_SIZE - t_len)))
    ksn = jnp.pad(ks[:, past:past + t_len], ((0, 0), (0, PAGE_SIZE - t_len)))
    per_b = lambda w: pl.BlockSpec((None, t_len, w), lambda b, pt: (b, 0, 0))
    hbm = pl.BlockSpec(memory_space=pl.ANY)
    in_specs = ([per_b(MLA_HEADS * HEAD_PAD), per_b(KV_LORA), per_b(ROPE_DIM)] + [_vmem_spec()] * 9 + [hbm, hbm])
    grid_spec = pltpu.PrefetchScalarGridSpec(
        num_scalar_prefetch=1,
        grid=(batch,),
        in_specs=in_specs,
        out_specs=pl.BlockSpec((None, t_len, MLA_HEADS * V_DIM), lambda b, pt: (b, 0, 0)),
        scratch_shapes=[pltpu.VMEM((2, n_pages, PAGE_SIZE, KV_LORA), F32),
                        pltpu.VMEM((2, n_pages, ROPE_DIM, PAGE_SIZE), F32),
                        pltpu.SemaphoreType.DMA((2, n_pages // chunk_pages, 2))],
    )
    out = pl.pallas_call(
        functools.partial(_sample_attn_kernel, n_pages=n_pages, chunk_pages=chunk_pages, t_len=t_len),
        grid_spec=grid_spec,
        out_shape=jax.ShapeDtypeStruct((batch, t_len, MLA_HEADS * V_DIM), BF16),
        compiler_params=_params("arbitrary"),
        name="mla_sample_attn",
    )(page_table, q.reshape(batch, t_len, -1), ckv_new.reshape(batch, t_len, -1),
      kpe_new.reshape(batch, t_len, -1), kc[:, :past], ks[:, :past], kcn, ksn, g_k_pad, g_pe,
      w_ukt_ext, w_uk_perm, w_uv.astype(BF16), cache_ckv, jnp.swapaxes(cache_kpe, 1, 2))
    return out.reshape(batch * t_len, MLA_HEADS * V_DIM)


def _mlp_ple(h, p_ref, gm_ref, wup_ref, wdown_ref, gp_ref, wgate_ref, wproj_ref, o_ref, ff_chunk):
    nb = _rms(h, gm_ref[...]).astype(BF16)
    y = jnp.zeros_like(h)
    for c in range(D_FF // ff_chunk):
        cs = slice(c * ff_chunk, (c + 1) * ff_chunk)
        a = jnp.maximum(jnp.dot(nb, wup_ref[:, cs], preferred_element_type=F32), 0.0)
        y = y + jnp.dot((a * a).astype(BF16), wdown_ref[cs, :], preferred_element_type=F32)
    h = h + y
    gate = _sigmoid(_bdot(_rms(h, gp_ref[...]), wgate_ref[...]))
    o_ref[...] = h + gate * _bdot(p_ref[...], wproj_ref[...])


def _post_mix_kernel(h_ref, a_ref, o2_ref, p_ref, wout_ref, gm_ref, wup_ref, wdown_ref, gp_ref, wgate_ref,
                     wproj_ref, o_ref, *, ff_chunk):
    n_attn = MLA_HEADS * V_DIM
    h = (h_ref[...] + jnp.dot(a_ref[...], wout_ref[:n_attn, :], preferred_element_type=F32)
         + jnp.dot(o2_ref[...], wout_ref[n_attn:, :], preferred_element_type=F32))
    _mlp_ple(h, p_ref, gm_ref, wup_ref, wdown_ref, gp_ref, wgate_ref, wproj_ref, o_ref, ff_chunk)


def _post_kernel(h_ref, p_ref, gm_ref, wup_ref, wdown_ref, gp_ref, wgate_ref, wproj_ref, o_ref, *, ff_chunk):
    _mlp_ple(h_ref[...], p_ref, gm_ref, wup_ref, wdown_ref, gp_ref, wgate_ref, wproj_ref, o_ref, ff_chunk)


def _post(h, p_layers, layer, weights, mix=None, *, tm, ff_chunk=1024):
    n = h.shape[0]
    row = lambda w: pl.BlockSpec((tm, w), lambda i: (i, 0))
    p_spec = pl.BlockSpec((None, tm, PLE_DIM), lambda i: (layer, i, 0))
    if mix is None:
        body, acts, specs = _post_kernel, (h, p_layers), [row(D_MODEL), p_spec]
    else:
        attn, o_gla, w_out = mix
        body, acts = _post_mix_kernel, (h, attn, o_gla, p_layers)
        specs = [row(D_MODEL), row(attn.shape[1]), row(o_gla.shape[1]), p_spec]
        weights = (w_out,) + tuple(weights)
    return pl.pallas_call(
        functools.partial(body, ff_chunk=ff_chunk),
        grid=(n // tm,),
        in_specs=specs + [_vmem_spec()] * len(weights),
        out_specs=row(D_MODEL),
        out_shape=jax.ShapeDtypeStruct((n, D_MODEL), F32),
        compiler_params=_params("parallel"),
        name="post_mix" if mix is not None else "post",
    )(*acts, *weights)


def _s5_prep_kernel(are_ref, aim_ref, ldt_ref, bre_ref, bim_ref, lr_ref, li_ref, bbr_ref, bbi_ref):
    a_re = are_ref[...]
    a_im = aim_ref[...]
    dt = jnp.exp(ldt_ref[...])
    mag = jnp.exp(a_re * dt)
    lr = mag * jnp.cos(a_im * dt)
    li = mag * jnp.sin(a_im * dt)
    den = a_re * a_re + a_im * a_im
    cr = ((lr - 1.0) * a_re + li * a_im) / den
    ci = (li * a_re - (lr - 1.0) * a_im) / den
    lr_ref[...] = lr
    li_ref[...] = li
    b_re = bre_ref[...]
    b_im = bim_ref[...]
    bbr_ref[...] = cr[:, None, :] * b_re - ci[:, None, :] * b_im
    bbi_ref[...] = cr[:, None, :] * b_im + ci[:, None, :] * b_re


def _s5_weights(a_re, a_im, log_dt, b_re, b_im, c_re, c_im):
    g, p_, ch = S5_GROUPS, S5_STATE, S5_GROUP
    gp = jax.ShapeDtypeStruct((g, p_), F32)
    gcp = jax.ShapeDtypeStruct((g, ch, p_), F32)
    lr, li, bbr, bbi = pl.pallas_call(
        _s5_prep_kernel, out_shape=(gp, gp, gcp, gcp), name="s5_prep",
    )(a_re, a_im, log_dt[:, None], jnp.transpose(b_re, (0, 2, 1)), jnp.transpose(b_im, (0, 2, 1)))
    per_slab = LANES // ch
    eye = jnp.eye(per_slab, dtype=F32)

    def b_blocks(bb):
        bb = bb.reshape(S5_SLABS, per_slab, ch, p_)
        return jnp.einsum('jgcp,gh->jgchp', bb, eye).reshape(S5_SLABS, LANES, S5_SLAB_STATE)

    def c_blocks(cc):
        cc = cc.reshape(S5_SLABS, per_slab, ch, p_)
        return jnp.einsum('jgcp,gh->jgphc', cc, eye).reshape(S5_SLABS, S5_SLAB_STATE, LANES)

    w_b = jnp.concatenate([b_blocks(bbr), b_blocks(bbi)], axis=2).astype(BF16)
    w_c = jnp.concatenate([c_blocks(c_re), -c_blocks(c_im)], axis=1).astype(BF16)
    lam = jnp.stack([lr.reshape(S5_SLABS, S5_SLAB_STATE), li.reshape(S5_SLABS, S5_SLAB_STATE)], axis=1)
    return w_b, w_c, lam


def _s5_kernel(x_ref, x0r_ref, x0i_ref, gn_ref, wb_ref, wc_ref, lam_ref, d_ref, wglu_ref,
               o_ref, xr_ref, xi_ref, bur_ref, bui_ref, sr_ref, si_ref, shuf_ref=None, *, nb, t_len):
    step = pl.program_id(0)
    ns = S5_SLAB_STATE

    @pl.when(step == 0)
    def _():
        for j in range(S5_SLABS):
            sr_ref[j] = x0r_ref[:, j * ns:(j + 1) * ns]
            si_ref[j] = x0i_ref[:, j * ns:(j + 1) * ns]

    if shuf_ref is None:
        x = x_ref[...]
    else:
        for bi in range(nb):
            for s in range(D_MODEL // LANES):
                shuf_ref[s, pl.ds(bi, t_len, stride=nb), :] = x_ref[bi, :, s * LANES:(s + 1) * LANES]
        x = jnp.concatenate([shuf_ref[s] for s in range(D_MODEL // LANES)], axis=1)
    u = _rms(x, gn_ref[...])
    ub = u.astype(BF16)
    ys = []
    for j in range(S5_SLABS):
        bu = jnp.dot(ub[:, j * LANES:(j + 1) * LANES], wb_ref[j], preferred_element_type=F32)
        bur_ref[j] = bu[:, :ns]
        bui_ref[j] = bu[:, ns:]
        lr = jnp.broadcast_to(lam_ref[j, 0:1, :], (SUBLANES, ns))
        li = jnp.broadcast_to(lam_ref[j, 1:2, :], (SUBLANES, ns))
        for gi in range(nb // SUBLANES):
            srow = slice(gi * SUBLANES, (gi + 1) * SUBLANES)
            xr = sr_ref[j, srow, :]
            xi = si_ref[j, srow, :]
            for t in range(t_len):
                rows = slice(t * nb + gi * SUBLANES, t * nb + (gi + 1) * SUBLANES)
                xr, xi = (lr * xr - li * xi + bur_ref[j, rows, :], lr * xi + li * xr + bui_ref[j, rows, :])
                bur_ref[j, rows, :] = xr
                bui_ref[j, rows, :] = xi
            sr_ref[j, srow, :] = xr
            si_ref[j, srow, :] = xi
        xs = jnp.concatenate([bur_ref[j].astype(BF16), bui_ref[j].astype(BF16)], axis=1)
        ys.append(jnp.dot(xs, wc_ref[j], preferred_element_type=F32))
    y = jnp.concatenate(ys, axis=1) + d_ref[...] * u
    gl = jax.nn.gelu(y)
    zg = _bdot(gl, wglu_ref[...])
    h = x + zg[:, :D_MODEL] * _sigmoid(zg[:, D_MODEL:])
    if shuf_ref is None:
        o_ref[...] = h
    else:
        for s in range(D_MODEL // LANES):
            shuf_ref[s] = h[:, s * LANES:(s + 1) * LANES]
        for bi in range(nb):
            for s in range(D_MODEL // LANES):
                o_ref[bi, :, s * LANES:(s + 1) * LANES] = shuf_ref[s, pl.ds(bi, t_len, stride=nb), :]

    @pl.when(step == pl.num_programs(0) - 1)
    def _():
        for j in range(S5_SLABS):
            xr_ref[:, j * ns:(j + 1) * ns] = sr_ref[j]
            xi_ref[:, j * ns:(j + 1) * ns] = si_ref[j]


def _s5(x, x0r, x0i, g_norm, w_b, w_c, lam, d, w_glu, *, nb, t_tile, batch_major):
    n = x.shape[0] * x.shape[1] if batch_major else x.shape[0]
    rows = nb * t_tile
    n_state = S5_GROUPS * S5_STATE
    full = lambda shape: pl.BlockSpec(shape, lambda i: (0,) * len(shape))
    st = jax.ShapeDtypeStruct((nb, n_state), F32)
    scratch = [pltpu.VMEM((S5_SLABS, rows, S5_SLAB_STATE), F32),
               pltpu.VMEM((S5_SLABS, rows, S5_SLAB_STATE), F32),
               pltpu.VMEM((S5_SLABS, nb, S5_SLAB_STATE), F32),
               pltpu.VMEM((S5_SLABS, nb, S5_SLAB_STATE), F32)]
    if batch_major:
        x_spec = out_spec = pl.BlockSpec((nb, t_tile, D_MODEL), lambda i: (0, i, 0))
        out_sds = jax.ShapeDtypeStruct((nb, n // nb, D_MODEL), F32)
        scratch.append(pltpu.VMEM((D_MODEL // LANES, rows, LANES), F32))
    else:
        x_spec = out_spec = pl.BlockSpec((rows, D_MODEL), lambda i: (i, 0))
        out_sds = jax.ShapeDtypeStruct((n, D_MODEL), F32)
    return pl.pallas_call(
        functools.partial(_s5_kernel, nb=nb, t_len=t_tile),
        grid=(n // rows,),
        in_specs=[x_spec, full((nb, n_state)), full((nb, n_state))] + [_vmem_spec()] * 6,
        out_specs=(out_spec, full((nb, n_state)), full((nb, n_state))),
        out_shape=(out_sds, st, st),
        scratch_shapes=scratch,
        compiler_params=_params("arbitrary"),
        name="s5_layer",
    )(x, x0r, x0i, g_norm[None, :], w_b, w_c, lam, d[None, :], w_glu.astype(BF16))


def _to_time_major(h, batch, seq):
    return h.reshape(batch, seq, -1).transpose(1, 0, 2).reshape(batch * seq, -1)


def _to_batch_major(h, batch, seq):
    return h.reshape(seq, batch, -1).transpose(1, 0, 2).reshape(batch * seq, -1)


def kernel(x_prompt, x_sample, cache_ckv, cache_kpe, state_gla, state_s5_re, state_s5_im, page_table,
           p_prompt, p_sample, g_norm_ab, w_in_ab, g_qlat, w_uq, g_kvlat, w_ukv, g_qh, g_kh,
           w_gla_a2, b_gla_a, g_gla_o, w_out_ab, g_norm_c, s5_a_re, s5_a_im, s5_log_dt,
           s5_b_re, s5_b_im, s5_c_re, s5_c_im, s5_d, w_glu, g_norm_mlp, w_up, w_down,
           g_norm_ple, w_ple_gate, w_ple_proj):
    bp, sp, _ = x_prompt.shape
    bs, ts_len, _ = x_sample.shape
    n_pages = page_table.shape[1]
    past = n_pages * PAGE_SIZE
    n_p, n_s = bp * sp, bs * ts_len
    bf = lambda a: a.astype(BF16)

    tiles = _tiles(sp, n_s, n_pages)
    n_layers = p_prompt.shape[0]
    pp = p_prompt.reshape(n_layers, n_p, PLE_DIM)
    ps = p_sample.reshape(n_layers, n_s, PLE_DIM)

    cos16, sin16 = _rope_tables(past + ts_len)
    kc = jnp.concatenate([cos16, cos16], axis=1)
    ks = jnp.concatenate([-sin16, sin16], axis=1)

    def head_tables(rows):
        c, s = cos16[rows], sin16[rows]
        ones = jnp.ones((c.shape[0], NOPE_DIM), F32)
        tail = jnp.zeros((c.shape[0], HEAD_PAD - QK_DIM), F32)
        return (jnp.concatenate([ones, c, c, tail + 1.0], axis=1),
                jnp.concatenate([ones * 0.0, -s, s, tail], axis=1))

    tc, ts = head_tables(slice(0, sp))
    tc_s, ts_s = (jnp.tile(a, (bs, 1)) for a in head_tables(slice(past, past + ts_len)))

    post_w = lambda i: (g_norm_mlp[i][None, :], bf(w_up[i]), bf(w_down[i]), g_norm_ple[i][None, :],
                        bf(w_ple_gate[i]), bf(w_ple_proj[i]))

    fw, w_uk, w_uv = _front_weights(w_in_ab[0], g_norm_ab[0], g_qlat[0], w_uq[0], g_kvlat[0], w_ukv[0],
                                    g_qh[0], g_kh[0], w_gla_a2[0], b_gla_a[0])
    (q_p, k_p, v_p, ckv_p, kpe_p, gq_p, gk_p, gv_p, lg_p, gr_p) = _front(
        x_prompt.reshape(n_p, D_MODEL), tc, ts, fw, tm=tiles.front_rows, table_tiles=sp // tiles.front_rows)
    (q_s, _, _, ckv_s, kpe_s, gq_s, gk_s, gv_s, lg_s, gr_s) = _front(
        x_sample.reshape(n_s, D_MODEL), tc_s, ts_s, fw, tm=tiles.sample_rows, table_tiles=n_s // tiles.sample_rows)
    kpe_p = kpe_p[:, NOPE_DIM:QK_DIM]
    kpe_s = kpe_s[:, NOPE_DIM:QK_DIM]

    attn_p = _prompt_attn(q_p, k_p, v_p, batch=bp, seq=sp, tq=tiles.attn_queries)
    o_p, gla_p = _gla_prompt(gq_p, gk_p, gv_p, lg_p, gr_p, g_gla_o[0], batch=bp, seq=sp, tg=tiles.gla_rows)
    gla_p = jnp.swapaxes(gla_p, -1, -2)

    attn_s = _sample_attn(q_s, ckv_s, kpe_s, cache_ckv[0], cache_kpe[0], page_table, kc, ks, g_kh[0],
                          w_uk, w_uv, t_len=ts_len, chunk_pages=tiles.attn_chunk_pages)
    o_s, gla_s = _gla_sample(gq_s, gk_s, gv_s, lg_s, gr_s, g_gla_o[0], state_gla[0], t_len=ts_len,
                             nb=tiles.gla_sample_batch)

    w_out = bf(w_out_ab[0])
    hp = _post(x_prompt.reshape(n_p, D_MODEL), pp, 0, post_w(0), mix=(attn_p, o_p, w_out), tm=tiles.post_rows)
    hs = _post(x_sample.reshape(n_s, D_MODEL), ps, 0, post_w(0), mix=(attn_s, o_s, w_out), tm=tiles.sample_rows)

    w_b, w_c, lam = _s5_weights(s5_a_re[0], s5_a_im[0], s5_log_dt[0], s5_b_re[0], s5_b_im[0],
                                s5_c_re[0], s5_c_im[0])
    n_state = S5_GROUPS * S5_STATE
    zero_state = jnp.zeros((bp, n_state), F32)
    hp, s5r_p, s5i_p = _s5(hp.reshape(bp, sp, D_MODEL), zero_state, zero_state, g_norm_c[0], w_b, w_c, lam,
                           s5_d[0], w_glu[0], nb=bp, t_tile=tiles.s5_positions, batch_major=True)
    hs_tm, s5r_s, s5i_s = _s5(_to_time_major(hs, bs, ts_len), state_s5_re[0].reshape(bs, n_state),
                              state_s5_im[0].reshape(bs, n_state), g_norm_c[0], w_b, w_c, lam,
                              s5_d[0], w_glu[0], nb=bs, t_tile=ts_len, batch_major=False)
    hp = _post(hp.reshape(n_p, D_MODEL), pp, 1, post_w(1), tm=tiles.post_rows)
    hs = _post(_to_batch_major(hs_tm, bs, ts_len), ps, 1, post_w(1), tm=tiles.sample_rows)

    st = lambda a, b: a.reshape(1, b, S5_GROUPS, S5_STATE)
    return (hp.reshape(bp, sp, D_MODEL), hs.reshape(bs, ts_len, D_MODEL),
            ckv_p.reshape(1, bp, sp, KV_LORA), kpe_p.reshape(1, bp, sp, ROPE_DIM), gla_p[None],
            st(s5r_p, bp), st(s5i_p, bp),
            ckv_s.reshape(1, bs, ts_len, KV_LORA), kpe_s.reshape(1, bs, ts_len, ROPE_DIM), gla_s[None],
            st(s5r_s, bs), st(s5i_s, bs))
```
